```python
import jax, jax.numpy as jnp
from jax import lax
import numpy as np

D_MODEL = 1024
BATCH = 2
SEQ = 8192
DEPTH = 2

GRID_W = 64
CTX_LEN = 256
N_EVEN = (DEPTH + 1) // 2
N_ODD = DEPTH // 2

MLA_HEADS = 8
QK_NOPE_DIM = 64
QK_ROPE_DIM = 32
QK_HEAD_DIM = QK_NOPE_DIM + QK_ROPE_DIM
V_HEAD_DIM = 64
Q_LORA_RANK = 384
KV_LORA_RANK = 256
QK_SCALE = QK_HEAD_DIM ** -0.5
ROPE_BASE = 10000.0
BLOCK_Q = 128
CONV_GROUPS = 8
CONV_GROUP_DIM = 64
CONV_DIM = CONV_GROUPS * CONV_GROUP_DIM
CONV_WIDTH = 3
CONV_OFFSET = Q_LORA_RANK + KV_LORA_RANK + QK_ROPE_DIM
IN_PROJ_DIM = CONV_OFFSET + 3 * CONV_DIM
MIX_OUT_DIM = MLA_HEADS * V_HEAD_DIM + CONV_DIM
FOURIER_GROUPS = 4
FOURIER_GROUP_DIM = D_MODEL // FOURIER_GROUPS
D_FF = ((8 * D_MODEL + 3 * 256 - 1) // (3 * 256)) * 256
EPS = 1e-6

kernel_name = "hybrid_mla_shortconv_fnet_dit_block"


def rmsnorm(x, g):
    xf = x.astype(jnp.float32)
    y = xf * lax.rsqrt(jnp.mean(xf * xf, axis=-1, keepdims=True) + EPS)
    return (y * g.astype(jnp.float32)).astype(x.dtype)


def axial_angles(T):
    rows = T // GRID_W
    row = jnp.broadcast_to(jnp.arange(rows)[:, None], (rows, GRID_W)).reshape(-1).astype(jnp.float32)
    col = jnp.broadcast_to(jnp.arange(GRID_W)[None, :], (rows, GRID_W)).reshape(-1).astype(jnp.float32)
    half = QK_ROPE_DIM // 2
    inv = 1.0 / (ROPE_BASE ** (jnp.arange(0, half, 2, dtype=jnp.float32) / half))
    return row[:, None] * inv, col[:, None] * inv


def rotate(x, ang):
    cos = jnp.cos(ang)[:, None, :].astype(x.dtype)
    sin = jnp.sin(ang)[:, None, :].astype(x.dtype)
    x1, x2 = jnp.split(x, 2, axis=-1)
    return jnp.concatenate([x1 * cos - x2 * sin, x2 * cos + x1 * sin], axis=-1)


def axial_rope(x, ang_row, ang_col):
    half = QK_ROPE_DIM // 2
    return jnp.concatenate([rotate(x[..., :half], ang_row), rotate(x[..., half:], ang_col)], axis=-1)


def mla_qkv(proj, q_norm_g, kv_norm_g, w_uq, w_ukv, q_gain, k_gain, angles):
    B, T, _ = proj.shape
    cq = proj[..., :Q_LORA_RANK]
    ckv = proj[..., Q_LORA_RANK:Q_LORA_RANK + KV_LORA_RANK]
    k_pe = proj[..., Q_LORA_RANK + KV_LORA_RANK:CONV_OFFSET]
    q = (rmsnorm(cq, q_norm_g) @ w_uq).reshape(B, T, MLA_HEADS, QK_HEAD_DIM)
    kv = (rmsnorm(ckv, kv_norm_g) @ w_ukv).reshape(B, T, MLA_HEADS, QK_NOPE_DIM + V_HEAD_DIM)
    k_nope, v = kv[..., :QK_NOPE_DIM], kv[..., QK_NOPE_DIM:]
    k = jnp.concatenate([k_nope, jnp.broadcast_to(k_pe[:, :, None, :], (B, T, MLA_HEADS, QK_ROPE_DIM))], axis=-1)
    q = rmsnorm(q, q_gain)
    k = rmsnorm(k, k_gain)
    if angles is not None:
        ang_row, ang_col = angles
        q = jnp.concatenate([q[..., :QK_NOPE_DIM], axial_rope(q[..., QK_NOPE_DIM:], ang_row, ang_col)], axis=-1)
        k = jnp.concatenate([k[..., :QK_NOPE_DIM], axial_rope(k[..., QK_NOPE_DIM:], ang_row, ang_col)], axis=-1)
    return q, k, v


def attend(q, k, v):
    s = jnp.einsum('bqhd,bkhd->bhqk', q, k, preferred_element_type=jnp.float32) * QK_SCALE
    p = jax.nn.softmax(s, axis=-1)
    return jnp.einsum('bhqk,bkhd->bqhd', p.astype(v.dtype), v)


def attend_blocked(q, k, v):
    B, S, H, D = q.shape
    nb = S // BLOCK_Q
    qb = q.reshape(B, nb, BLOCK_Q, H, D).transpose(1, 0, 2, 3, 4)
    ob = lax.map(lambda qi: attend(qi, k, v), qb)
    return ob.transpose(1, 0, 2, 3, 4).reshape(B, S, H * V_HEAD_DIM)


def short_conv(p, conv_w):
    b_gate, c_gate, u = jnp.split(p, 3, axis=-1)
    z = c_gate * u
    T = z.shape[1]
    zp = jnp.pad(z, ((0, 0), (1, 1), (0, 0)))
    y = zp[:, 0:T] * conv_w[0] + zp[:, 1:T + 1] * conv_w[1] + zp[:, 2:T + 2] * conv_w[2]
    return b_gate * y


def even_mixer(h_lat, h_ctx, w_in, q_norm_g, kv_norm_g, w_uq, w_ukv, q_gain, k_gain, conv_w, w_o, with_ctx_out):
    B, S, _ = h_lat.shape
    L = h_ctx.shape[1]
    p_lat = h_lat @ w_in
    p_ctx = h_ctx @ w_in
    q_l, k_l, v_l = mla_qkv(p_lat, q_norm_g, kv_norm_g, w_uq, w_ukv, q_gain, k_gain, axial_angles(S))
    q_c, k_c, v_c = mla_qkv(p_ctx, q_norm_g, kv_norm_g, w_uq, w_ukv, q_gain, k_gain, None)
    k_all = jnp.concatenate([k_l, k_c], axis=1)
    v_all = jnp.concatenate([v_l, v_c], axis=1)
    a_l = attend_blocked(q_l, k_all, v_all)
    s_l = short_conv(p_lat[..., CONV_OFFSET:], conv_w)
    out_l = jnp.concatenate([a_l, s_l], axis=-1) @ w_o
    if not with_ctx_out:
        return out_l, None
    a_c = attend(q_c, k_c, v_c).reshape(B, L, MLA_HEADS * V_HEAD_DIM)
    s_c = short_conv(p_ctx[..., CONV_OFFSET:], conv_w)
    out_c = jnp.concatenate([a_c, s_c], axis=-1) @ w_o
    return out_l, out_c


def fourier_mixer(h, w_f):
    B, T, _ = h.shape
    hg = h.astype(jnp.float32).reshape(B, T, FOURIER_GROUPS, FOURIER_GROUP_DIM)
    f = jnp.fft.fft2(hg, axes=(1, 3), norm="ortho").real
    return f.reshape(B, T, D_MODEL).astype(h.dtype) @ w_f


def swiglu(h, w1, w3, w2):
    return (jax.nn.silu(h @ w1) * (h @ w3)) @ w2


def ada(cvec, w, b):
    return jnp.split(jax.nn.silu(cvec) @ w + b, 6, axis=-1)


def setup_inputs(seed: int = 0) -> dict:
    key = jax.random.key(seed)
    ks = jax.random.split(key, 24)
    nrm = jax.random.normal
    f32 = jnp.float32
    D = D_MODEL
    return {
        "x": nrm(ks[0], (BATCH, SEQ, D), f32),
        "c": nrm(ks[1], (BATCH, D), f32),
        "ctx": nrm(ks[2], (BATCH, CTX_LEN, D), f32),
        "c_ctx": nrm(ks[3], (D,), f32),
        "ada_w": nrm(ks[4], (DEPTH, D, 6 * D), f32) * D ** -0.5,
        "ada_b": nrm(ks[5], (DEPTH, 6 * D), f32) * 0.01,
        "norm1_g": 1.0 + 0.02 * nrm(ks[6], (DEPTH, D), f32),
        "norm2_g": 1.0 + 0.02 * nrm(ks[7], (DEPTH, D), f32),
        "w_in": nrm(ks[8], (N_EVEN, D, IN_PROJ_DIM), f32) * D ** -0.5,
        "q_norm_g": 1.0 + 0.02 * nrm(ks[9], (N_EVEN, Q_LORA_RANK), f32),
        "kv_norm_g": 1.0 + 0.02 * nrm(ks[10], (N_EVEN, KV_LORA_RANK), f32),
        "w_uq": nrm(ks[11], (N_EVEN, Q_LORA_RANK, MLA_HEADS * QK_HEAD_DIM), f32) * Q_LORA_RANK ** -0.5,
        "w_ukv": nrm(ks[12], (N_EVEN, KV_LORA_RANK, MLA_HEADS * (QK_NOPE_DIM + V_HEAD_DIM)), f32) * KV_LORA_RANK ** -0.5,
        "q_gain": 1.0 + 0.02 * nrm(ks[13], (N_EVEN, QK_HEAD_DIM), f32),
        "k_gain": 1.0 + 0.02 * nrm(ks[14], (N_EVEN, QK_HEAD_DIM), f32),
        "conv_w": nrm(ks[15], (N_EVEN, CONV_WIDTH, CONV_DIM), f32) * CONV_WIDTH ** -0.5,
        "w_o": nrm(ks[16], (N_EVEN, MIX_OUT_DIM, D), f32) * MIX_OUT_DIM ** -0.5,
        "w_fourier": nrm(ks[17], (N_ODD, D, D), f32) * D ** -0.5,
        "ffn_w1": nrm(ks[18], (DEPTH, D, D_FF), f32) * D ** -0.5,
        "ffn_w3": nrm(ks[19], (DEPTH, D, D_FF), f32) * D ** -0.5,
        "ffn_w2": nrm(ks[20], (DEPTH, D_FF, D), f32) * D_FF ** -0.5,
    }


def reference(x, c, ctx, c_ctx, ada_w, ada_b, norm1_g, norm2_g, w_in, q_norm_g, kv_norm_g, w_uq, w_ukv,
              q_gain, k_gain, conv_w, w_o, w_fourier, ffn_w1, ffn_w3, ffn_w2):
    for i in range(DEPTH):
        last = i == DEPTH - 1
        j = i // 2
        sh1, sc1, g1, sh2, sc2, g2 = [m[:, None, :] for m in ada(c, ada_w[i], ada_b[i])]
        csh1, csc1, cg1, csh2, csc2, cg2 = ada(c_ctx, ada_w[i], ada_b[i])
        h_l = rmsnorm(x, norm1_g[i]) * (1 + sc1) + sh1
        if i % 2 == 0:
            h_c = rmsnorm(ctx, norm1_g[i]) * (1 + csc1) + csh1
            out_l, out_c = even_mixer(h_l, h_c, w_in[j], q_norm_g[j], kv_norm_g[j], w_uq[j], w_ukv[j],
                                      q_gain[j], k_gain[j], conv_w[j], w_o[j], not last)
        else:
            out_l = fourier_mixer(h_l, w_fourier[j])
            out_c = None
            if not last:
                h_c = rmsnorm(ctx, norm1_g[i]) * (1 + csc1) + csh1
                out_c = fourier_mixer(h_c, w_fourier[j])
        x = x + g1 * out_l
        x = x + g2 * swiglu(rmsnorm(x, norm2_g[i]) * (1 + sc2) + sh2, ffn_w1[i], ffn_w3[i], ffn_w2[i])
        if not last:
            ctx = ctx + cg1 * out_c
            ctx = ctx + cg2 * swiglu(rmsnorm(ctx, norm2_g[i]) * (1 + csc2) + csh2, ffn_w1[i], ffn_w3[i], ffn_w2[i])
    return x
```

```python
import functools
import math

import jax
import jax.numpy as jnp
from jax import lax
from jax.experimental import pallas as pl
from jax.experimental.pallas import tpu as pltpu

D_MODEL = 1024
GRID_W = 64
MLA_HEADS = 8
QK_NOPE_DIM = 64
QK_ROPE_DIM = 32
QK_HEAD_DIM = QK_NOPE_DIM + QK_ROPE_DIM
V_HEAD_DIM = 64
Q_LORA_RANK = 384
KV_LORA_RANK = 256
QK_SCALE = QK_HEAD_DIM ** -0.5
ROPE_BASE = 10000.0
CONV_DIM = 512
CONV_OFFSET = Q_LORA_RANK + KV_LORA_RANK + QK_ROPE_DIM
FOURIER_GROUPS = 4
FOURIER_GROUP_DIM = D_MODEL // FOURIER_GROUPS
EPS = 1e-6

LANES = 128
BF16_SUBLANES = 16
VMEM_LIMIT_BYTES = 56 * 1024 * 1024

HEAD_PAD = LANES
IN_PROJ_PAD = Q_LORA_RANK + KV_LORA_RANK + HEAD_PAD + 3 * CONV_DIM
PE_COL = Q_LORA_RANK + KV_LORA_RANK
CONV_COL = PE_COL + HEAD_PAD
DFT_T1 = 64
LOG2E = math.log2(math.e)

bf16 = jnp.bfloat16
f32 = jnp.float32


def _cparams(sem):
    return pltpu.CompilerParams(dimension_semantics=sem, vmem_limit_bytes=VMEM_LIMIT_BYTES)


def _const_spec(shape):
    nd = len(shape)
    return pl.BlockSpec(shape, lambda *_: (0,) * nd, pipeline_mode=pl.Buffered(1))


def _dot(a, b):
    return jnp.dot(a, b, preferred_element_type=f32)


def _rms_scale(x, width):
    return lax.rsqrt(jnp.sum(x * x, axis=-1, keepdims=True) * (1.0 / width) + EPS)


def _ada_body(c_ref, w_ref, b_ref, o_ref):
    c = c_ref[...]
    s = c / (1.0 + jnp.exp(-c))
    o_ref[0] = _dot(s.astype(bf16), w_ref[0].astype(bf16)) + b_ref[0]


def _ada(cvec, ada_w, ada_b):
    depth, d, n = ada_w.shape
    tn = 1536
    return pl.pallas_call(
        _ada_body,
        grid=(depth, n // tn),
        in_specs=[
            pl.BlockSpec((8, d), lambda l, j: (0, 0)),
            pl.BlockSpec((1, d, tn), lambda l, j: (l, 0, j)),
            pl.BlockSpec((1, 1, tn), lambda l, j: (l, 0, j)),
        ],
        out_specs=pl.BlockSpec((1, 8, tn), lambda l, j: (l, 0, j)),
        out_shape=jax.ShapeDtypeStruct((depth, 8, n), f32),
        compiler_params=_cparams(("arbitrary", "arbitrary")),
        name="ada",
    )(cvec, ada_w, ada_b.reshape(depth, 1, n))


def _modulated_norm(x, mod, g_ref, lo):
    d = D_MODEL
    shift = mod[:, lo:lo + d]
    scale = mod[:, lo + d:lo + 2 * d]
    gain = g_ref[...] * (1.0 + scale)
    return (x * _rms_scale(x, d)) * gain + shift


def _rope(t, cos, sin_a, sin_b):
    return t * cos + pltpu.roll(t, LANES - 8, axis=1) * sin_a + pltpu.roll(t, 8, axis=1) * sin_b


def _qkv_body(*refs, rope):
    (x_ref, mod_ref, n1g_ref, win_ref, qng_ref, kvng_ref, wuq_ref, wukv_ref,
     qgain_ref, kgain_ref) = refs[:10]
    if rope:
        cos_ref, sina_ref, sinb_ref = refs[10:13]
        q_out, kt_out, v_out, pc_out = refs[13:]
        cos, sin_a, sin_b = cos_ref[...], sina_ref[...], sinb_ref[...]
    else:
        q_out, kt_out, v_out, pc_out = refs[10:]
    x = x_ref[0]
    h = _modulated_norm(x, mod_ref[0, 0], n1g_ref, 0)
    p = _dot(h.astype(bf16), win_ref[...])
    pc_out[0] = p[:, CONV_COL:].astype(bf16)

    cq = p[:, :Q_LORA_RANK]
    cqn = cq * _rms_scale(cq, Q_LORA_RANK) * qng_ref[...]
    q = _dot(cqn.astype(bf16), wuq_ref[...])
    ckv = p[:, Q_LORA_RANK:PE_COL]
    ckvn = ckv * _rms_scale(ckv, KV_LORA_RANK) * kvng_ref[...]
    kv = _dot(ckvn.astype(bf16), wukv_ref[...])
    k_pe = p[:, PE_COL:CONV_COL]

    lane = lax.broadcasted_iota(jnp.int32, (1, HEAD_PAD), 1)
    ones_col = (lane == V_HEAD_DIM).astype(f32)
    qgain = qgain_ref[...]
    kgain = kgain_ref[...]
    for hd in range(MLA_HEADS):
        sl = slice(hd * HEAD_PAD, (hd + 1) * HEAD_PAD)
        qh = q[:, sl]
        qh = qh * _rms_scale(qh, QK_HEAD_DIM) * qgain
        kh = kv[:, sl] + k_pe
        kh = kh * _rms_scale(kh, QK_HEAD_DIM) * kgain
        if rope:
            qh = _rope(qh, cos, sin_a, sin_b)
            kh = _rope(kh, cos, sin_a, sin_b)
        q_out[0, hd] = qh.astype(bf16)
        kt_out[0, hd] = kh.T.astype(bf16)
        vh = kv[:, MLA_HEADS * HEAD_PAD + hd * HEAD_PAD:MLA_HEADS * HEAD_PAD + (hd + 1) * HEAD_PAD]
        v_out[0, hd] = (vh + ones_col).astype(bf16)


def _qkv(x, mods, layer, mod_row, n1g, w, rope_tabs, tm):
    b, t, d = x.shape
    hh, hp = MLA_HEADS, HEAD_PAD
    rope = rope_tabs is not None
    in_specs = [
        pl.BlockSpec((1, tm, d), lambda bi, i: (bi, i, 0)),
        pl.BlockSpec((1, 1, 1, 6 * d), lambda bi, i: (layer, mod_row(bi), 0, 0)),
        _const_spec((1, d)),
        _const_spec(w["w_in"].shape),
        _const_spec((1, Q_LORA_RANK)),
        _const_spec((1, KV_LORA_RANK)),
        _const_spec(w["w_uq"].shape),
        _const_spec(w["w_ukv"].shape),
        _const_spec((1, hp)),
        _const_spec((1, hp)),
    ]
    args = [x, mods, n1g, w["w_in"], w["q_norm_g"], w["kv_norm_g"], w["w_uq"], w["w_ukv"],
            w["q_gain"], w["k_gain"]]
    if rope:
        in_specs += [pl.BlockSpec((tm, hp), lambda bi, i: (i, 0))] * 3
        args += list(rope_tabs)
    return pl.pallas_call(
        functools.partial(_qkv_body, rope=rope),
        grid=(b, t // tm),
        in_specs=in_specs,
        out_specs=[
            pl.BlockSpec((1, hh, tm, hp), lambda bi, i: (bi, 0, i, 0)),
            pl.BlockSpec((1, hh, hp, tm), lambda bi, i: (bi, 0, 0, i)),
            pl.BlockSpec((1, hh, tm, hp), lambda bi, i: (bi, 0, i, 0)),
            pl.BlockSpec((1, tm, 3 * CONV_DIM), lambda bi, i: (bi, i, 0)),
        ],
        out_shape=[
            jax.ShapeDtypeStruct((b, hh, t, hp), bf16),
            jax.ShapeDtypeStruct((b, hh, hp, t), bf16),
            jax.ShapeDtypeStruct((b, hh, t, hp), bf16),
            jax.ShapeDtypeStruct((b, t, 3 * CONV_DIM), bf16),
        ],
        compiler_params=_cparams(("arbitrary", "arbitrary")),
        name="qkv_rope" if rope else "qkv_ctx",
    )(*args)


def _attn_body(q_ref, *refs, chunks):
    o_ref = refs[-1]
    q = q_ref[0, 0]
    m = None
    acc = None
    for src, (n_chunks, tk) in enumerate(chunks):
        kt_ref, v_ref = refs[2 * src], refs[2 * src + 1]
        for c in range(n_chunks):
            s = _dot(q, kt_ref[0, 0, :, c * tk:(c + 1) * tk])
            m_cur = jnp.max(s, axis=1, keepdims=True)
            m_new = m_cur if m is None else jnp.maximum(m, m_cur)
            p = jnp.exp2(s - m_new).astype(bf16)
            pv = _dot(p, v_ref[0, 0, c * tk:(c + 1) * tk, :])
            acc = pv if m is None else jnp.exp2(m - m_new) * acc + pv
            m = m_new
    o_ref[0, 0] = (acc / acc[:, V_HEAD_DIM:V_HEAD_DIM + 1]).astype(bf16)


def _attention(q, kv_sources, tq, tk):
    b, hh, t, hp = q.shape
    in_specs = [pl.BlockSpec((1, 1, tq, hp), lambda bi, hi, i: (bi, hi, i, 0))]
    args = [q]
    chunks = []
    for kt, v in kv_sources:
        n = kt.shape[-1]
        step = min(tk, n)
        chunks.append((n // step, step))
        in_specs += [pl.BlockSpec((1, 1, hp, n), lambda bi, hi, i: (bi, hi, 0, 0)),
                     pl.BlockSpec((1, 1, n, hp), lambda bi, hi, i: (bi, hi, 0, 0))]
        args += [kt, v]
    return pl.pallas_call(
        functools.partial(_attn_body, chunks=tuple(chunks)),
        grid=(b, hh, t // tq),
        in_specs=in_specs,
        out_specs=pl.BlockSpec((1, 1, tq, hp), lambda bi, hi, i: (bi, hi, i, 0)),
        out_shape=jax.ShapeDtypeStruct((b, hh, t, hp), bf16),
        compiler_params=_cparams(("arbitrary", "arbitrary", "arbitrary")),
        name="attn%d" % len(kv_sources),
    )(*args)


def _gated_ffn(x, out, mod, n2g_ref, w1_ref, w3_ref, w2_ref, ff_chunk):
    d = D_MODEL
    x1 = x + mod[:, 2 * d:3 * d] * out
    h2 = _modulated_norm(x1, mod, n2g_ref, 3 * d).astype(bf16)
    d_ff = w1_ref.shape[1]
    y = None
    for c in range(d_ff // ff_chunk):
        sl = slice(c * ff_chunk, (c + 1) * ff_chunk)
        u = _dot(h2, w1_ref[:, sl])
        g = _dot(h2, w3_ref[:, sl])
        act = (u / (1.0 + jnp.exp(-u)) * g).astype(bf16)
        part = _dot(act, w2_ref[sl, :])
        y = part if y is None else y + part
    return x1 + mod[:, 5 * d:6 * d] * y


def _ff_chunk(d_ff):
    for n in (2, 4, 1):
        if d_ff % (n * LANES) == 0:
            return d_ff // n
    return d_ff


def _mix_ffn_body(*refs, halo):
    x_ref, a_ref, pc_ref = refs[:3]
    if halo:
        pprev_ref, pnext_ref = refs[3:5]
        refs = refs[5:]
    else:
        refs = refs[3:]
    (mod_ref, convw_ref, woa_ref, woc_ref, n2g_ref, w1_ref, w3_ref, w2_ref, o_ref) = refs
    x = x_ref[0]
    mod = mod_ref[0, 0]
    tm = x.shape[0]

    out = None
    for hd in range(MLA_HEADS):
        part = _dot(a_ref[0, hd], woa_ref[hd])
        out = part if out is None else out + part

    cd = CONV_DIM
    pc = pc_ref[0].astype(f32)
    z = pc[:, cd:2 * cd] * pc[:, 2 * cd:]
    row = lax.broadcasted_iota(jnp.int32, (tm, cd), 0)
    if halo:
        i = pl.program_id(1)
        last = pl.num_programs(1) - 1
        pp = pprev_ref[0, BF16_SUBLANES - 1:BF16_SUBLANES, :].astype(f32)
        pn = pnext_ref[0, 0:1, :].astype(f32)
        z_prev = jnp.where(i > 0, pp[:, cd:2 * cd] * pp[:, 2 * cd:], 0.0)
        z_next = jnp.where(i < last, pn[:, cd:2 * cd] * pn[:, 2 * cd:], 0.0)
    else:
        z_prev = jnp.zeros((1, cd), f32)
        z_next = jnp.zeros((1, cd), f32)
    z_up = jnp.where(row == 0, z_prev, pltpu.roll(z, 1, axis=0))
    z_dn = jnp.where(row == tm - 1, z_next, pltpu.roll(z, tm - 1, axis=0))
    cw = convw_ref[...]
    y = z_up * cw[0:1, :] + z * cw[1:2, :] + z_dn * cw[2:3, :]
    out = out + _dot((pc[:, :cd] * y).astype(bf16), woc_ref[...])

    o_ref[0] = _gated_ffn(x, out, mod, n2g_ref, w1_ref, w3_ref, w2_ref, _ff_chunk(w1_ref.shape[1]))


def _mix_ffn(x, a, pc, mods, layer, mod_row, conv_w, w, n2g, w1, w3, w2, tm):
    b, t, d = x.shape
    hh, hp = MLA_HEADS, HEAD_PAD
    halo = t > tm
    in_specs = [
        pl.BlockSpec((1, tm, d), lambda bi, i: (bi, i, 0)),
        pl.BlockSpec((1, hh, tm, hp), lambda bi, i: (bi, 0, i, 0)),
        pl.BlockSpec((1, tm, 3 * CONV_DIM), lambda bi, i: (bi, i, 0)),
    ]
    args = [x, a, pc]
    if halo:
        per = tm // BF16_SUBLANES
        nblk = t // BF16_SUBLANES
        in_specs += [
            pl.BlockSpec((1, BF16_SUBLANES, 3 * CONV_DIM),
                         lambda bi, i: (bi, jnp.maximum(i * per - 1, 0), 0)),
            pl.BlockSpec((1, BF16_SUBLANES, 3 * CONV_DIM),
                         lambda bi, i: (bi, jnp.minimum((i + 1) * per, nblk - 1), 0)),
        ]
        args += [pc, pc]
    in_specs += [
        pl.BlockSpec((1, 1, 1, 6 * d), lambda bi, i: (layer, mod_row(bi), 0, 0)),
        _const_spec(conv_w.shape),
        _const_spec(w["w_o_attn"].shape),
        _const_spec(w["w_o_conv"].shape),
        _const_spec((1, d)),
        _const_spec(w1.shape),
        _const_spec(w3.shape),
        _const_spec(w2.shape),
    ]
    args += [mods, conv_w, w["w_o_attn"], w["w_o_conv"], n2g, w1, w3, w2]
    return pl.pallas_call(
        functools.partial(_mix_ffn_body, halo=halo),
        grid=(b, t // tm),
        in_specs=in_specs,
        out_specs=pl.BlockSpec((1, tm, d), lambda bi, i: (bi, i, 0)),
        out_shape=jax.ShapeDtypeStruct((b, t, d), f32),
        compiler_params=_cparams(("arbitrary", "arbitrary")),
        name="mix_ffn_halo" if halo else "mix_ffn",
    )(*args)


def _dft_a_body(x_ref, mod_ref, n1g_ref, cs_ref, twc_ref, tws_ref, o_ref, *, k):
    d = D_MODEL
    mod = mod_ref[0, 0]
    cs = cs_ref[...]
    reps = d // LANES
    for j in range(k):
        x = x_ref[0, :, j * d:(j + 1) * d]
        h = _modulated_norm(x, mod, n1g_ref, 0)
        a = _dot(cs, h.astype(bf16))
        a_re, a_im = a[:DFT_T1], a[DFT_T1:]
        c = jnp.tile(twc_ref[:, j * LANES:(j + 1) * LANES], (1, reps))
        s = jnp.tile(tws_ref[:, j * LANES:(j + 1) * LANES], (1, reps))
        o_ref[0, 0, :, j * d:(j + 1) * d] = (a_re * c + a_im * s).astype(bf16)
        o_ref[0, 1, :, j * d:(j + 1) * d] = (a_im * c - a_re * s).astype(bf16)


def _dft_a(x, mods, layer, n1g, tabs, k):
    b, t, d = x.shape
    t2 = t // DFT_T1
    xv = x.reshape(b, DFT_T1, t2 * d)
    return pl.pallas_call(
        functools.partial(_dft_a_body, k=k),
        grid=(b, t2 // k),
        in_specs=[
            pl.BlockSpec((1, DFT_T1, k * d), lambda bi, j: (bi, 0, j)),
            pl.BlockSpec((1, 1, 1, 6 * d), lambda bi, j: (layer, bi, 0, 0)),
            _const_spec((1, d)),
            _const_spec((2 * DFT_T1, DFT_T1)),
            pl.BlockSpec((DFT_T1, k * LANES), lambda bi, j: (0, j)),
            pl.BlockSpec((DFT_T1, k * LANES), lambda bi, j: (0, j)),
        ],
        out_specs=pl.BlockSpec((1, 2, DFT_T1, k * d), lambda bi, j: (bi, 0, 0, j)),
        out_shape=jax.ShapeDtypeStruct((b, 2, DFT_T1, t2 * d), bf16),
        compiler_params=_cparams(("arbitrary", "arbitrary")),
        name="dft_a",
    )(xv, mods, n1g, tabs["cs64"], tabs["tw_cos"], tabs["tw_sin"])


def _dft_ffn_body(x_ref, a_ref, mod_ref, m_ref, cc_ref, sc_ref, wf_ref, n2g_ref,
                  w1_ref, w3_ref, w2_ref, o_ref, *, n1):
    d = D_MODEL
    gd = FOURIER_GROUP_DIM
    t2 = a_ref.shape[3]
    mod = mod_ref[0, 0]
    mm = m_ref[...]
    fs = []
    for i in range(n1):
        bc = _dot(mm[:, :t2], a_ref[0, 0, i]) + _dot(mm[:, t2:], a_ref[0, 1, i])
        b_re = bc[:t2].astype(bf16)
        b_im = bc[t2:].astype(bf16)
        cols = []
        for g in range(FOURIER_GROUPS):
            sl = slice(g * gd, (g + 1) * gd)
            cols.append(_dot(b_re[:, sl], cc_ref[...]) + _dot(b_im[:, sl], sc_ref[...]))
        fs.append(jnp.concatenate(cols, axis=1))
    f = jnp.concatenate(fs, axis=0).astype(bf16)
    out = _dot(f, wf_ref[...])
    x = jnp.concatenate([x_ref[0, :, i * d:(i + 1) * d] for i in range(n1)], axis=0)
    res = _gated_ffn(x, out, mod, n2g_ref, w1_ref, w3_ref, w2_ref, _ff_chunk(w1_ref.shape[1]))
    for i in range(n1):
        o_ref[0, :, i * d:(i + 1) * d] = res[i * t2:(i + 1) * t2]


def _dft_ffn(x, a, mods, layer, tabs, wf, n2g, w1, w3, w2, n1):
    b, t, d = x.shape
    t2 = t // DFT_T1
    xv = x.reshape(b, t2, DFT_T1 * d)
    av = a.reshape(b, 2, DFT_T1, t2, d)
    out = pl.pallas_call(
        functools.partial(_dft_ffn_body, n1=n1),
        grid=(b, DFT_T1 // n1),
        in_specs=[
            pl.BlockSpec((1, t2, n1 * d), lambda bi, j: (bi, 0, j)),
            pl.BlockSpec((1, 2, n1, t2, d), lambda bi, j: (bi, 0, j, 0, 0)),
            pl.BlockSpec((1, 1, 1, 6 * d), lambda bi, j: (layer, bi, 0, 0)),
            _const_spec((2 * t2, 2 * t2)),
            _const_spec((FOURIER_GROUP_DIM, FOURIER_GROUP_DIM)),
            _const_spec((FOURIER_GROUP_DIM, FOURIER_GROUP_DIM)),
            _const_spec(wf.shape),
            _const_spec((1, d)),
            _const_spec(w1.shape),
            _const_spec(w3.shape),
            _const_spec(w2.shape),
        ],
        out_specs=pl.BlockSpec((1, t2, n1 * d), lambda bi, j: (bi, 0, j)),
        out_shape=jax.ShapeDtypeStruct((b, t2, DFT_T1 * d), f32),
        compiler_params=_cparams(("arbitrary", "arbitrary")),
        name="dft_ffn",
    )(xv, av, mods, tabs["m2"], tabs["cc"], tabs["sc"], wf, n2g, w1, w3, w2)
    return out.reshape(b, t, d)


def _angle(i, j, n):
    return (2.0 * math.pi / n) * ((i * j) % n).astype(f32)


def _dft_tables(t):
    t1, t2 = DFT_T1, t // DFT_T1
    i1 = jnp.arange(t1, dtype=jnp.int32)
    ang64 = _angle(i1[:, None], i1[None, :], t1)
    s1 = t1 ** -0.5
    cs64 = jnp.concatenate([jnp.cos(ang64) * s1, -jnp.sin(ang64) * s1], axis=0).astype(bf16)
    i2 = jnp.arange(t2, dtype=jnp.int32)
    angt = _angle(i1[:, None], i2[None, :], t)
    tw_cos = jnp.repeat(jnp.cos(angt), LANES, axis=1)
    tw_sin = jnp.repeat(jnp.sin(angt), LANES, axis=1)
    ang2 = _angle(i2[:, None], i2[None, :], t2)
    s2 = t2 ** -0.5
    c2, sn2 = jnp.cos(ang2) * s2, jnp.sin(ang2) * s2
    m2 = jnp.concatenate([jnp.concatenate([c2, sn2], axis=1),
                          jnp.concatenate([-sn2, c2], axis=1)], axis=0).astype(bf16)
    ic = jnp.arange(FOURIER_GROUP_DIM, dtype=jnp.int32)
    angc = _angle(ic[:, None], ic[None, :], FOURIER_GROUP_DIM)
    sc_ = FOURIER_GROUP_DIM ** -0.5
    return dict(cs64=cs64, tw_cos=tw_cos, tw_sin=tw_sin, m2=m2,
                cc=(jnp.cos(angc) * sc_).astype(bf16), sc=(jnp.sin(angc) * sc_).astype(bf16))


def _rope_tables(t):
    rows = t // GRID_W
    row = jnp.broadcast_to(jnp.arange(rows)[:, None], (rows, GRID_W)).reshape(-1).astype(f32)
    col = jnp.broadcast_to(jnp.arange(GRID_W)[None, :], (rows, GRID_W)).reshape(-1).astype(f32)
    half = QK_ROPE_DIM // 2
    inv = 1.0 / (ROPE_BASE ** (jnp.arange(0, half, 2, dtype=f32) / half))
    ar, ac = row[:, None] * inv, col[:, None] * inv
    zeros = jnp.zeros((t, 8), f32)
    cos = jnp.concatenate([jnp.ones((t, QK_NOPE_DIM), f32), jnp.cos(ar), jnp.cos(ar), jnp.cos(ac),
                           jnp.cos(ac), jnp.ones((t, HEAD_PAD - QK_HEAD_DIM), f32)], axis=1)
    z64 = jnp.zeros((t, QK_NOPE_DIM), f32)
    z32 = jnp.zeros((t, HEAD_PAD - QK_HEAD_DIM), f32)
    sin_a = jnp.concatenate([z64, -jnp.sin(ar), zeros, -jnp.sin(ac), zeros, z32], axis=1)
    sin_b = jnp.concatenate([z64, zeros, jnp.sin(ar), zeros, jnp.sin(ac), z32], axis=1)
    return cos, sin_a, sin_b


def _prep_even(j, w_in, q_norm_g, kv_norm_g, w_uq, w_ukv, q_gain, k_gain, w_o):
    hh, hp = MLA_HEADS, HEAD_PAD
    d = D_MODEL
    wi = w_in[j]
    pe = jnp.zeros((d, hp), f32).at[:, QK_NOPE_DIM:QK_HEAD_DIM].set(wi[:, PE_COL:CONV_OFFSET])
    w_in_p = jnp.concatenate([wi[:, :PE_COL], pe, wi[:, CONV_OFFSET:]], axis=1).astype(bf16)
    wq = jnp.pad(w_uq[j].reshape(Q_LORA_RANK, hh, QK_HEAD_DIM), ((0, 0), (0, 0), (0, hp - QK_HEAD_DIM)))
    wkv = w_ukv[j].reshape(KV_LORA_RANK, hh, QK_NOPE_DIM + V_HEAD_DIM)
    wk = jnp.pad(wkv[:, :, :QK_NOPE_DIM], ((0, 0), (0, 0), (0, hp - QK_NOPE_DIM)))
    wv = jnp.pad(wkv[:, :, QK_NOPE_DIM:], ((0, 0), (0, 0), (0, hp - V_HEAD_DIM)))
    w_ukv_p = jnp.concatenate([wk.reshape(KV_LORA_RANK, hh * hp), wv.reshape(KV_LORA_RANK, hh * hp)], axis=1)
    woa = jnp.pad(w_o[j][:hh * V_HEAD_DIM].reshape(hh, V_HEAD_DIM, d), ((0, 0), (0, hp - V_HEAD_DIM), (0, 0)))
    pad_gain = lambda g: jnp.pad(g, (0, hp - QK_HEAD_DIM)).reshape(1, hp)
    return dict(
        w_in=w_in_p, w_uq=wq.reshape(Q_LORA_RANK, hh * hp).astype(bf16), w_ukv=w_ukv_p.astype(bf16),
        q_norm_g=q_norm_g[j].reshape(1, -1), kv_norm_g=kv_norm_g[j].reshape(1, -1),
        q_gain=pad_gain(q_gain[j] * (QK_SCALE * LOG2E)), k_gain=pad_gain(k_gain[j]),
        w_o_attn=woa.astype(bf16), w_o_conv=w_o[j][hh * V_HEAD_DIM:].astype(bf16))


def _pick(n, pref):
    return pref if n % pref == 0 else n


def kernel(x, c, ctx, c_ctx, ada_w, ada_b, norm1_g, norm2_g, w_in, q_norm_g, kv_norm_g, w_uq, w_ukv,
           q_gain, k_gain, conv_w, w_o, w_fourier, ffn_w1, ffn_w3, ffn_w2):
    b, s, d = x.shape
    depth = ada_w.shape[0]
    cvec = jnp.zeros((8, d), f32).at[:b].set(c).at[b].set(c_ctx)
    mods = _ada(cvec, ada_w, ada_b).reshape(depth, 8, 1, 6 * d)
    lat_row = lambda bi: bi
    ctx_row = lambda bi: b
    w1 = ffn_w1.astype(bf16)
    w3 = ffn_w3.astype(bf16)
    w2 = ffn_w2.astype(bf16)
    for i in range(depth):
        last = i == depth - 1
        j = i // 2
        n1g = norm1_g[i].reshape(1, d)
        n2g = norm2_g[i].reshape(1, d)
        if i % 2 == 0:
            w = _prep_even(j, w_in, q_norm_g, kv_norm_g, w_uq, w_ukv, q_gain, k_gain, w_o)
            tm = _pick(s, 512)
            lc = ctx.shape[1]
            q_l, kt_l, v_l, pc_l = _qkv(x, mods, i, lat_row, n1g, w, _rope_tables(s), tm)
            q_c, kt_c, v_c, pc_c = _qkv(ctx, mods, i, ctx_row, n1g, w, None, lc)
            a_l = _attention(q_l, [(kt_l, v_l), (kt_c, v_c)], _pick(s, 512), _pick(s, 1024))
            x_new = _mix_ffn(x, a_l, pc_l, mods, i, lat_row, conv_w[j], w, n2g, w1[i], w3[i], w2[i], tm)
            if not last:
                a_c = _attention(q_c, [(kt_c, v_c)], lc, lc)
                ctx = _mix_ffn(ctx, a_c, pc_c, mods, i, ctx_row, conv_w[j], w, n2g, w1[i], w3[i], w2[i], lc)
            x = x_new
        else:
            tabs = _dft_tables(s)
            t2 = s // DFT_T1
            a = _dft_a(x, mods, i, n1g, tabs, _pick(t2, 16))
            x = _dft_ffn(x, a, mods, i, tabs, w_fourier[j].astype(bf16), n2g, w1[i], w3[i], w2[i], 4)
            assert last, "odd non-final layers are not implemented"
    return x
```

```python
import functools
import math

import jax
import jax.numpy as jnp
from jax import lax
from jax.experimental import pallas as pl
from jax.experimental.pallas import tpu as pltpu

D_MODEL = 1024
GRID_W = 64
MLA_HEADS = 8
QK_NOPE_DIM = 64
QK_ROPE_DIM = 32
QK_HEAD_DIM = QK_NOPE_DIM + QK_ROPE_DIM
V_HEAD_DIM = 64
Q_LORA_RANK = 384
KV_LORA_RANK = 256
QK_SCALE = QK_HEAD_DIM ** -0.5
ROPE_BASE = 10000.0
CONV_DIM = 512
CONV_OFFSET = Q_LORA_RANK + KV_LORA_RANK + QK_ROPE_DIM
FOURIER_GROUPS = 4
FOURIER_GROUP_DIM = D_MODEL // FOURIER_GROUPS
EPS = 1e-6

LANES = 128
BF16_SUBLANES = 16
VMEM_LIMIT_BYTES = 56 * 1024 * 1024

HEAD_PAD = LANES
ALL_HEADS = MLA_HEADS * HEAD_PAD
PE_COL = Q_LORA_RANK + KV_LORA_RANK
PE_SW_COL = PE_COL + HEAD_PAD
CONV_COL = PE_SW_COL + HEAD_PAD
IN_PROJ_PAD = CONV_COL + 3 * CONV_DIM
QKV_ROW_BLOCK = 128
DFT_T1 = 64
LOG2E = math.log2(math.e)
NT_DIMS = (((1,), (1,)), ((), ()))

bf16 = jnp.bfloat16
f32 = jnp.float32


def _cparams(sem):
    return pltpu.CompilerParams(dimension_semantics=sem, vmem_limit_bytes=VMEM_LIMIT_BYTES)


def _const_spec(shape, lead=None):
    if lead is None:
        nd = len(shape)
        return pl.BlockSpec(shape, lambda *_: (0,) * nd, pipeline_mode=pl.Buffered(1))
    nd = len(shape) - 1
    return pl.BlockSpec((1,) + tuple(shape[1:]), lambda *_: (lead,) + (0,) * nd,
                        pipeline_mode=pl.Buffered(1))


def _mod_spec(layer, mod_row):
    return pl.BlockSpec((1, 1, 1, 6 * D_MODEL), lambda bi, i: (layer, mod_row(bi), 0, 0))


def _dot(a, b):
    return jnp.dot(a, b, preferred_element_type=f32)


def _rms_scale(x, width):
    return lax.rsqrt(jnp.sum(x * x, axis=-1, keepdims=True) * (1.0 / width) + EPS)


def _ada_body(c_ref, w_ref, b_ref, o_ref):
    c = c_ref[...]
    s = c / (1.0 + jnp.exp(-c))
    o_ref[0] = _dot(s.astype(bf16), w_ref[0].astype(bf16)) + b_ref[0]


def _ada(cvec, ada_w, ada_b):
    depth, d, n = ada_w.shape
    tn = 1536
    return pl.pallas_call(
        _ada_body,
        grid=(depth, n // tn),
        in_specs=[
            pl.BlockSpec((8, d), lambda l, j: (0, 0)),
            pl.BlockSpec((1, d, tn), lambda l, j: (l, 0, j)),
            pl.BlockSpec((1, 1, tn), lambda l, j: (l, 0, j)),
        ],
        out_specs=pl.BlockSpec((1, 8, tn), lambda l, j: (l, 0, j)),
        out_shape=jax.ShapeDtypeStruct((depth, 8, n), f32),
        compiler_params=_cparams(("arbitrary", "arbitrary")),
        name="ada",
    )(cvec, ada_w, ada_b.reshape(depth, 1, n))


def _modulated_norm(x, mod, g_ref, lo):
    d = D_MODEL
    shift = mod[:, lo:lo + d]
    scale = mod[:, lo + d:lo + 2 * d]
    gain = g_ref[...] * (1.0 + scale)
    return (x * _rms_scale(x, d)) * gain + shift


def _qkv_body(*refs, rope):
    (x_ref, mod_ref, n1g_ref, win_ref, qng_ref, kvng_ref, wuq_ref, wukv_ref, gains_ref) = refs[:9]
    if rope:
        rrow_ref, rcol_ref = refs[9:11]
        refs = refs[11:]
    else:
        refs = refs[9:]
    q_out, k_out, v_out, pc_out, q_scr, kv_scr, pe_scr = refs
    x = x_ref[0]
    tm = x.shape[0]
    h = _modulated_norm(x, mod_ref[0, 0], n1g_ref, 0)
    p = _dot(h.astype(bf16), win_ref[...])
    pc_out[0] = p[:, CONV_COL:].astype(bf16)
    pe_scr[...] = p[:, PE_COL:CONV_COL]
    cq = p[:, :Q_LORA_RANK]
    cqn = cq * _rms_scale(cq, Q_LORA_RANK) * qng_ref[...]
    q_scr[...] = _dot(cqn.astype(bf16), wuq_ref[...])
    ckv = p[:, Q_LORA_RANK:PE_COL]
    ckvn = ckv * _rms_scale(ckv, KV_LORA_RANK) * kvng_ref[...]
    kv_scr[...] = _dot(ckvn.astype(bf16), wukv_ref[...])

    gains = gains_ref[...]
    lane = lax.broadcasted_iota(jnp.int32, (1, HEAD_PAD), 1)
    ones_col = (lane == V_HEAD_DIM).astype(f32)
    rb = min(QKV_ROW_BLOCK, tm)
    per = rb // GRID_W
    grid_row0 = pl.program_id(1) * (tm // GRID_W)

    def block(ib, carry):
        r0 = pl.multiple_of(ib * rb, rb)
        rows = pl.ds(r0, rb)
        if rope:
            tabs = []
            for comp in range(2):
                parts = []
                for g in range(per):
                    rr = rrow_ref[comp, pl.ds(grid_row0 + ib * per + g, 1), :]
                    parts.append(rcol_ref[comp] + rr)
                tabs.append(jnp.concatenate(parts, axis=0))
            cos, sin = tabs
            qa, qb = cos * gains[0:1], sin * gains[1:2]
            ka, kb = cos * gains[2:3], sin * gains[3:4]
        pe = pe_scr[rows, :HEAD_PAD]
        pe_sw = pe_scr[rows, HEAD_PAD:]
        for hd in range(MLA_HEADS):
            sl = slice(hd * HEAD_PAD, (hd + 1) * HEAD_PAD)
            sl2 = slice(ALL_HEADS + hd * HEAD_PAD, ALL_HEADS + (hd + 1) * HEAD_PAD)
            qh = q_scr[rows, sl]
            kh = kv_scr[rows, sl] + pe
            rq = _rms_scale(qh, QK_HEAD_DIM)
            rk = _rms_scale(kh, QK_HEAD_DIM)
            if rope:
                qo = (qh * qa + q_scr[rows, sl2] * qb) * rq
                ko = (kh * ka + pe_sw * kb) * rk
            else:
                qo = qh * gains[0:1] * rq
                ko = kh * gains[2:3] * rk
            q_out[0, hd, rows, :] = qo.astype(bf16)
            k_out[0, hd, rows, :] = ko.astype(bf16)
            v_out[0, hd, rows, :] = (kv_scr[rows, sl2] + ones_col).astype(bf16)
        return carry

    lax.fori_loop(0, tm // rb, block, 0)


def _qkv(x, mods, layer, mod_row, n1g, w, rope_tabs, tm):
    b, t, d = x.shape
    hh, hp = MLA_HEADS, HEAD_PAD
    rope = rope_tabs is not None
    in_specs = [
        pl.BlockSpec((1, tm, d), lambda bi, i: (bi, i, 0)),
        _mod_spec(layer, mod_row),
        _const_spec((1, d)),
        _const_spec(w["w_in"].shape),
        _const_spec((1, Q_LORA_RANK)),
        _const_spec((1, KV_LORA_RANK)),
        _const_spec(w["w_uq"].shape),
        _const_spec(w["w_ukv"].shape),
        _const_spec((8, hp)),
    ]
    args = [x, mods, n1g, w["w_in"], w["q_norm_g"], w["kv_norm_g"], w["w_uq"], w["w_ukv"], w["gains"]]
    if rope:
        rrow, rcol = rope_tabs
        in_specs += [_const_spec(rrow.shape), _const_spec(rcol.shape)]
        args += [rrow, rcol]
    head_spec = pl.BlockSpec((1, hh, tm, hp), lambda bi, i: (bi, 0, i, 0))
    head_shape = jax.ShapeDtypeStruct((b, hh, t, hp), bf16)
    return pl.pallas_call(
        functools.partial(_qkv_body, rope=rope),
        grid=(b, t // tm),
        in_specs=in_specs,
        out_specs=[head_spec, head_spec, head_spec,
                   pl.BlockSpec((1, tm, 3 * CONV_DIM), lambda bi, i: (bi, i, 0))],
        out_shape=[head_shape, head_shape, head_shape,
                   jax.ShapeDtypeStruct((b, t, 3 * CONV_DIM), bf16)],
        scratch_shapes=[pltpu.VMEM((tm, 2 * ALL_HEADS), f32), pltpu.VMEM((tm, 2 * ALL_HEADS), f32),
                        pltpu.VMEM((tm, 2 * hp), f32)],
        compiler_params=_cparams(("arbitrary", "arbitrary")),
        name="qkv_rope" if rope else "qkv_ctx",
    )(*args)


def _attn_body(q_ref, *refs, chunks):
    o_ref = refs[-1]
    q = q_ref[0, 0]
    m = None
    acc = None
    for src, (n_chunks, tk) in enumerate(chunks):
        k_ref, v_ref = refs[2 * src], refs[2 * src + 1]
        for c in range(n_chunks):
            ks = slice(c * tk, (c + 1) * tk)
            s = lax.dot_general(q, k_ref[0, 0, ks, :], NT_DIMS, preferred_element_type=f32)
            m_cur = jnp.max(s, axis=1, keepdims=True)
            m_new = m_cur if m is None else jnp.maximum(m, m_cur)
            p = jnp.exp2(s - m_new).astype(bf16)
            pv = _dot(p, v_ref[0, 0, ks, :])
            acc = pv if m is None else jnp.exp2(m - m_new) * acc + pv
            m = m_new
    o_ref[0, 0] = (acc / acc[:, V_HEAD_DIM:V_HEAD_DIM + 1]).astype(bf16)


def _attention(q, kv_sources, tq, tk):
    b, hh, t, hp = q.shape
    in_specs = [pl.BlockSpec((1, 1, tq, hp), lambda bi, hi, i: (bi, hi, i, 0))]
    args = [q]
    chunks = []
    for k, v in kv_sources:
        n = k.shape[2]
        step = min(tk, n)
        chunks.append((n // step, step))
        in_specs += [pl.BlockSpec((1, 1, n, hp), lambda bi, hi, i: (bi, hi, 0, 0))] * 2
        args += [k, v]
    return pl.pallas_call(
        functools.partial(_attn_body, chunks=tuple(chunks)),
        grid=(b, hh, t // tq),
        in_specs=in_specs,
        out_specs=pl.BlockSpec((1, 1, tq, hp), lambda bi, hi, i: (bi, hi, i, 0)),
        out_shape=jax.ShapeDtypeStruct((b, hh, t, hp), bf16),
        compiler_params=_cparams(("arbitrary", "arbitrary", "arbitrary")),
        name="attn%d" % len(kv_sources),
    )(*args)


def _gated_ffn(x, out, mod, n2g_ref, w1_ref, w3_ref, w2_ref):
    d = D_MODEL
    x1 = x + mod[:, 2 * d:3 * d] * out
    h2 = _modulated_norm(x1, mod, n2g_ref, 3 * d).astype(bf16)
    d_ff = w1_ref.shape[2]
    ff_chunk = _ff_chunk(d_ff)
    y = None
    for c in range(d_ff // ff_chunk):
        sl = slice(c * ff_chunk, (c + 1) * ff_chunk)
        u = _dot(h2, w1_ref[0, :, sl])
        g = _dot(h2, w3_ref[0, :, sl])
        act = (u / (1.0 + jnp.exp(-u)) * g).astype(bf16)
        part = _dot(act, w2_ref[0, sl, :])
        y = part if y is None else y + part
    return x1 + mod[:, 5 * d:6 * d] * y


def _ff_chunk(d_ff):
    for n in (2, 4, 1):
        if d_ff % (n * LANES) == 0:
            return d_ff // n
    return d_ff


def _mix_ffn_body(*refs, halo):
    x_ref, a_ref, pc_ref = refs[:3]
    if halo:
        pprev_ref, pnext_ref = refs[3:5]
        refs = refs[5:]
    else:
        refs = refs[3:]
    (mod_ref, convw_ref, woa_ref, woc_ref, n2g_ref, w1_ref, w3_ref, w2_ref, o_ref) = refs
    x = x_ref[0]
    mod = mod_ref[0, 0]
    tm = x.shape[0]

    out = None
    for hd in range(MLA_HEADS):
        part = _dot(a_ref[0, hd], woa_ref[hd])
        out = part if out is None else out + part

    cd = CONV_DIM
    pc = pc_ref[0].astype(f32)
    z = pc[:, cd:2 * cd] * pc[:, 2 * cd:]
    row = lax.broadcasted_iota(jnp.int32, (tm, cd), 0)
    if halo:
        i = pl.program_id(1)
        last = pl.num_programs(1) - 1
        pp = pprev_ref[0, BF16_SUBLANES - 1:BF16_SUBLANES, :].astype(f32)
        pn = pnext_ref[0, 0:1, :].astype(f32)
        z_prev = jnp.where(i > 0, pp[:, cd:2 * cd] * pp[:, 2 * cd:], 0.0)
        z_next = jnp.where(i < last, pn[:, cd:2 * cd] * pn[:, 2 * cd:], 0.0)
    else:
        z_prev = jnp.zeros((1, cd), f32)
        z_next = jnp.zeros((1, cd), f32)
    z_up = jnp.where(row == 0, z_prev, pltpu.roll(z, 1, axis=0))
    z_dn = jnp.where(row == tm - 1, z_next, pltpu.roll(z, tm - 1, axis=0))
    cw = convw_ref[0]
    y = z_up * cw[0:1, :] + z * cw[1:2, :] + z_dn * cw[2:3, :]
    out = out + _dot((pc[:, :cd] * y).astype(bf16), woc_ref[...])

    o_ref[0] = _gated_ffn(x, out, mod, n2g_ref, w1_ref, w3_ref, w2_ref)


def _mix_ffn(x, a, pc, mods, layer, mod_row, conv_w, j, w, n2g, w1, w3, w2, tm):
    b, t, d = x.shape
    hh, hp = MLA_HEADS, HEAD_PAD
    halo = t > tm
    in_specs = [
        pl.BlockSpec((1, tm, d), lambda bi, i: (bi, i, 0)),
        pl.BlockSpec((1, hh, tm, hp), lambda bi, i: (bi, 0, i, 0)),
        pl.BlockSpec((1, tm, 3 * CONV_DIM), lambda bi, i: (bi, i, 0)),
    ]
    args = [x, a, pc]
    if halo:
        per = tm // BF16_SUBLANES
        nblk = t // BF16_SUBLANES
        in_specs += [
            pl.BlockSpec((1, BF16_SUBLANES, 3 * CONV_DIM),
                         lambda bi, i: (bi, jnp.maximum(i * per - 1, 0), 0)),
            pl.BlockSpec((1, BF16_SUBLANES, 3 * CONV_DIM),
                         lambda bi, i: (bi, jnp.minimum((i + 1) * per, nblk - 1), 0)),
        ]
        args += [pc, pc]
    in_specs += [
        _mod_spec(layer, mod_row),
        _const_spec(conv_w.shape, j),
        _const_spec(w["w_o_attn"].shape),
        _const_spec(w["w_o_conv"].shape),
        _const_spec((1, d)),
        _const_spec(w1.shape, layer),
        _const_spec(w3.shape, layer),
        _const_spec(w2.shape, layer),
    ]
    args += [mods, conv_w, w["w_o_attn"], w["w_o_conv"], n2g, w1, w3, w2]
    return pl.pallas_call(
        functools.partial(_mix_ffn_body, halo=halo),
        grid=(b, t // tm),
        in_specs=in_specs,
        out_specs=pl.BlockSpec((1, tm, d), lambda bi, i: (bi, i, 0)),
        out_shape=jax.ShapeDtypeStruct((b, t, d), f32),
        compiler_params=_cparams(("arbitrary", "arbitrary")),
        name="mix_ffn_halo" if halo else "mix_ffn",
    )(*args)


def _dft_a_body(x_ref, mod_ref, n1g_ref, cs_ref, twc_ref, tws_ref, o_ref, *, k):
    d = D_MODEL
    mod = mod_ref[0, 0]
    cs = cs_ref[...]
    reps = d // LANES
    for j in range(k):
        x = x_ref[0, :, j * d:(j + 1) * d]
        h = _modulated_norm(x, mod, n1g_ref, 0)
        a = _dot(cs, h.astype(bf16))
        a_re, a_im = a[:DFT_T1], a[DFT_T1:]
        c = jnp.tile(twc_ref[:, j * LANES:(j + 1) * LANES], (1, reps))
        s = jnp.tile(tws_ref[:, j * LANES:(j + 1) * LANES], (1, reps))
        o_ref[0, 0, :, j * d:(j + 1) * d] = (a_re * c + a_im * s).astype(bf16)
        o_ref[0, 1, :, j * d:(j + 1) * d] = (a_im * c - a_re * s).astype(bf16)


def _dft_a(x, mods, layer, n1g, tabs, k):
    b, t, d = x.shape
    t2 = t // DFT_T1
    xv = x.reshape(b, DFT_T1, t2 * d)
    return pl.pallas_call(
        functools.partial(_dft_a_body, k=k),
        grid=(b, t2 // k),
        in_specs=[
            pl.BlockSpec((1, DFT_T1, k * d), lambda bi, j: (bi, 0, j)),
            _mod_spec(layer, lambda bi: bi),
            _const_spec((1, d)),
            _const_spec((2 * DFT_T1, DFT_T1)),
            pl.BlockSpec((DFT_T1, k * LANES), lambda bi, j: (0, j)),
            pl.BlockSpec((DFT_T1, k * LANES), lambda bi, j: (0, j)),
        ],
        out_specs=pl.BlockSpec((1, 2, DFT_T1, k * d), lambda bi, j: (bi, 0, 0, j)),
        out_shape=jax.ShapeDtypeStruct((b, 2, DFT_T1, t2 * d), bf16),
        compiler_params=_cparams(("arbitrary", "arbitrary")),
        name="dft_a",
    )(xv, mods, n1g, tabs["cs64"], tabs["tw_cos"], tabs["tw_sin"])


def _dft_ffn_body(x_ref, a_ref, mod_ref, m_ref, cc_ref, sc_ref, wf_ref, n2g_ref,
                  w1_ref, w3_ref, w2_ref, o_ref, *, n1):
    d = D_MODEL
    gd = FOURIER_GROUP_DIM
    t2 = a_ref.shape[3]
    mod = mod_ref[0, 0]
    mm = m_ref[...]
    fs = []
    for i in range(n1):
        bc = _dot(mm[:, :t2], a_ref[0, 0, i]) + _dot(mm[:, t2:], a_ref[0, 1, i])
        b_re = bc[:t2].astype(bf16)
        b_im = bc[t2:].astype(bf16)
        cols = []
        for g in range(FOURIER_GROUPS):
            sl = slice(g * gd, (g + 1) * gd)
            cols.append(_dot(b_re[:, sl], cc_ref[...]) + _dot(b_im[:, sl], sc_ref[...]))
        fs.append(jnp.concatenate(cols, axis=1))
    f = jnp.concatenate(fs, axis=0).astype(bf16)
    out = _dot(f, wf_ref[0])
    x = jnp.concatenate([x_ref[0, :, i * d:(i + 1) * d] for i in range(n1)], axis=0)
    res = _gated_ffn(x, out, mod, n2g_ref, w1_ref, w3_ref, w2_ref)
    for i in range(n1):
        o_ref[0, :, i * d:(i + 1) * d] = res[i * t2:(i + 1) * t2]


def _dft_ffn(x, a, mods, layer, tabs, wf, j, n2g, w1, w3, w2, n1):
    b, t, d = x.shape
    t2 = t // DFT_T1
    xv = x.reshape(b, t2, DFT_T1 * d)
    av = a.reshape(b, 2, DFT_T1, t2, d)
    out = pl.pallas_call(
        functools.partial(_dft_ffn_body, n1=n1),
        grid=(b, DFT_T1 // n1),
        in_specs=[
            pl.BlockSpec((1, t2, n1 * d), lambda bi, jj: (bi, 0, jj)),
            pl.BlockSpec((1, 2, n1, t2, d), lambda bi, jj: (bi, 0, jj, 0, 0)),
            _mod_spec(layer, lambda bi: bi),
            _const_spec((2 * t2, 2 * t2)),
            _const_spec((FOURIER_GROUP_DIM, FOURIER_GROUP_DIM)),
            _const_spec((FOURIER_GROUP_DIM, FOURIER_GROUP_DIM)),
            _const_spec(wf.shape, j),
            _const_spec((1, d)),
            _const_spec(w1.shape, layer),
            _const_spec(w3.shape, layer),
            _const_spec(w2.shape, layer),
        ],
        out_specs=pl.BlockSpec((1, t2, n1 * d), lambda bi, jj: (bi, 0, jj)),
        out_shape=jax.ShapeDtypeStruct((b, t2, DFT_T1 * d), f32),
        compiler_params=_cparams(("arbitrary", "arbitrary")),
        name="dft_ffn",
    )(xv, av, mods, tabs["m2"], tabs["cc"], tabs["sc"], wf, n2g, w1, w3, w2)
    return out.reshape(b, t, d)


def _angle(i, j, n):
    return (2.0 * math.pi / n) * ((i * j) % n).astype(f32)


def _dft_tables(t):
    t1, t2 = DFT_T1, t // DFT_T1
    i1 = jnp.arange(t1, dtype=jnp.int32)
    ang64 = _angle(i1[:, None], i1[None, :], t1)
    s1 = t1 ** -0.5
    cs64 = jnp.concatenate([jnp.cos(ang64) * s1, -jnp.sin(ang64) * s1], axis=0).astype(bf16)
    i2 = jnp.arange(t2, dtype=jnp.int32)
    angt = _angle(i1[:, None], i2[None, :], t)
    tw_cos = jnp.repeat(jnp.cos(angt), LANES, axis=1)
    tw_sin = jnp.repeat(jnp.sin(angt), LANES, axis=1)
    ang2 = _angle(i2[:, None], i2[None, :], t2)
    s2 = t2 ** -0.5
    c2, sn2 = jnp.cos(ang2) * s2, jnp.sin(ang2) * s2
    m2 = jnp.concatenate([jnp.concatenate([c2, sn2], axis=1),
                          jnp.concatenate([-sn2, c2], axis=1)], axis=0).astype(bf16)
    ic = jnp.arange(FOURIER_GROUP_DIM, dtype=jnp.int32)
    angc = _angle(ic[:, None], ic[None, :], FOURIER_GROUP_DIM)
    sc_ = FOURIER_GROUP_DIM ** -0.5
    return dict(cs64=cs64, tw_cos=tw_cos, tw_sin=tw_sin, m2=m2,
                cc=(jnp.cos(angc) * sc_).astype(bf16), sc=(jnp.sin(angc) * sc_).astype(bf16))


def _rope_tables(t):
    rows = t // GRID_W
    half = QK_ROPE_DIM // 2
    inv = 1.0 / (ROPE_BASE ** (jnp.arange(0, half, 2, dtype=f32) / half))
    ar = jnp.arange(rows, dtype=f32)[:, None] * inv
    ac = jnp.arange(GRID_W, dtype=f32)[:, None] * inv
    pad_hi = HEAD_PAD - QK_HEAD_DIM

    def place(n, first, second, at, lead):
        z = jnp.zeros((n, half), f32)
        blocks = [jnp.full((n, QK_NOPE_DIM), lead, f32)]
        blocks += [jnp.concatenate([first, second], axis=1), z] if at == 0 else \
                  [z, jnp.concatenate([first, second], axis=1)]
        blocks.append(jnp.zeros((n, pad_hi), f32))
        return jnp.concatenate(blocks, axis=1)

    rrow = jnp.stack([place(rows, jnp.cos(ar), jnp.cos(ar), 0, 0.0),
                      place(rows, -jnp.sin(ar), jnp.sin(ar), 0, 0.0)])
    rcol = jnp.stack([place(GRID_W, jnp.cos(ac), jnp.cos(ac), 1, 1.0),
                      place(GRID_W, -jnp.sin(ac), jnp.sin(ac), 1, 0.0)])
    return rrow, rcol


def _partner(a, axis):
    parts = jnp.split(a, 4, axis=axis)
    return jnp.concatenate([parts[1], parts[0], parts[3], parts[2]], axis=axis)


def _prep_even(j, w_in, q_norm_g, kv_norm_g, w_uq, w_ukv, q_gain, k_gain, w_o):
    hh, hp = MLA_HEADS, HEAD_PAD
    d = D_MODEL
    nope, hd = QK_NOPE_DIM, QK_HEAD_DIM
    wi = w_in[j]
    w_pe = wi[:, PE_COL:CONV_OFFSET]
    place = lambda wcols: jnp.zeros((d, hp), f32).at[:, nope:hd].set(wcols)
    w_in_p = jnp.concatenate([wi[:, :PE_COL], place(w_pe), place(_partner(w_pe, 1)),
                              wi[:, CONV_OFFSET:]], axis=1).astype(bf16)
    wq = w_uq[j].reshape(Q_LORA_RANK, hh, hd)
    pad_head = lambda a: jnp.pad(a, ((0, 0), (0, 0), (0, hp - a.shape[2]))).reshape(a.shape[0], hh * hp)
    wq_sw = jnp.zeros_like(wq).at[:, :, nope:].set(_partner(wq[:, :, nope:], 2))
    w_uq_p = jnp.concatenate([pad_head(wq), pad_head(wq_sw)], axis=1).astype(bf16)
    wkv = w_ukv[j].reshape(KV_LORA_RANK, hh, nope + V_HEAD_DIM)
    w_ukv_p = jnp.concatenate([pad_head(wkv[:, :, :nope]), pad_head(wkv[:, :, nope:])], axis=1).astype(bf16)
    woa = jnp.pad(w_o[j][:hh * V_HEAD_DIM].reshape(hh, V_HEAD_DIM, d), ((0, 0), (0, hp - V_HEAD_DIM), (0, 0)))

    def gain_rows(g):
        sw = jnp.zeros_like(g).at[nope:].set(_partner(g[nope:], 0))
        return [jnp.pad(g, (0, hp - hd)), jnp.pad(sw, (0, hp - hd))]

    zero = jnp.zeros((hp,), f32)
    gains = jnp.stack(gain_rows(q_gain[j] * (QK_SCALE * LOG2E)) + gain_rows(k_gain[j]) + [zero] * 4)
    return dict(
        w_in=w_in_p, w_uq=w_uq_p, w_ukv=w_ukv_p,
        q_norm_g=q_norm_g[j].reshape(1, -1), kv_norm_g=kv_norm_g[j].reshape(1, -1), gains=gains,
        w_o_attn=woa.astype(bf16), w_o_conv=w_o[j][hh * V_HEAD_DIM:].astype(bf16))


def _pick(n, pref):
    return pref if n % pref == 0 else n


def kernel(x, c, ctx, c_ctx, ada_w, ada_b, norm1_g, norm2_g, w_in, q_norm_g, kv_norm_g, w_uq, w_ukv,
           q_gain, k_gain, conv_w, w_o, w_fourier, ffn_w1, ffn_w3, ffn_w2):
    b, s, d = x.shape
    depth = ada_w.shape[0]
    cvec = jnp.zeros((8, d), f32).at[:b].set(c).at[b].set(c_ctx)
    mods = _ada(cvec, ada_w, ada_b).reshape(depth, 8, 1, 6 * d)
    lat_row = lambda bi: bi
    ctx_row = lambda bi: b
    w1 = ffn_w1.astype(bf16)
    w3 = ffn_w3.astype(bf16)
    w2 = ffn_w2.astype(bf16)
    wf = w_fourier.astype(bf16)
    for i in range(depth):
        last = i == depth - 1
        j = i // 2
        n1g = norm1_g[i].reshape(1, d)
        n2g = norm2_g[i].reshape(1, d)
        if i % 2 == 0:
            w = _prep_even(j, w_in, q_norm_g, kv_norm_g, w_uq, w_ukv, q_gain, k_gain, w_o)
            tm = _pick(s, 512)
            lc = ctx.shape[1]
            q_l, k_l, v_l, pc_l = _qkv(x, mods, i, lat_row, n1g, w, _rope_tables(s), tm)
            q_c, k_c, v_c, pc_c = _qkv(ctx, mods, i, ctx_row, n1g, w, None, lc)
            a_l = _attention(q_l, [(k_l, v_l), (k_c, v_c)], _pick(s, 512), _pick(s, 1024))
            x_new = _mix_ffn(x, a_l, pc_l, mods, i, lat_row, conv_w, j, w, n2g, w1, w3, w2, tm)
            if not last:
                a_c = _attention(q_c, [(k_c, v_c)], lc, lc)
                ctx = _mix_ffn(ctx, a_c, pc_c, mods, i, ctx_row, conv_w, j, w, n2g, w1, w3, w2, lc)
            x = x_new
        else:
            tabs = _dft_tables(s)
            t2 = s // DFT_T1
            a = _dft_a(x, mods, i, n1g, tabs, _pick(t2, 16))
            x = _dft_ffn(x, a, mods, i, tabs, wf, j, n2g, w1, w3, w2, 4)
            assert last, "odd non-final layers are not implemented"
    return x
```

```python
import functools
import math

import jax
import jax.numpy as jnp
from jax import lax
from jax.experimental import pallas as pl
from jax.experimental.pallas import tpu as pltpu

D_MODEL = 1024
GRID_W = 64
MLA_HEADS = 8
QK_NOPE_DIM = 64
QK_ROPE_DIM = 32
QK_HEAD_DIM = QK_NOPE_DIM + QK_ROPE_DIM
V_HEAD_DIM = 64
Q_LORA_RANK = 384
KV_LORA_RANK = 256
QK_SCALE = QK_HEAD_DIM ** -0.5
ROPE_BASE = 10000.0
CONV_DIM = 512
CONV_OFFSET = Q_LORA_RANK + KV_LORA_RANK + QK_ROPE_DIM
FOURIER_GROUPS = 4
FOURIER_GROUP_DIM = D_MODEL // FOURIER_GROUPS
EPS = 1e-6

LANES = 128
BF16_SUBLANES = 16
VMEM_LIMIT_BYTES = 56 * 1024 * 1024

HEAD_PAD = LANES
ALL_HEADS = MLA_HEADS * HEAD_PAD
PE_COL = Q_LORA_RANK + KV_LORA_RANK
PE_SW_COL = PE_COL + HEAD_PAD
CONV_COL = PE_SW_COL + HEAD_PAD
IN_PROJ_PAD = CONV_COL + 3 * CONV_DIM
QKV_ROW_BLOCK = 128
DFT_T1 = 64
LOG2E = math.log2(math.e)
EXP2_SAFE_SCORE = 64.0
BF16_ROUNDING_SLACK = 1.02
NT_DIMS = (((1,), (1,)), ((), ()))

bf16 = jnp.bfloat16
f32 = jnp.float32


def _cparams(sem):
    return pltpu.CompilerParams(dimension_semantics=sem, vmem_limit_bytes=VMEM_LIMIT_BYTES)


def _const_spec(shape, lead=None):
    if lead is None:
        nd = len(shape)
        return pl.BlockSpec(shape, lambda *_: (0,) * nd, pipeline_mode=pl.Buffered(1))
    nd = len(shape) - 1
    return pl.BlockSpec((1,) + tuple(shape[1:]), lambda *_: (lead,) + (0,) * nd,
                        pipeline_mode=pl.Buffered(1))


def _mod_spec(layer, mod_row):
    return pl.BlockSpec((1, 1, 1, 6 * D_MODEL), lambda bi, i: (layer, mod_row(bi), 0, 0))


def _dot(a, b):
    return jnp.dot(a, b, preferred_element_type=f32)


def _rms_scale(x, width):
    return lax.rsqrt(jnp.sum(x * x, axis=-1, keepdims=True) * (1.0 / width) + EPS)


def _ada_body(c_ref, w_ref, b_ref, o_ref):
    c = c_ref[...]
    s = c / (1.0 + jnp.exp(-c))
    o_ref[0] = _dot(s.astype(bf16), w_ref[0].astype(bf16)) + b_ref[0]


def _ada(cvec, ada_w, ada_b):
    depth, d, n = ada_w.shape
    tn = 1536
    return pl.pallas_call(
        _ada_body,
        grid=(depth, n // tn),
        in_specs=[
            pl.BlockSpec((8, d), lambda l, j: (0, 0)),
            pl.BlockSpec((1, d, tn), lambda l, j: (l, 0, j)),
            pl.BlockSpec((1, 1, tn), lambda l, j: (l, 0, j)),
        ],
        out_specs=pl.BlockSpec((1, 8, tn), lambda l, j: (l, 0, j)),
        out_shape=jax.ShapeDtypeStruct((depth, 8, n), f32),
        compiler_params=_cparams(("arbitrary", "arbitrary")),
        name="ada",
    )(cvec, ada_w, ada_b.reshape(depth, 1, n))


def _modulated_norm(x, mod, g_ref, lo):
    d = D_MODEL
    shift = mod[:, lo:lo + d]
    scale = mod[:, lo + d:lo + 2 * d]
    gain = g_ref[...] * (1.0 + scale)
    return (x * _rms_scale(x, d)) * gain + shift


def _qkv_body(*refs, rope):
    (x_ref, mod_ref, n1g_ref, win_ref, qng_ref, kvng_ref, wuq_ref, wukv_ref, gains_ref) = refs[:9]
    if rope:
        rrow_ref, rcol_ref = refs[9:11]
        refs = refs[11:]
    else:
        refs = refs[9:]
    q_out, k_out, v_out, pc_out, q_scr, kv_scr, pe_scr = refs
    x = x_ref[0]
    tm = x.shape[0]
    h = _modulated_norm(x, mod_ref[0, 0], n1g_ref, 0)
    p = _dot(h.astype(bf16), win_ref[...])
    pc_out[0] = p[:, CONV_COL:].astype(bf16)
    pe_scr[...] = p[:, PE_COL:CONV_COL]
    cq = p[:, :Q_LORA_RANK]
    cqn = cq * _rms_scale(cq, Q_LORA_RANK) * qng_ref[...]
    q_scr[...] = _dot(cqn.astype(bf16), wuq_ref[...])
    ckv = p[:, Q_LORA_RANK:PE_COL]
    ckvn = ckv * _rms_scale(ckv, KV_LORA_RANK) * kvng_ref[...]
    kv_scr[...] = _dot(ckvn.astype(bf16), wukv_ref[...])

    gains = gains_ref[...]
    lane = lax.broadcasted_iota(jnp.int32, (1, HEAD_PAD), 1)
    ones_col = (lane == V_HEAD_DIM).astype(f32)
    rb = min(QKV_ROW_BLOCK, tm)
    per = rb // GRID_W
    grid_row0 = pl.program_id(1) * (tm // GRID_W)

    def block(ib, carry):
        r0 = pl.multiple_of(ib * rb, rb)
        rows = pl.ds(r0, rb)
        if rope:
            tabs = []
            for comp in range(2):
                parts = []
                for g in range(per):
                    rr = rrow_ref[comp, pl.ds(grid_row0 + ib * per + g, 1), :]
                    parts.append(rcol_ref[comp] + rr)
                tabs.append(jnp.concatenate(parts, axis=0))
            cos, sin = tabs
            qa, qb = cos * gains[0:1], sin * gains[1:2]
            ka, kb = cos * gains[2:3], sin * gains[3:4]
        pe = pe_scr[rows, :HEAD_PAD]
        pe_sw = pe_scr[rows, HEAD_PAD:]
        for hd in range(MLA_HEADS):
            sl = slice(hd * HEAD_PAD, (hd + 1) * HEAD_PAD)
            sl2 = slice(ALL_HEADS + hd * HEAD_PAD, ALL_HEADS + (hd + 1) * HEAD_PAD)
            qh = q_scr[rows, sl]
            kh = kv_scr[rows, sl] + pe
            rq = _rms_scale(qh, QK_HEAD_DIM)
            rk = _rms_scale(kh, QK_HEAD_DIM)
            if rope:
                qo = (qh * qa + q_scr[rows, sl2] * qb) * rq
                ko = (kh * ka + pe_sw * kb) * rk
            else:
                qo = qh * gains[0:1] * rq
                ko = kh * gains[2:3] * rk
            q_out[0, hd, rows, :] = qo.astype(bf16)
            k_out[0, hd, rows, :] = ko.astype(bf16)
            v_out[0, hd, rows, :] = (kv_scr[rows, sl2] + ones_col).astype(bf16)
        return carry

    lax.fori_loop(0, tm // rb, block, 0)


def _qkv(x, mods, layer, mod_row, n1g, w, rope_tabs, tm):
    b, t, d = x.shape
    hh, hp = MLA_HEADS, HEAD_PAD
    rope = rope_tabs is not None
    in_specs = [
        pl.BlockSpec((1, tm, d), lambda bi, i: (bi, i, 0)),
        _mod_spec(layer, mod_row),
        _const_spec((1, d)),
        _const_spec(w["w_in"].shape),
        _const_spec((1, Q_LORA_RANK)),
        _const_spec((1, KV_LORA_RANK)),
        _const_spec(w["w_uq"].shape),
        _const_spec(w["w_ukv"].shape),
        _const_spec((8, hp)),
    ]
    args = [x, mods, n1g, w["w_in"], w["q_norm_g"], w["kv_norm_g"], w["w_uq"], w["w_ukv"], w["gains"]]
    if rope:
        rrow, rcol = rope_tabs
        in_specs += [_const_spec(rrow.shape), _const_spec(rcol.shape)]
        args += [rrow, rcol]
    head_spec = pl.BlockSpec((1, hh, tm, hp), lambda bi, i: (bi, 0, i, 0))
    head_shape = jax.ShapeDtypeStruct((b, hh, t, hp), bf16)
    return pl.pallas_call(
        functools.partial(_qkv_body, rope=rope),
        grid=(b, t // tm),
        in_specs=in_specs,
        out_specs=[head_spec, head_spec, head_spec,
                   pl.BlockSpec((1, tm, 3 * CONV_DIM), lambda bi, i: (bi, i, 0))],
        out_shape=[head_shape, head_shape, head_shape,
                   jax.ShapeDtypeStruct((b, t, 3 * CONV_DIM), bf16)],
        scratch_shapes=[pltpu.VMEM((tm, 2 * ALL_HEADS), f32), pltpu.VMEM((tm, 2 * ALL_HEADS), f32),
                        pltpu.VMEM((tm, 2 * hp), f32)],
        compiler_params=_cparams(("arbitrary", "arbitrary")),
        name="qkv_rope" if rope else "qkv_ctx",
    )(*args)


def _attn_body(q_ref, *refs, chunks, online):
    o_ref = refs[-1]
    q = q_ref[0, 0]
    m = None
    acc = None
    for src, (n_chunks, tk) in enumerate(chunks):
        k_ref, v_ref = refs[2 * src], refs[2 * src + 1]
        for c in range(n_chunks):
            ks = slice(c * tk, (c + 1) * tk)
            s = lax.dot_general(q, k_ref[0, 0, ks, :], NT_DIMS, preferred_element_type=f32)
            if online:
                m_cur = jnp.max(s, axis=1, keepdims=True)
                m_new = m_cur if m is None else jnp.maximum(m, m_cur)
                pv = _dot(jnp.exp2(s - m_new).astype(bf16), v_ref[0, 0, ks, :])
                acc = pv if m is None else jnp.exp2(m - m_new) * acc + pv
                m = m_new
            else:
                pv = _dot(jnp.exp2(s).astype(bf16), v_ref[0, 0, ks, :])
                acc = pv if acc is None else acc + pv
    o_ref[0, 0] = (acc / acc[:, V_HEAD_DIM:V_HEAD_DIM + 1]).astype(bf16)


def _attention(q, kv_sources, tq, tk, online=True):
    b, hh, t, hp = q.shape
    in_specs = [pl.BlockSpec((1, 1, tq, hp), lambda bi, hi, i: (bi, hi, i, 0))]
    args = [q]
    chunks = []
    for k, v in kv_sources:
        n = k.shape[2]
        step = min(tk, n)
        chunks.append((n // step, step))
        in_specs += [pl.BlockSpec((1, 1, n, hp), lambda bi, hi, i: (bi, hi, 0, 0))] * 2
        args += [k, v]
    return pl.pallas_call(
        functools.partial(_attn_body, chunks=tuple(chunks), online=online),
        grid=(b, hh, t // tq),
        in_specs=in_specs,
        out_specs=pl.BlockSpec((1, 1, tq, hp), lambda bi, hi, i: (bi, hi, i, 0)),
        out_shape=jax.ShapeDtypeStruct((b, hh, t, hp), bf16),
        compiler_params=_cparams(("arbitrary", "arbitrary", "arbitrary")),
        name="attn%d%s" % (len(kv_sources), "" if online else "_bounded"),
    )(*args)


def _gated_ffn(x, out, mod, n2g_ref, w1_ref, w3_ref, w2_ref):
    d = D_MODEL
    x1 = x + mod[:, 2 * d:3 * d] * out
    h2 = _modulated_norm(x1, mod, n2g_ref, 3 * d).astype(bf16)
    d_ff = w1_ref.shape[2]
    ff_chunk = _ff_chunk(d_ff)
    y = None
    for c in range(d_ff // ff_chunk):
        sl = slice(c * ff_chunk, (c + 1) * ff_chunk)
        u = _dot(h2, w1_ref[0, :, sl])
        g = _dot(h2, w3_ref[0, :, sl])
        act = (u / (1.0 + jnp.exp(-u)) * g).astype(bf16)
        part = _dot(act, w2_ref[0, sl, :])
        y = part if y is None else y + part
    return x1 + mod[:, 5 * d:6 * d] * y


def _ff_chunk(d_ff):
    for n in (2, 4, 1):
        if d_ff % (n * LANES) == 0:
            return d_ff // n
    return d_ff


def _mix_ffn_body(*refs, halo):
    x_ref, a_ref, pc_ref = refs[:3]
    if halo:
        pprev_ref, pnext_ref = refs[3:5]
        refs = refs[5:]
    else:
        refs = refs[3:]
    (mod_ref, convw_ref, woa_ref, woc_ref, n2g_ref, w1_ref, w3_ref, w2_ref, o_ref) = refs
    x = x_ref[0]
    mod = mod_ref[0, 0]
    tm = x.shape[0]

    out = None
    for hd in range(MLA_HEADS):
        part = _dot(a_ref[0, hd], woa_ref[hd])
        out = part if out is None else out + part

    cd = CONV_DIM
    pc = pc_ref[0].astype(f32)
    z = pc[:, cd:2 * cd] * pc[:, 2 * cd:]
    row = lax.broadcasted_iota(jnp.int32, (tm, cd), 0)
    if halo:
        i = pl.program_id(1)
        last = pl.num_programs(1) - 1
        pp = pprev_ref[0, BF16_SUBLANES - 1:BF16_SUBLANES, :].astype(f32)
        pn = pnext_ref[0, 0:1, :].astype(f32)
        z_prev = jnp.where(i > 0, pp[:, cd:2 * cd] * pp[:, 2 * cd:], 0.0)
        z_next = jnp.where(i < last, pn[:, cd:2 * cd] * pn[:, 2 * cd:], 0.0)
    else:
        z_prev = jnp.zeros((1, cd), f32)
        z_next = jnp.zeros((1, cd), f32)
    z_up = jnp.where(row == 0, z_prev, pltpu.roll(z, 1, axis=0))
    z_dn = jnp.where(row == tm - 1, z_next, pltpu.roll(z, tm - 1, axis=0))
    cw = convw_ref[0]
    y = z_up * cw[0:1, :] + z * cw[1:2, :] + z_dn * cw[2:3, :]
    out = out + _dot((pc[:, :cd] * y).astype(bf16), woc_ref[...])

    o_ref[0] = _gated_ffn(x, out, mod, n2g_ref, w1_ref, w3_ref, w2_ref)


def _mix_ffn(x, a, pc, mods, layer, mod_row, conv_w, j, w, n2g, w1, w3, w2, tm):
    b, t, d = x.shape
    hh, hp = MLA_HEADS, HEAD_PAD
    halo = t > tm
    in_specs = [
        pl.BlockSpec((1, tm, d), lambda bi, i: (bi, i, 0)),
        pl.BlockSpec((1, hh, tm, hp), lambda bi, i: (bi, 0, i, 0)),
        pl.BlockSpec((1, tm, 3 * CONV_DIM), lambda bi, i: (bi, i, 0)),
    ]
    args = [x, a, pc]
    if halo:
        per = tm // BF16_SUBLANES
        nblk = t // BF16_SUBLANES
        in_specs += [
            pl.BlockSpec((1, BF16_SUBLANES, 3 * CONV_DIM),
                         lambda bi, i: (bi, jnp.maximum(i * per - 1, 0), 0)),
            pl.BlockSpec((1, BF16_SUBLANES, 3 * CONV_DIM),
                         lambda bi, i: (bi, jnp.minimum((i + 1) * per, nblk - 1), 0)),
        ]
        args += [pc, pc]
    in_specs += [
        _mod_spec(layer, mod_row),
        _const_spec(conv_w.shape, j),
        _const_spec(w["w_o_attn"].shape),
        _const_spec(w["w_o_conv"].shape),
        _const_spec((1, d)),
        _const_spec(w1.shape, layer),
        _const_spec(w3.shape, layer),
        _const_spec(w2.shape, layer),
    ]
    args += [mods, conv_w, w["w_o_attn"], w["w_o_conv"], n2g, w1, w3, w2]
    return pl.pallas_call(
        functools.partial(_mix_ffn_body, halo=halo),
        grid=(b, t // tm),
        in_specs=in_specs,
        out_specs=pl.BlockSpec((1, tm, d), lambda bi, i: (bi, i, 0)),
        out_shape=jax.ShapeDtypeStruct((b, t, d), f32),
        compiler_params=_cparams(("arbitrary", "arbitrary")),
        name="mix_ffn_halo" if halo else "mix_ffn",
    )(*args)


def _dft_a_body(x_ref, mod_ref, n1g_ref, cs_ref, twc_ref, tws_ref, o_ref, *, k):
    d = D_MODEL
    mod = mod_ref[0, 0]
    cs = cs_ref[...]
    reps = d // LANES
    for j in range(k):
        x = x_ref[0, :, j * d:(j + 1) * d]
        h = _modulated_norm(x, mod, n1g_ref, 0)
        a = _dot(cs, h.astype(bf16))
        a_re, a_im = a[:DFT_T1], a[DFT_T1:]
        c = jnp.tile(twc_ref[:, j * LANES:(j + 1) * LANES], (1, reps))
        s = jnp.tile(tws_ref[:, j * LANES:(j + 1) * LANES], (1, reps))
        o_ref[0, 0, :, j * d:(j + 1) * d] = (a_re * c + a_im * s).astype(bf16)
        o_ref[0, 1, :, j * d:(j + 1) * d] = (a_im * c - a_re * s).astype(bf16)


def _dft_a(x, mods, layer, n1g, tabs, k):
    b, t, d = x.shape
    t2 = t // DFT_T1
    xv = x.reshape(b, DFT_T1, t2 * d)
    return pl.pallas_call(
        functools.partial(_dft_a_body, k=k),
        grid=(b, t2 // k),
        in_specs=[
            pl.BlockSpec((1, DFT_T1, k * d), lambda bi, j: (bi, 0, j)),
            _mod_spec(layer, lambda bi: bi),
            _const_spec((1, d)),
            _const_spec((2 * DFT_T1, DFT_T1)),
            pl.BlockSpec((DFT_T1, k * LANES), lambda bi, j: (0, j)),
            pl.BlockSpec((DFT_T1, k * LANES), lambda bi, j: (0, j)),
        ],
        out_specs=pl.BlockSpec((1, 2, DFT_T1, k * d), lambda bi, j: (bi, 0, 0, j)),
        out_shape=jax.ShapeDtypeStruct((b, 2, DFT_T1, t2 * d), bf16),
        compiler_params=_cparams(("arbitrary", "arbitrary")),
        name="dft_a",
    )(xv, mods, n1g, tabs["cs64"], tabs["tw_cos"], tabs["tw_sin"])


def _dft_ffn_body(x_ref, a_ref, mod_ref, m_ref, cc_ref, sc_ref, wf_ref, n2g_ref,
                  w1_ref, w3_ref, w2_ref, o_ref, *, n1):
    d = D_MODEL
    gd = FOURIER_GROUP_DIM
    t2 = a_ref.shape[3]
    mod = mod_ref[0, 0]
    mm = m_ref[...]
    fs = []
    for i in range(n1):
        bc = _dot(mm[:, :t2], a_ref[0, 0, i]) + _dot(mm[:, t2:], a_ref[0, 1, i])
        b_re = bc[:t2].astype(bf16)
        b_im = bc[t2:].astype(bf16)
        cols = []
        for g in range(FOURIER_GROUPS):
            sl = slice(g * gd, (g + 1) * gd)
            cols.append(_dot(b_re[:, sl], cc_ref[...]) + _dot(b_im[:, sl], sc_ref[...]))
        fs.append(jnp.concatenate(cols, axis=1))
    f = jnp.concatenate(fs, axis=0).astype(bf16)
    out = _dot(f, wf_ref[0])
    x = jnp.concatenate([x_ref[0, :, i * d:(i + 1) * d] for i in range(n1)], axis=0)
    res = _gated_ffn(x, out, mod, n2g_ref, w1_ref, w3_ref, w2_ref)
    for i in range(n1):
        o_ref[0, :, i * d:(i + 1) * d] = res[i * t2:(i + 1) * t2]


def _dft_ffn(x, a, mods, layer, tabs, wf, j, n2g, w1, w3, w2, n1):
    b, t, d = x.shape
    t2 = t // DFT_T1
    xv = x.reshape(b, t2, DFT_T1 * d)
    av = a.reshape(b, 2, DFT_T1, t2, d)
    out = pl.pallas_call(
        functools.partial(_dft_ffn_body, n1=n1),
        grid=(b, DFT_T1 // n1),
        in_specs=[
            pl.BlockSpec((1, t2, n1 * d), lambda bi, jj: (bi, 0, jj)),
            pl.BlockSpec((1, 2, n1, t2, d), lambda bi, jj: (bi, 0, jj, 0, 0)),
            _mod_spec(layer, lambda bi: bi),
            _const_spec((2 * t2, 2 * t2)),
            _const_spec((FOURIER_GROUP_DIM, FOURIER_GROUP_DIM)),
            _const_spec((FOURIER_GROUP_DIM, FOURIER_GROUP_DIM)),
            _const_spec(wf.shape, j),
            _const_spec((1, d)),
            _const_spec(w1.shape, layer),
            _const_spec(w3.shape, layer),
            _const_spec(w2.shape, layer),
        ],
        out_specs=pl.BlockSpec((1, t2, n1 * d), lambda bi, jj: (bi, 0, jj)),
        out_shape=jax.ShapeDtypeStruct((b, t2, DFT_T1 * d), f32),
        compiler_params=_cparams(("arbitrary", "arbitrary")),
        name="dft_ffn",
    )(xv, av, mods, tabs["m2"], tabs["cc"], tabs["sc"], wf, n2g, w1, w3, w2)
    return out.reshape(b, t, d)


def _angle(i, j, n):
    return (2.0 * math.pi / n) * ((i * j) % n).astype(f32)


def _dft_tables(t):
    t1, t2 = DFT_T1, t // DFT_T1
    i1 = jnp.arange(t1, dtype=jnp.int32)
    ang64 = _angle(i1[:, None], i1[None, :], t1)
    s1 = t1 ** -0.5
    cs64 = jnp.concatenate([jnp.cos(ang64) * s1, -jnp.sin(ang64) * s1], axis=0).astype(bf16)
    i2 = jnp.arange(t2, dtype=jnp.int32)
    angt = _angle(i1[:, None], i2[None, :], t)
    tw_cos = jnp.repeat(jnp.cos(angt), LANES, axis=1)
    tw_sin = jnp.repeat(jnp.sin(angt), LANES, axis=1)
    ang2 = _angle(i2[:, None], i2[None, :], t2)
    s2 = t2 ** -0.5
    c2, sn2 = jnp.cos(ang2) * s2, jnp.sin(ang2) * s2
    m2 = jnp.concatenate([jnp.concatenate([c2, sn2], axis=1),
                          jnp.concatenate([-sn2, c2], axis=1)], axis=0).astype(bf16)
    ic = jnp.arange(FOURIER_GROUP_DIM, dtype=jnp.int32)
    angc = _angle(ic[:, None], ic[None, :], FOURIER_GROUP_DIM)
    sc_ = FOURIER_GROUP_DIM ** -0.5
    return dict(cs64=cs64, tw_cos=tw_cos, tw_sin=tw_sin, m2=m2,
                cc=(jnp.cos(angc) * sc_).astype(bf16), sc=(jnp.sin(angc) * sc_).astype(bf16))


def _rope_tables(t):
    rows = t // GRID_W
    half = QK_ROPE_DIM // 2
    inv = 1.0 / (ROPE_BASE ** (jnp.arange(0, half, 2, dtype=f32) / half))
    ar = jnp.arange(rows, dtype=f32)[:, None] * inv
    ac = jnp.arange(GRID_W, dtype=f32)[:, None] * inv
    pad_hi = HEAD_PAD - QK_HEAD_DIM

    def place(n, first, second, at, lead):
        z = jnp.zeros((n, half), f32)
        blocks = [jnp.full((n, QK_NOPE_DIM), lead, f32)]
        blocks += [jnp.concatenate([first, second], axis=1), z] if at == 0 else \
                  [z, jnp.concatenate([first, second], axis=1)]
        blocks.append(jnp.zeros((n, pad_hi), f32))
        return jnp.concatenate(blocks, axis=1)

    rrow = jnp.stack([place(rows, jnp.cos(ar), jnp.cos(ar), 0, 0.0),
                      place(rows, -jnp.sin(ar), jnp.sin(ar), 0, 0.0)])
    rcol = jnp.stack([place(GRID_W, jnp.cos(ac), jnp.cos(ac), 1, 1.0),
                      place(GRID_W, -jnp.sin(ac), jnp.sin(ac), 1, 0.0)])
    return rrow, rcol


def _partner(a, axis):
    parts = jnp.split(a, 4, axis=axis)
    return jnp.concatenate([parts[1], parts[0], parts[3], parts[2]], axis=axis)


def _prep_even(j, w_in, q_norm_g, kv_norm_g, w_uq, w_ukv, q_gain, k_gain, w_o):
    hh, hp = MLA_HEADS, HEAD_PAD
    d = D_MODEL
    nope, hd = QK_NOPE_DIM, QK_HEAD_DIM
    wi = w_in[j]
    w_pe = wi[:, PE_COL:CONV_OFFSET]
    place = lambda wcols: jnp.zeros((d, hp), f32).at[:, nope:hd].set(wcols)
    w_in_p = jnp.concatenate([wi[:, :PE_COL], place(w_pe), place(_partner(w_pe, 1)),
                              wi[:, CONV_OFFSET:]], axis=1).astype(bf16)
    wq = w_uq[j].reshape(Q_LORA_RANK, hh, hd)
    pad_head = lambda a: jnp.pad(a, ((0, 0), (0, 0), (0, hp - a.shape[2]))).reshape(a.shape[0], hh * hp)
    wq_sw = jnp.zeros_like(wq).at[:, :, nope:].set(_partner(wq[:, :, nope:], 2))
    w_uq_p = jnp.concatenate([pad_head(wq), pad_head(wq_sw)], axis=1).astype(bf16)
    wkv = w_ukv[j].reshape(KV_LORA_RANK, hh, nope + V_HEAD_DIM)
    w_ukv_p = jnp.concatenate([pad_head(wkv[:, :, :nope]), pad_head(wkv[:, :, nope:])], axis=1).astype(bf16)
    woa = jnp.pad(w_o[j][:hh * V_HEAD_DIM].reshape(hh, V_HEAD_DIM, d), ((0, 0), (0, hp - V_HEAD_DIM), (0, 0)))

    def gain_rows(g):
        sw = jnp.zeros_like(g).at[nope:].set(_partner(g[nope:], 0))
        return [jnp.pad(g, (0, hp - hd)), jnp.pad(sw, (0, hp - hd))]

    zero = jnp.zeros((hp,), f32)
    gains = jnp.stack(gain_rows(q_gain[j] * (QK_SCALE * LOG2E)) + gain_rows(k_gain[j]) + [zero] * 4)
    return dict(
        w_in=w_in_p, w_uq=w_uq_p, w_ukv=w_ukv_p,
        q_norm_g=q_norm_g[j].reshape(1, -1), kv_norm_g=kv_norm_g[j].reshape(1, -1), gains=gains,
        w_o_attn=woa.astype(bf16), w_o_conv=w_o[j][hh * V_HEAD_DIM:].astype(bf16))


def _pick(n, pref):
    return pref if n % pref == 0 else n


def kernel(x, c, ctx, c_ctx, ada_w, ada_b, norm1_g, norm2_g, w_in, q_norm_g, kv_norm_g, w_uq, w_ukv,
           q_gain, k_gain, conv_w, w_o, w_fourier, ffn_w1, ffn_w3, ffn_w2):
    b, s, d = x.shape
    depth = ada_w.shape[0]
    cvec = jnp.zeros((8, d), f32).at[:b].set(c).at[b].set(c_ctx)
    mods = _ada(cvec, ada_w, ada_b).reshape(depth, 8, 1, 6 * d)
    lat_row = lambda bi: bi
    ctx_row = lambda bi: b
    w1 = ffn_w1.astype(bf16)
    w3 = ffn_w3.astype(bf16)
    w2 = ffn_w2.astype(bf16)
    wf = w_fourier.astype(bf16)
    for i in range(depth):
        last = i == depth - 1
        j = i // 2
        n1g = norm1_g[i].reshape(1, d)
        n2g = norm2_g[i].reshape(1, d)
        if i % 2 == 0:
            w = _prep_even(j, w_in, q_norm_g, kv_norm_g, w_uq, w_ukv, q_gain, k_gain, w_o)
            tm = _pick(s, 512)
            lc = ctx.shape[1]
            q_l, k_l, v_l, pc_l = _qkv(x, mods, i, lat_row, n1g, w, _rope_tables(s), tm)
            q_c, k_c, v_c, pc_c = _qkv(ctx, mods, i, ctx_row, n1g, w, None, lc)
            bound = (QK_SCALE * LOG2E * QK_HEAD_DIM * BF16_ROUNDING_SLACK
                     * jnp.max(jnp.abs(q_gain[j])) * jnp.max(jnp.abs(k_gain[j])))
            tq, tk = _pick(s, 512), _pick(s, 1024)
            a_l = lax.cond(
                bound < EXP2_SAFE_SCORE,
                lambda *ops: _attention(ops[0], [ops[1:3], ops[3:5]], _pick(s, 1024), _pick(s, 256),
                                        online=False),
                lambda *ops: _attention(ops[0], [ops[1:3], ops[3:5]], tq, tk, online=True),
                q_l, k_l, v_l, k_c, v_c)
            x_new = _mix_ffn(x, a_l, pc_l, mods, i, lat_row, conv_w, j, w, n2g, w1, w3, w2, tm)
            if not last:
                a_c = _attention(q_c, [(k_c, v_c)], lc, lc)
                ctx = _mix_ffn(ctx, a_c, pc_c, mods, i, ctx_row, conv_w, j, w, n2g, w1, w3, w2, lc)
            x = x_new
        else:
            tabs = _dft_tables(s)
            t2 = s // DFT_T1
            a = _dft_a(x, mods, i, n1g, tabs, _pick(t2, 16))
            x = _dft_ffn(x, a, mods, i, tabs, wf, j, n2g, w1, w3, w2, 4)
            assert last, "odd non-final layers are not implemented"
    return x
```

```python
import functools
import math

import jax
import jax.numpy as jnp
from jax import lax
from jax.experimental import pallas as pl
from jax.experimental.pallas import tpu as pltpu

D_MODEL = 1024
GRID_W = 64
MLA_HEADS = 8
QK_NOPE_DIM = 64
QK_ROPE_DIM = 32
QK_HEAD_DIM = QK_NOPE_DIM + QK_ROPE_DIM
V_HEAD_DIM = 64
Q_LORA_RANK = 384
KV_LORA_RANK = 256
QK_SCALE = QK_HEAD_DIM ** -0.5
ROPE_BASE = 10000.0
CONV_DIM = 512
CONV_OFFSET = Q_LORA_RANK + KV_LORA_RANK + QK_ROPE_DIM
FOURIER_GROUPS = 4
FOURIER_GROUP_DIM = D_MODEL // FOURIER_GROUPS
EPS = 1e-6

LANES = 128
BF16_SUBLANES = 16
MXU_TILE = 256
VMEM_LIMIT_BYTES = 56 * 1024 * 1024

HEAD_PAD = LANES
ALL_HEADS = MLA_HEADS * HEAD_PAD
PE_COL = Q_LORA_RANK + KV_LORA_RANK
PE_SW_COL = PE_COL + HEAD_PAD
CONV_COL = PE_SW_COL + HEAD_PAD
IN_PROJ_PAD = CONV_COL + 3 * CONV_DIM
QKV_ROW_BLOCK = 128
DFT_T1 = 64
LOG2E = math.log2(math.e)
EXP2_SAFE_SCORE = 64.0
BF16_ROUNDING_SLACK = 1.02
NT_DIMS = (((1,), (1,)), ((), ()))

bf16 = jnp.bfloat16
f32 = jnp.float32


def _cparams(sem):
    return pltpu.CompilerParams(dimension_semantics=sem, vmem_limit_bytes=VMEM_LIMIT_BYTES)


def _const_spec(shape, lead=None):
    if lead is None:
        nd = len(shape)
        return pl.BlockSpec(shape, lambda *_: (0,) * nd, pipeline_mode=pl.Buffered(1))
    nd = len(shape) - 1
    return pl.BlockSpec((1,) + tuple(shape[1:]), lambda *_: (lead,) + (0,) * nd,
                        pipeline_mode=pl.Buffered(1))


def _mod_spec(layer, mod_row):
    return pl.BlockSpec((1, 1, 1, 6 * D_MODEL), lambda bi, i: (layer, mod_row(bi), 0, 0))


def _dot(a, b):
    return jnp.dot(a, b, preferred_element_type=f32)


def _rms_scale(x, width):
    return lax.rsqrt(jnp.sum(x * x, axis=-1, keepdims=True) * (1.0 / width) + EPS)


def _ada_body(c_ref, w_ref, b_ref, o_ref):
    c = c_ref[...]
    s = c / (1.0 + jnp.exp(-c))
    o_ref[0] = _dot(s.astype(bf16), w_ref[0].astype(bf16)) + b_ref[0]


def _ada(cvec, ada_w, ada_b):
    depth, d, n = ada_w.shape
    tn = 1536
    return pl.pallas_call(
        _ada_body,
        grid=(depth, n // tn),
        in_specs=[
            pl.BlockSpec((8, d), lambda l, j: (0, 0)),
            pl.BlockSpec((1, d, tn), lambda l, j: (l, 0, j)),
            pl.BlockSpec((1, 1, tn), lambda l, j: (l, 0, j)),
        ],
        out_specs=pl.BlockSpec((1, 8, tn), lambda l, j: (l, 0, j)),
        out_shape=jax.ShapeDtypeStruct((depth, 8, n), f32),
        compiler_params=_cparams(("arbitrary", "arbitrary")),
        name="ada",
    )(cvec, ada_w, ada_b.reshape(depth, 1, n))


def _modulated_norm(x, mod, g_ref, lo):
    d = D_MODEL
    shift = mod[:, lo:lo + d]
    scale = mod[:, lo + d:lo + 2 * d]
    gain = g_ref[...] * (1.0 + scale)
    return (x * _rms_scale(x, d)) * gain + shift


def _qkv_body(*refs, rope):
    (x_ref, mod_ref, n1g_ref, win_ref, qng_ref, kvng_ref, wuq_ref, wukv_ref, gains_ref) = refs[:9]
    if rope:
        rrow_ref, rcol_ref = refs[9:11]
        refs = refs[11:]
    else:
        refs = refs[9:]
    q_out, k_out, v_out, pc_out, q_scr, kv_scr, pe_scr = refs
    x = x_ref[0]
    tm = x.shape[0]
    h = _modulated_norm(x, mod_ref[0, 0], n1g_ref, 0)
    p = _dot(h.astype(bf16), win_ref[...])
    pc_out[0] = p[:, CONV_COL:].astype(bf16)
    pe_scr[...] = p[:, PE_COL:CONV_COL]
    cq = p[:, :Q_LORA_RANK]
    cqn = cq * _rms_scale(cq, Q_LORA_RANK) * qng_ref[...]
    q_scr[...] = _dot(cqn.astype(bf16), wuq_ref[...])
    ckv = p[:, Q_LORA_RANK:PE_COL]
    ckvn = ckv * _rms_scale(ckv, KV_LORA_RANK) * kvng_ref[...]
    kv_scr[...] = _dot(ckvn.astype(bf16), wukv_ref[...])

    gains = gains_ref[...]
    lane = lax.broadcasted_iota(jnp.int32, (1, HEAD_PAD), 1)
    ones_col = [(lane == _denominator_lane(par)).astype(f32) for par in range(2)]
    rb = min(QKV_ROW_BLOCK, tm)
    per = rb // GRID_W
    grid_row0 = pl.program_id(1) * (tm // GRID_W)

    def block(ib, carry):
        r0 = pl.multiple_of(ib * rb, rb)
        rows = pl.ds(r0, rb)
        if rope:
            tabs = []
            for comp in range(2):
                parts = []
                for g in range(per):
                    rr = rrow_ref[comp, pl.ds(grid_row0 + ib * per + g, 1), :]
                    parts.append(rcol_ref[comp] + rr)
                tabs.append(jnp.concatenate(parts, axis=0))
            cos, sin = tabs
            qa, qb = cos * gains[0:1], sin * gains[1:2]
            ka, kb = cos * gains[2:3], sin * gains[3:4]
        pe = pe_scr[rows, :HEAD_PAD]
        pe_sw = pe_scr[rows, HEAD_PAD:]
        for hd in range(MLA_HEADS):
            sl = slice(hd * HEAD_PAD, (hd + 1) * HEAD_PAD)
            sl2 = slice(ALL_HEADS + hd * HEAD_PAD, ALL_HEADS + (hd + 1) * HEAD_PAD)
            qh = q_scr[rows, sl]
            kh = kv_scr[rows, sl] + pe
            rq = _rms_scale(qh, QK_HEAD_DIM)
            rk = _rms_scale(kh, QK_HEAD_DIM)
            if rope:
                qo = (qh * qa + q_scr[rows, sl2] * qb) * rq
                ko = (kh * ka + pe_sw * kb) * rk
            else:
                qo = qh * gains[0:1] * rq
                ko = kh * gains[2:3] * rk
            q_out[0, hd, rows, :] = qo.astype(bf16)
            k_out[0, hd, rows, :] = ko.astype(bf16)
            v_out[0, hd, rows, :] = (kv_scr[rows, sl2] + ones_col[hd % 2]).astype(bf16)
        return carry

    lax.fori_loop(0, tm // rb, block, 0)


def _qkv(x, mods, layer, mod_row, n1g, w, rope_tabs, tm):
    b, t, d = x.shape
    hh, hp = MLA_HEADS, HEAD_PAD
    rope = rope_tabs is not None
    in_specs = [
        pl.BlockSpec((1, tm, d), lambda bi, i: (bi, i, 0)),
        _mod_spec(layer, mod_row),
        _const_spec((1, d)),
        _const_spec(w["w_in"].shape),
        _const_spec((1, Q_LORA_RANK)),
        _const_spec((1, KV_LORA_RANK)),
        _const_spec(w["w_uq"].shape),
        _const_spec(w["w_ukv"].shape),
        _const_spec((8, hp)),
    ]
    args = [x, mods, n1g, w["w_in"], w["q_norm_g"], w["kv_norm_g"], w["w_uq"], w["w_ukv"], w["gains"]]
    if rope:
        rrow, rcol = rope_tabs
        in_specs += [_const_spec(rrow.shape), _const_spec(rcol.shape)]
        args += [rrow, rcol]
    head_spec = pl.BlockSpec((1, hh, tm, hp), lambda bi, i: (bi, 0, i, 0))
    head_shape = jax.ShapeDtypeStruct((b, hh, t, hp), bf16)
    return pl.pallas_call(
        functools.partial(_qkv_body, rope=rope),
        grid=(b, t // tm),
        in_specs=in_specs,
        out_specs=[head_spec, head_spec, head_spec,
                   pl.BlockSpec((1, tm, 3 * CONV_DIM), lambda bi, i: (bi, i, 0))],
        out_shape=[head_shape, head_shape, head_shape,
                   jax.ShapeDtypeStruct((b, t, 3 * CONV_DIM), bf16)],
        scratch_shapes=[pltpu.VMEM((tm, 2 * ALL_HEADS), f32), pltpu.VMEM((tm, 2 * ALL_HEADS), f32),
                        pltpu.VMEM((tm, 2 * hp), f32)],
        compiler_params=_cparams(("arbitrary", "arbitrary")),
        name="qkv_rope" if rope else "qkv_ctx",
    )(*args)


def _denominator_lane(parity):
    return V_HEAD_DIM if parity == 0 else 0


def _attn_head(q, k_refs, v_refs, par, chunks, online):
    m = None
    acc = None
    for src, (n_chunks, tk) in enumerate(chunks):
        for c in range(n_chunks):
            ks = slice(c * tk, (c + 1) * tk)
            s = lax.dot_general(q, k_refs[src][0, par, ks, :], NT_DIMS, preferred_element_type=f32)
            if online:
                m_cur = jnp.max(s, axis=1, keepdims=True)
                m_new = m_cur if m is None else jnp.maximum(m, m_cur)
                pv = _dot(jnp.exp2(s - m_new).astype(bf16), v_refs[src][0, par, ks, :])
                acc = pv if m is None else jnp.exp2(m - m_new) * acc + pv
                m = m_new
            else:
                pv = _dot(jnp.exp2(s).astype(bf16), v_refs[src][0, par, ks, :])
                acc = pv if acc is None else acc + pv
    lane = _denominator_lane(par)
    return acc / acc[:, lane:lane + 1]


def _attn_body(q_ref, *refs, chunks, online):
    o_ref = refs[-1]
    k_refs, v_refs = refs[0:-1:2], refs[1:-1:2]
    even = _attn_head(q_ref[0, 0], k_refs, v_refs, 0, chunks, online)
    odd = _attn_head(q_ref[0, 1], k_refs, v_refs, 1, chunks, online)
    lane = lax.broadcasted_iota(jnp.int32, even.shape, 1)
    o_ref[0] = jnp.where(lane < V_HEAD_DIM, even, odd).astype(bf16)


def _attention(q, kv_sources, tq, tk, online=True):
    b, hh, t, hp = q.shape
    pair_spec = lambda n: pl.BlockSpec((1, 2, n, hp), lambda bi, pi, i: (bi, pi, 0, 0))
    in_specs = [pl.BlockSpec((1, 2, tq, hp), lambda bi, pi, i: (bi, pi, i, 0))]
    args = [q]
    chunks = []
    for k, v in kv_sources:
        n = k.shape[2]
        step = min(tk, n)
        chunks.append((n // step, step))
        in_specs += [pair_spec(n), pair_spec(n)]
        args += [k, v]
    return pl.pallas_call(
        functools.partial(_attn_body, chunks=tuple(chunks), online=online),
        grid=(b, hh // 2, t // tq),
        in_specs=in_specs,
        out_specs=pl.BlockSpec((1, tq, hp), lambda bi, pi, i: (bi, i, pi)),
        out_shape=jax.ShapeDtypeStruct((b, t, hh * V_HEAD_DIM), bf16),
        compiler_params=_cparams(("arbitrary", "arbitrary", "arbitrary")),
        name="attn%d%s" % (len(kv_sources), "" if online else "_bounded"),
    )(*args)


def _gated_ffn(x, out, mod, n2g_ref, w1_ref, w3_ref, w2_ref):
    d = D_MODEL
    x1 = x + mod[:, 2 * d:3 * d] * out
    h2 = _modulated_norm(x1, mod, n2g_ref, 3 * d).astype(bf16)
    y = None
    for lo, hi in _ff_chunks(w1_ref.shape[2]):
        sl = slice(lo, hi)
        u = _dot(h2, w1_ref[0, :, sl])
        g = _dot(h2, w3_ref[0, :, sl])
        act = (u / (1.0 + jnp.exp(-u)) * g).astype(bf16)
        part = _dot(act, w2_ref[0, sl, :])
        y = part if y is None else y + part
    return x1 + mod[:, 5 * d:6 * d] * y


def _ff_chunks(d_ff):
    cut = min(d_ff, -(-d_ff // (2 * MXU_TILE)) * MXU_TILE)
    return [(0, cut), (cut, d_ff)] if cut < d_ff else [(0, d_ff)]


def _mix_ffn_body(*refs, halo):
    x_ref, a_ref, pc_ref = refs[:3]
    if halo:
        pprev_ref, pnext_ref = refs[3:5]
        refs = refs[5:]
    else:
        refs = refs[3:]
    (mod_ref, convw_ref, woa_ref, woc_ref, n2g_ref, w1_ref, w3_ref, w2_ref, o_ref) = refs
    x = x_ref[0]
    mod = mod_ref[0, 0]
    tm = x.shape[0]

    out = _dot(a_ref[0], woa_ref[...])

    cd = CONV_DIM
    pc = pc_ref[0].astype(f32)
    z = pc[:, cd:2 * cd] * pc[:, 2 * cd:]
    row = lax.broadcasted_iota(jnp.int32, (tm, cd), 0)
    if halo:
        i = pl.program_id(1)
        last = pl.num_programs(1) - 1
        pp = pprev_ref[0, BF16_SUBLANES - 1:BF16_SUBLANES, :].astype(f32)
        pn = pnext_ref[0, 0:1, :].astype(f32)
        z_prev = jnp.where(i > 0, pp[:, cd:2 * cd] * pp[:, 2 * cd:], 0.0)
        z_next = jnp.where(i < last, pn[:, cd:2 * cd] * pn[:, 2 * cd:], 0.0)
    else:
        z_prev = jnp.zeros((1, cd), f32)
        z_next = jnp.zeros((1, cd), f32)
    z_up = jnp.where(row == 0, z_prev, pltpu.roll(z, 1, axis=0))
    z_dn = jnp.where(row == tm - 1, z_next, pltpu.roll(z, tm - 1, axis=0))
    cw = convw_ref[0]
    y = z_up * cw[0:1, :] + z * cw[1:2, :] + z_dn * cw[2:3, :]
    out = out + _dot((pc[:, :cd] * y).astype(bf16), woc_ref[...])

    o_ref[0] = _gated_ffn(x, out, mod, n2g_ref, w1_ref, w3_ref, w2_ref)


def _mix_ffn(x, a, pc, mods, layer, mod_row, conv_w, j, w, n2g, w1, w3, w2, tm):
    b, t, d = x.shape
    hh, hp = MLA_HEADS, HEAD_PAD
    halo = t > tm
    in_specs = [
        pl.BlockSpec((1, tm, d), lambda bi, i: (bi, i, 0)),
        pl.BlockSpec((1, tm, hh * V_HEAD_DIM), lambda bi, i: (bi, i, 0)),
        pl.BlockSpec((1, tm, 3 * CONV_DIM), lambda bi, i: (bi, i, 0)),
    ]
    args = [x, a, pc]
    if halo:
        per = tm // BF16_SUBLANES
        nblk = t // BF16_SUBLANES
        in_specs += [
            pl.BlockSpec((1, BF16_SUBLANES, 3 * CONV_DIM),
                         lambda bi, i: (bi, jnp.maximum(i * per - 1, 0), 0)),
            pl.BlockSpec((1, BF16_SUBLANES, 3 * CONV_DIM),
                         lambda bi, i: (bi, jnp.minimum((i + 1) * per, nblk - 1), 0)),
        ]
        args += [pc, pc]
    in_specs += [
        _mod_spec(layer, mod_row),
        _const_spec(conv_w.shape, j),
        _const_spec(w["w_o_attn"].shape),
        _const_spec(w["w_o_conv"].shape),
        _const_spec((1, d)),
        _const_spec(w1.shape, layer),
        _const_spec(w3.shape, layer),
        _const_spec(w2.shape, layer),
    ]
    args += [mods, conv_w, w["w_o_attn"], w["w_o_conv"], n2g, w1, w3, w2]
    return pl.pallas_call(
        functools.partial(_mix_ffn_body, halo=halo),
        grid=(b, t // tm),
        in_specs=in_specs,
        out_specs=pl.BlockSpec((1, tm, d), lambda bi, i: (bi, i, 0)),
        out_shape=jax.ShapeDtypeStruct((b, t, d), f32),
        compiler_params=_cparams(("arbitrary", "arbitrary")),
        name="mix_ffn_halo" if halo else "mix_ffn",
    )(*args)


def _dft_a_body(x_ref, mod_ref, n1g_ref, cs_ref, twc_ref, tws_ref, o_ref, *, k):
    d = D_MODEL
    mod = mod_ref[0, 0]
    cs = cs_ref[...]
    reps = d // LANES
    for j in range(k):
        x = x_ref[0, :, j * d:(j + 1) * d]
        h = _modulated_norm(x, mod, n1g_ref, 0)
        a = _dot(cs, h.astype(bf16))
        a_re, a_im = a[:DFT_T1], a[DFT_T1:]
        c = jnp.tile(twc_ref[:, j * LANES:(j + 1) * LANES], (1, reps))
        s = jnp.tile(tws_ref[:, j * LANES:(j + 1) * LANES], (1, reps))
        o_ref[0, 0, :, j * d:(j + 1) * d] = (a_re * c + a_im * s).astype(bf16)
        o_ref[0, 1, :, j * d:(j + 1) * d] = (a_im * c - a_re * s).astype(bf16)


def _dft_a(x, mods, layer, n1g, tabs, k):
    b, t, d = x.shape
    t2 = t // DFT_T1
    xv = x.reshape(b, DFT_T1, t2 * d)
    return pl.pallas_call(
        functools.partial(_dft_a_body, k=k),
        grid=(b, t2 // k),
        in_specs=[
            pl.BlockSpec((1, DFT_T1, k * d), lambda bi, j: (bi, 0, j)),
            _mod_spec(layer, lambda bi: bi),
            _const_spec((1, d)),
            _const_spec((2 * DFT_T1, DFT_T1)),
            pl.BlockSpec((DFT_T1, k * LANES), lambda bi, j: (0, j)),
            pl.BlockSpec((DFT_T1, k * LANES), lambda bi, j: (0, j)),
        ],
        out_specs=pl.BlockSpec((1, 2, DFT_T1, k * d), lambda bi, j: (bi, 0, 0, j)),
        out_shape=jax.ShapeDtypeStruct((b, 2, DFT_T1, t2 * d), bf16),
        compiler_params=_cparams(("arbitrary", "arbitrary")),
        name="dft_a",
    )(xv, mods, n1g, tabs["cs64"], tabs["tw_cos"], tabs["tw_sin"])


def _dft_ffn_body(x_ref, a_ref, mod_ref, m_ref, cc_ref, sc_ref, wf_ref, n2g_ref,
                  w1_ref, w3_ref, w2_ref, o_ref, *, n1):
    d = D_MODEL
    gd = FOURIER_GROUP_DIM
    t2 = a_ref.shape[3]
    mod = mod_ref[0, 0]
    mm = m_ref[...]
    fs = []
    for i in range(n1):
        bc = _dot(mm[:, :t2], a_ref[0, 0, i]) + _dot(mm[:, t2:], a_ref[0, 1, i])
        b_re = bc[:t2].astype(bf16)
        b_im = bc[t2:].astype(bf16)
        cols = []
        for g in range(FOURIER_GROUPS):
            sl = slice(g * gd, (g + 1) * gd)
            cols.append(_dot(b_re[:, sl], cc_ref[...]) + _dot(b_im[:, sl], sc_ref[...]))
        fs.append(jnp.concatenate(cols, axis=1))
    f = jnp.concatenate(fs, axis=0).astype(bf16)
    out = _dot(f, wf_ref[0])
    x = jnp.concatenate([x_ref[0, :, i * d:(i + 1) * d] for i in range(n1)], axis=0)
    res = _gated_ffn(x, out, mod, n2g_ref, w1_ref, w3_ref, w2_ref)
    for i in range(n1):
        o_ref[0, :, i * d:(i + 1) * d] = res[i * t2:(i + 1) * t2]


def _dft_ffn(x, a, mods, layer, tabs, wf, j, n2g, w1, w3, w2, n1):
    b, t, d = x.shape
    t2 = t // DFT_T1
    xv = x.reshape(b, t2, DFT_T1 * d)
    av = a.reshape(b, 2, DFT_T1, t2, d)
    out = pl.pallas_call(
        functools.partial(_dft_ffn_body, n1=n1),
        grid=(b, DFT_T1 // n1),
        in_specs=[
            pl.BlockSpec((1, t2, n1 * d), lambda bi, jj: (bi, 0, jj)),
            pl.BlockSpec((1, 2, n1, t2, d), lambda bi, jj: (bi, 0, jj, 0, 0)),
            _mod_spec(layer, lambda bi: bi),
            _const_spec((2 * t2, 2 * t2)),
            _const_spec((FOURIER_GROUP_DIM, FOURIER_GROUP_DIM)),
            _const_spec((FOURIER_GROUP_DIM, FOURIER_GROUP_DIM)),
            _const_spec(wf.shape, j),
            _const_spec((1, d)),
            _const_spec(w1.shape, layer),
            _const_spec(w3.shape, layer),
            _const_spec(w2.shape, layer),
        ],
        out_specs=pl.BlockSpec((1, t2, n1 * d), lambda bi, jj: (bi, 0, jj)),
        out_shape=jax.ShapeDtypeStruct((b, t2, DFT_T1 * d), f32),
        compiler_params=_cparams(("arbitrary", "arbitrary")),
        name="dft_ffn",
    )(xv, av, mods, tabs["m2"], tabs["cc"], tabs["sc"], wf, n2g, w1, w3, w2)
    return out.reshape(b, t, d)


def _angle(i, j, n):
    return (2.0 * math.pi / n) * ((i * j) % n).astype(f32)


def _dft_tables(t):
    t1, t2 = DFT_T1, t // DFT_T1
    i1 = jnp.arange(t1, dtype=jnp.int32)
    ang64 = _angle(i1[:, None], i1[None, :], t1)
    s1 = t1 ** -0.5
    cs64 = jnp.concatenate([jnp.cos(ang64) * s1, -jnp.sin(ang64) * s1], axis=0).astype(bf16)
    i2 = jnp.arange(t2, dtype=jnp.int32)
    angt = _angle(i1[:, None], i2[None, :], t)
    tw_cos = jnp.repeat(jnp.cos(angt), LANES, axis=1)
    tw_sin = jnp.repeat(jnp.sin(angt), LANES, axis=1)
    ang2 = _angle(i2[:, None], i2[None, :], t2)
    s2 = t2 ** -0.5
    c2, sn2 = jnp.cos(ang2) * s2, jnp.sin(ang2) * s2
    m2 = jnp.concatenate([jnp.concatenate([c2, sn2], axis=1),
                          jnp.concatenate([-sn2, c2], axis=1)], axis=0).astype(bf16)
    ic = jnp.arange(FOURIER_GROUP_DIM, dtype=jnp.int32)
    angc = _angle(ic[:, None], ic[None, :], FOURIER_GROUP_DIM)
    sc_ = FOURIER_GROUP_DIM ** -0.5
    return dict(cs64=cs64, tw_cos=tw_cos, tw_sin=tw_sin, m2=m2,
                cc=(jnp.cos(angc) * sc_).astype(bf16), sc=(jnp.sin(angc) * sc_).astype(bf16))


def _rope_tables(t):
    rows = t // GRID_W
    half = QK_ROPE_DIM // 2
    inv = 1.0 / (ROPE_BASE ** (jnp.arange(0, half, 2, dtype=f32) / half))
    ar = jnp.arange(rows, dtype=f32)[:, None] * inv
    ac = jnp.arange(GRID_W, dtype=f32)[:, None] * inv
    pad_hi = HEAD_PAD - QK_HEAD_DIM

    def place(n, first, second, at, lead):
        z = jnp.zeros((n, half), f32)
        blocks = [jnp.full((n, QK_NOPE_DIM), lead, f32)]
        blocks += [jnp.concatenate([first, second], axis=1), z] if at == 0 else \
                  [z, jnp.concatenate([first, second], axis=1)]
        blocks.append(jnp.zeros((n, pad_hi), f32))
        return jnp.concatenate(blocks, axis=1)

    rrow = jnp.stack([place(rows, jnp.cos(ar), jnp.cos(ar), 0, 0.0),
                      place(rows, -jnp.sin(ar), jnp.sin(ar), 0, 0.0)])
    rcol = jnp.stack([place(GRID_W, jnp.cos(ac), jnp.cos(ac), 1, 1.0),
                      place(GRID_W, -jnp.sin(ac), jnp.sin(ac), 1, 0.0)])
    return rrow, rcol


def _partner(a, axis):
    parts = jnp.split(a, 4, axis=axis)
    return jnp.concatenate([parts[1], parts[0], parts[3], parts[2]], axis=axis)


def _prep_even(j, w_in, q_norm_g, kv_norm_g, w_uq, w_ukv, q_gain, k_gain, w_o):
    hh, hp = MLA_HEADS, HEAD_PAD
    d = D_MODEL
    nope, hd = QK_NOPE_DIM, QK_HEAD_DIM
    wi = w_in[j]
    w_pe = wi[:, PE_COL:CONV_OFFSET]
    place = lambda wcols: jnp.zeros((d, hp), f32).at[:, nope:hd].set(wcols)
    w_in_p = jnp.concatenate([wi[:, :PE_COL], place(w_pe), place(_partner(w_pe, 1)),
                              wi[:, CONV_OFFSET:]], axis=1).astype(bf16)
    wq = w_uq[j].reshape(Q_LORA_RANK, hh, hd)
    pad_head = lambda a: jnp.pad(a, ((0, 0), (0, 0), (0, hp - a.shape[2]))).reshape(a.shape[0], hh * hp)
    wq_sw = jnp.zeros_like(wq).at[:, :, nope:].set(_partner(wq[:, :, nope:], 2))
    w_uq_p = jnp.concatenate([pad_head(wq), pad_head(wq_sw)], axis=1).astype(bf16)
    wkv = w_ukv[j].reshape(KV_LORA_RANK, hh, nope + V_HEAD_DIM)
    wv = wkv[:, :, nope:].reshape(KV_LORA_RANK, hh // 2, 2, V_HEAD_DIM)
    zv = jnp.zeros_like(wv[:, :, 0])
    wv = jnp.stack([jnp.concatenate([wv[:, :, 0], zv], axis=-1),
                    jnp.concatenate([zv, wv[:, :, 1]], axis=-1)], axis=2).reshape(KV_LORA_RANK, hh * hp)
    w_ukv_p = jnp.concatenate([pad_head(wkv[:, :, :nope]), wv], axis=1).astype(bf16)

    def gain_rows(g):
        sw = jnp.zeros_like(g).at[nope:].set(_partner(g[nope:], 0))
        return [jnp.pad(g, (0, hp - hd)), jnp.pad(sw, (0, hp - hd))]

    zero = jnp.zeros((hp,), f32)
    gains = jnp.stack(gain_rows(q_gain[j] * (QK_SCALE * LOG2E)) + gain_rows(k_gain[j]) + [zero] * 4)
    return dict(
        w_in=w_in_p, w_uq=w_uq_p, w_ukv=w_ukv_p,
        q_norm_g=q_norm_g[j].reshape(1, -1), kv_norm_g=kv_norm_g[j].reshape(1, -1), gains=gains,
        w_o_attn=w_o[j][:hh * V_HEAD_DIM].astype(bf16), w_o_conv=w_o[j][hh * V_HEAD_DIM:].astype(bf16))


def _pick(n, pref):
    return pref if n % pref == 0 else n


def kernel(x, c, ctx, c_ctx, ada_w, ada_b, norm1_g, norm2_g, w_in, q_norm_g, kv_norm_g, w_uq, w_ukv,
           q_gain, k_gain, conv_w, w_o, w_fourier, ffn_w1, ffn_w3, ffn_w2):
    b, s, d = x.shape
    depth = ada_w.shape[0]
    cvec = jnp.zeros((8, d), f32).at[:b].set(c).at[b].set(c_ctx)
    mods = _ada(cvec, ada_w, ada_b).reshape(depth, 8, 1, 6 * d)
    lat_row = lambda bi: bi
    ctx_row = lambda bi: b
    w1 = ffn_w1.astype(bf16)
    w3 = ffn_w3.astype(bf16)
    w2 = ffn_w2.astype(bf16)
    wf = w_fourier.astype(bf16)
    for i in range(depth):
        last = i == depth - 1
        j = i // 2
        n1g = norm1_g[i].reshape(1, d)
        n2g = norm2_g[i].reshape(1, d)
        if i % 2 == 0:
            w = _prep_even(j, w_in, q_norm_g, kv_norm_g, w_uq, w_ukv, q_gain, k_gain, w_o)
            tm = _pick(s, 512)
            lc = ctx.shape[1]
            q_l, k_l, v_l, pc_l = _qkv(x, mods, i, lat_row, n1g, w, _rope_tables(s), tm)
            q_c, k_c, v_c, pc_c = _qkv(ctx, mods, i, ctx_row, n1g, w, None, lc)
            bound = (QK_SCALE * LOG2E * QK_HEAD_DIM * BF16_ROUNDING_SLACK
                     * jnp.max(jnp.abs(q_gain[j])) * jnp.max(jnp.abs(k_gain[j])))
            tq, tk = _pick(s, 512), _pick(s, 1024)
            a_l = lax.cond(
                bound < EXP2_SAFE_SCORE,
                lambda *ops: _attention(ops[0], [ops[1:3], ops[3:5]], tq, _pick(s, 256), online=False),
                lambda *ops: _attention(ops[0], [ops[1:3], ops[3:5]], tq, tk, online=True),
                q_l, k_l, v_l, k_c, v_c)
            x_new = _mix_ffn(x, a_l, pc_l, mods, i, lat_row, conv_w, j, w, n2g, w1, w3, w2, tm)
            if not last:
                a_c = _attention(q_c, [(k_c, v_c)], lc, lc)
                ctx = _mix_ffn(ctx, a_c, pc_c, mods, i, ctx_row, conv_w, j, w, n2g, w1, w3, w2, lc)
            x = x_new
        else:
            tabs = _dft_tables(s)
            t2 = s // DFT_T1
            a = _dft_a(x, mods, i, n1g, tabs, _pick(t2, 16))
            x = _dft_ffn(x, a, mods, i, tabs, wf, j, n2g, w1, w3, w2, 4)
            assert last, "odd non-final layers are not implemented"
    return x
```

```python
import functools
import math

import jax
import jax.numpy as jnp
from jax import lax
from jax.experimental import pallas as pl
from jax.experimental.pallas import tpu as pltpu

D_MODEL = 1024
GRID_W = 64
MLA_HEADS = 8
QK_NOPE_DIM = 64
QK_ROPE_DIM = 32
QK_HEAD_DIM = QK_NOPE_DIM + QK_ROPE_DIM
V_HEAD_DIM = 64
Q_LORA_RANK = 384
KV_LORA_RANK = 256
QK_SCALE = QK_HEAD_DIM ** -0.5
ROPE_BASE = 10000.0
CONV_DIM = 512
CONV_OFFSET = Q_LORA_RANK + KV_LORA_RANK + QK_ROPE_DIM
FOURIER_GROUPS = 4
FOURIER_GROUP_DIM = D_MODEL // FOURIER_GROUPS
EPS = 1e-6

LANES = 128
BF16_SUBLANES = 16
MXU_TILE = 256
VMEM_LIMIT_BYTES = 56 * 1024 * 1024

HEAD_PAD = LANES
ALL_HEADS = MLA_HEADS * HEAD_PAD
PE_COL = Q_LORA_RANK + KV_LORA_RANK
PE_SW_COL = PE_COL + HEAD_PAD
CONV_COL = PE_SW_COL + HEAD_PAD
IN_PROJ_PAD = CONV_COL + 3 * CONV_DIM
QKV_ROW_BLOCK = 128
DFT_T1 = 64
LOG2E = math.log2(math.e)
EXP2_SAFE_SCORE = 64.0
BF16_ROUNDING_SLACK = 1.02
NT_DIMS = (((1,), (1,)), ((), ()))

bf16 = jnp.bfloat16
f32 = jnp.float32


def _cparams(sem):
    return pltpu.CompilerParams(dimension_semantics=sem, vmem_limit_bytes=VMEM_LIMIT_BYTES)


def _const_spec(shape, lead=None):
    if lead is None:
        nd = len(shape)
        return pl.BlockSpec(shape, lambda *_: (0,) * nd, pipeline_mode=pl.Buffered(1))
    nd = len(shape) - 1
    return pl.BlockSpec((1,) + tuple(shape[1:]), lambda *_: (lead,) + (0,) * nd,
                        pipeline_mode=pl.Buffered(1))


def _mod_spec(layer, mod_row):
    return pl.BlockSpec((1, 1, 1, 6 * D_MODEL), lambda bi, i: (layer, mod_row(bi), 0, 0))


def _dot(a, b):
    return jnp.dot(a, b, preferred_element_type=f32)


def _rms_scale(x, width):
    return lax.rsqrt(jnp.sum(x * x, axis=-1, keepdims=True) * (1.0 / width) + EPS)


def _ada_body(c_ref, w_ref, b_ref, o_ref):
    c = c_ref[...]
    s = c / (1.0 + jnp.exp(-c))
    o_ref[0] = _dot(s.astype(bf16), w_ref[0].astype(bf16)) + b_ref[0]


def _ada(cvec, ada_w, ada_b):
    depth, d, n = ada_w.shape
    tn = 1536
    return pl.pallas_call(
        _ada_body,
        grid=(depth, n // tn),
        in_specs=[
            pl.BlockSpec((8, d), lambda l, j: (0, 0)),
            pl.BlockSpec((1, d, tn), lambda l, j: (l, 0, j)),
            pl.BlockSpec((1, 1, tn), lambda l, j: (l, 0, j)),
        ],
        out_specs=pl.BlockSpec((1, 8, tn), lambda l, j: (l, 0, j)),
        out_shape=jax.ShapeDtypeStruct((depth, 8, n), f32),
        compiler_params=_cparams(("arbitrary", "arbitrary")),
        name="ada",
    )(cvec, ada_w, ada_b.reshape(depth, 1, n))


def _modulated_norm(x, mod, g_ref, lo):
    d = D_MODEL
    shift = mod[:, lo:lo + d]
    scale = mod[:, lo + d:lo + 2 * d]
    gain = g_ref[...] * (1.0 + scale)
    return (x * _rms_scale(x, d)) * gain + shift


def _qkv_body(*refs, rope):
    (x_ref, mod_ref, n1g_ref, win_ref, qng_ref, kvng_ref, wuq_ref, wukv_ref, gains_ref) = refs[:9]
    if rope:
        rrow_ref, rcol_ref = refs[9:11]
        refs = refs[11:]
    else:
        refs = refs[9:]
    q_out, k_out, v_out, pc_out, q_scr, kv_scr, pe_scr = refs
    x = x_ref[0]
    tm = x.shape[0]
    h = _modulated_norm(x, mod_ref[0, 0], n1g_ref, 0)
    p = _dot(h.astype(bf16), win_ref[...])
    pc_out[0] = p[:, CONV_COL:].astype(bf16)
    pe_scr[...] = p[:, PE_COL:CONV_COL]
    cq = p[:, :Q_LORA_RANK]
    cqn = cq * _rms_scale(cq, Q_LORA_RANK) * qng_ref[...]
    q_scr[...] = _dot(cqn.astype(bf16), wuq_ref[...])
    ckv = p[:, Q_LORA_RANK:PE_COL]
    ckvn = ckv * _rms_scale(ckv, KV_LORA_RANK) * kvng_ref[...]
    kv_scr[...] = _dot(ckvn.astype(bf16), wukv_ref[...])

    gains = gains_ref[...]
    lane = lax.broadcasted_iota(jnp.int32, (1, HEAD_PAD), 1)
    ones_col = [(lane == _denominator_lane(par)).astype(f32) for par in range(2)]
    rb = min(QKV_ROW_BLOCK, tm)
    per = rb // GRID_W
    grid_row0 = pl.program_id(1) * (tm // GRID_W)

    def block(ib, carry):
        r0 = pl.multiple_of(ib * rb, rb)
        rows = pl.ds(r0, rb)
        if rope:
            tabs = []
            for comp in range(2):
                parts = []
                for g in range(per):
                    rr = rrow_ref[comp, pl.ds(grid_row0 + ib * per + g, 1), :]
                    parts.append(rcol_ref[comp] + rr)
                tabs.append(jnp.concatenate(parts, axis=0))
            cos, sin = tabs
            qa, qb = cos * gains[0:1], sin * gains[1:2]
            ka, kb = cos * gains[2:3], sin * gains[3:4]
        pe = pe_scr[rows, :HEAD_PAD]
        pe_sw = pe_scr[rows, HEAD_PAD:]
        for hd in range(MLA_HEADS):
            sl = slice(hd * HEAD_PAD, (hd + 1) * HEAD_PAD)
            sl2 = slice(ALL_HEADS + hd * HEAD_PAD, ALL_HEADS + (hd + 1) * HEAD_PAD)
            qh = q_scr[rows, sl]
            kh = kv_scr[rows, sl] + pe
            rq = _rms_scale(qh, QK_HEAD_DIM)
            rk = _rms_scale(kh, QK_HEAD_DIM)
            if rope:
                qo = (qh * qa + q_scr[rows, sl2] * qb) * rq
                ko = (kh * ka + pe_sw * kb) * rk
            else:
                qo = qh * gains[0:1] * rq
                ko = kh * gains[2:3] * rk
            q_out[0, hd, rows, :] = qo.astype(bf16)
            k_out[0, hd, rows, :] = ko.astype(bf16)
            v_out[0, hd, rows, :] = (kv_scr[rows, sl2] + ones_col[hd % 2]).astype(bf16)
        return carry

    lax.fori_loop(0, tm // rb, block, 0)


def _qkv(x, mods, layer, mod_row, n1g, w, rope_tabs, tm):
    b, t, d = x.shape
    hh, hp = MLA_HEADS, HEAD_PAD
    rope = rope_tabs is not None
    in_specs = [
        pl.BlockSpec((1, tm, d), lambda bi, i: (bi, i, 0)),
        _mod_spec(layer, mod_row),
        _const_spec((1, d)),
        _const_spec(w["w_in"].shape),
        _const_spec((1, Q_LORA_RANK)),
        _const_spec((1, KV_LORA_RANK)),
        _const_spec(w["w_uq"].shape),
        _const_spec(w["w_ukv"].shape),
        _const_spec((8, hp)),
    ]
    args = [x, mods, n1g, w["w_in"], w["q_norm_g"], w["kv_norm_g"], w["w_uq"], w["w_ukv"], w["gains"]]
    if rope:
        rrow, rcol = rope_tabs
        in_specs += [_const_spec(rrow.shape), _const_spec(rcol.shape)]
        args += [rrow, rcol]
    head_spec = pl.BlockSpec((1, hh, tm, hp), lambda bi, i: (bi, 0, i, 0))
    head_shape = jax.ShapeDtypeStruct((b, hh, t, hp), bf16)
    return pl.pallas_call(
        functools.partial(_qkv_body, rope=rope),
        grid=(b, t // tm),
        in_specs=in_specs,
        out_specs=[head_spec, head_spec, head_spec,
                   pl.BlockSpec((1, tm, 3 * CONV_DIM), lambda bi, i: (bi, i, 0))],
        out_shape=[head_shape, head_shape, head_shape,
                   jax.ShapeDtypeStruct((b, t, 3 * CONV_DIM), bf16)],
        scratch_shapes=[pltpu.VMEM((tm, 2 * ALL_HEADS), f32), pltpu.VMEM((tm, 2 * ALL_HEADS), f32),
                        pltpu.VMEM((tm, 2 * hp), f32)],
        compiler_params=_cparams(("arbitrary", "arbitrary")),
        name="qkv_rope" if rope else "qkv_ctx",
    )(*args)


def _denominator_lane(parity):
    return V_HEAD_DIM if parity == 0 else 0


def _attn_head(q, k_refs, v_refs, par, chunks, online):
    m = None
    acc = None
    for src, (n_chunks, tk) in enumerate(chunks):
        for c in range(n_chunks):
            ks = slice(c * tk, (c + 1) * tk)
            s = lax.dot_general(q, k_refs[src][0, par, ks, :], NT_DIMS, preferred_element_type=f32)
            if online:
                m_cur = jnp.max(s, axis=1, keepdims=True)
                m_new = m_cur if m is None else jnp.maximum(m, m_cur)
                pv = _dot(jnp.exp2(s - m_new).astype(bf16), v_refs[src][0, par, ks, :])
                acc = pv if m is None else jnp.exp2(m - m_new) * acc + pv
                m = m_new
            else:
                pv = _dot(jnp.exp2(s).astype(bf16), v_refs[src][0, par, ks, :])
                acc = pv if acc is None else acc + pv
    lane = _denominator_lane(par)
    return acc / acc[:, lane:lane + 1]


def _attn_body(q_ref, *refs, chunks, online):
    o_ref = refs[-1]
    k_refs, v_refs = refs[0:-1:2], refs[1:-1:2]
    even = _attn_head(q_ref[0, 0], k_refs, v_refs, 0, chunks, online)
    odd = _attn_head(q_ref[0, 1], k_refs, v_refs, 1, chunks, online)
    lane = lax.broadcasted_iota(jnp.int32, even.shape, 1)
    o_ref[0] = jnp.where(lane < V_HEAD_DIM, even, odd).astype(bf16)


def _attention(q, kv_sources, tq, tk, online=True):
    b, hh, t, hp = q.shape
    pair_spec = lambda n: pl.BlockSpec((1, 2, n, hp), lambda bi, pi, i: (bi, pi, 0, 0))
    in_specs = [pl.BlockSpec((1, 2, tq, hp), lambda bi, pi, i: (bi, pi, i, 0))]
    args = [q]
    chunks = []
    for k, v in kv_sources:
        n = k.shape[2]
        step = min(tk, n)
        chunks.append((n // step, step))
        in_specs += [pair_spec(n), pair_spec(n)]
        args += [k, v]
    return pl.pallas_call(
        functools.partial(_attn_body, chunks=tuple(chunks), online=online),
        grid=(b, hh // 2, t // tq),
        in_specs=in_specs,
        out_specs=pl.BlockSpec((1, tq, hp), lambda bi, pi, i: (bi, i, pi)),
        out_shape=jax.ShapeDtypeStruct((b, t, hh * V_HEAD_DIM), bf16),
        compiler_params=_cparams(("arbitrary", "arbitrary", "arbitrary")),
        name="attn%d%s" % (len(kv_sources), "" if online else "_bounded"),
    )(*args)


def _gated_ffn(x, out, mod, n2g_ref, w1_ref, w3_ref, w2_ref):
    d = D_MODEL
    x1 = x + mod[:, 2 * d:3 * d] * out
    h2 = _modulated_norm(x1, mod, n2g_ref, 3 * d).astype(bf16)
    y = None
    for lo, hi in _ff_chunks(w1_ref.shape[2]):
        sl = slice(lo, hi)
        u = _dot(h2, w1_ref[0, :, sl])
        g = _dot(h2, w3_ref[0, :, sl])
        act = (u / (1.0 + jnp.exp(-u)) * g).astype(bf16)
        part = _dot(act, w2_ref[0, sl, :])
        y = part if y is None else y + part
    return x1 + mod[:, 5 * d:6 * d] * y


def _ff_chunks(d_ff):
    cut = min(d_ff, -(-d_ff // (2 * MXU_TILE)) * MXU_TILE)
    return [(0, cut), (cut, d_ff)] if cut < d_ff else [(0, d_ff)]


def _mix_ffn_body(*refs, halo):
    x_ref, a_ref, pc_ref = refs[:3]
    if halo:
        pprev_ref, pnext_ref = refs[3:5]
        refs = refs[5:]
    else:
        refs = refs[3:]
    (mod_ref, convw_ref, woa_ref, woc_ref, n2g_ref, w1_ref, w3_ref, w2_ref, o_ref) = refs
    x = x_ref[0]
    mod = mod_ref[0, 0]
    tm = x.shape[0]

    out = _dot(a_ref[0], woa_ref[...])

    cd = CONV_DIM
    pc = pc_ref[0].astype(f32)
    z = pc[:, cd:2 * cd] * pc[:, 2 * cd:]
    row = lax.broadcasted_iota(jnp.int32, (tm, cd), 0)
    if halo:
        i = pl.program_id(1)
        last = pl.num_programs(1) - 1
        pp = pprev_ref[0, BF16_SUBLANES - 1:BF16_SUBLANES, :].astype(f32)
        pn = pnext_ref[0, 0:1, :].astype(f32)
        z_prev = jnp.where(i > 0, pp[:, cd:2 * cd] * pp[:, 2 * cd:], 0.0)
        z_next = jnp.where(i < last, pn[:, cd:2 * cd] * pn[:, 2 * cd:], 0.0)
    else:
        z_prev = jnp.zeros((1, cd), f32)
        z_next = jnp.zeros((1, cd), f32)
    z_up = jnp.where(row == 0, z_prev, pltpu.roll(z, 1, axis=0))
    z_dn = jnp.where(row == tm - 1, z_next, pltpu.roll(z, tm - 1, axis=0))
    cw = convw_ref[0]
    y = z_up * cw[0:1, :] + z * cw[1:2, :] + z_dn * cw[2:3, :]
    out = out + _dot((pc[:, :cd] * y).astype(bf16), woc_ref[...])

    o_ref[0] = _gated_ffn(x, out, mod, n2g_ref, w1_ref, w3_ref, w2_ref)


def _mix_ffn(x, a, pc, mods, layer, mod_row, conv_w, j, w, n2g, w1, w3, w2, tm):
    b, t, d = x.shape
    hh, hp = MLA_HEADS, HEAD_PAD
    halo = t > tm
    in_specs = [
        pl.BlockSpec((1, tm, d), lambda bi, i: (bi, i, 0)),
        pl.BlockSpec((1, tm, hh * V_HEAD_DIM), lambda bi, i: (bi, i, 0)),
        pl.BlockSpec((1, tm, 3 * CONV_DIM), lambda bi, i: (bi, i, 0)),
    ]
    args = [x, a, pc]
    if halo:
        per = tm // BF16_SUBLANES
        nblk = t // BF16_SUBLANES
        in_specs += [
            pl.BlockSpec((1, BF16_SUBLANES, 3 * CONV_DIM),
                         lambda bi, i: (bi, jnp.maximum(i * per - 1, 0), 0)),
            pl.BlockSpec((1, BF16_SUBLANES, 3 * CONV_DIM),
                         lambda bi, i: (bi, jnp.minimum((i + 1) * per, nblk - 1), 0)),
        ]
        args += [pc, pc]
    in_specs += [
        _mod_spec(layer, mod_row),
        _const_spec(conv_w.shape, j),
        _const_spec(w["w_o_attn"].shape),
        _const_spec(w["w_o_conv"].shape),
        _const_spec((1, d)),
        _const_spec(w1.shape, layer),
        _const_spec(w3.shape, layer),
        _const_spec(w2.shape, layer),
    ]
    args += [mods, conv_w, w["w_o_attn"], w["w_o_conv"], n2g, w1, w3, w2]
    return pl.pallas_call(
        functools.partial(_mix_ffn_body, halo=halo),
        grid=(b, t // tm),
        in_specs=in_specs,
        out_specs=pl.BlockSpec((1, tm, d), lambda bi, i: (bi, i, 0)),
        out_shape=jax.ShapeDtypeStruct((b, t, d), f32),
        compiler_params=_cparams(("arbitrary", "arbitrary")),
        name="mix_ffn_halo" if halo else "mix_ffn",
    )(*args)


SUBLANES = 8


def _dft_a_body(x_ref, mod_ref, n1g_ref, kw_ref, cd_ref, twc_ref, tws_ref, o_ref):
    d = D_MODEL
    gd = FOURIER_GROUP_DIM
    rows = DFT_T1 * SUBLANES
    mod = mod_ref[0, 0]
    halves = [[], []]
    for half in range(x_ref.shape[2] // SUBLANES):
        sub = slice(half * SUBLANES, (half + 1) * SUBLANES)
        x = x_ref[0, :, sub, :].reshape(rows, d)
        h = _modulated_norm(x, mod, n1g_ref, 0).astype(bf16)
        a = _dot(kw_ref[...], h).astype(bf16)
        tw_c = twc_ref[half]
        tw_s = tws_ref[half]
        re_cols, im_cols = [], []
        for g in range(FOURIER_GROUPS):
            sl = slice(g * gd, (g + 1) * gd)
            z = _dot(jnp.concatenate([a[:rows, sl], a[rows:, sl]], axis=1), cd_ref[...])
            z_re, z_im = z[:, :gd], z[:, gd:]
            c = jnp.tile(tw_c, (1, gd // LANES))
            s = jnp.tile(tw_s, (1, gd // LANES))
            re_cols.append(z_re * c + z_im * s)
            im_cols.append(z_im * c - z_re * s)
        halves[0].append(jnp.concatenate(re_cols, axis=1).reshape(DFT_T1, SUBLANES, d))
        halves[1].append(jnp.concatenate(im_cols, axis=1).reshape(DFT_T1, SUBLANES, d))
    for comp in range(2):
        o_ref[0, comp] = jnp.concatenate(halves[comp], axis=1).astype(bf16)


def _dft_a(x, mods, layer, n1g, tabs):
    b, t, d = x.shape
    t2 = t // DFT_T1
    k = BF16_SUBLANES
    rows = DFT_T1 * SUBLANES
    return pl.pallas_call(
        _dft_a_body,
        grid=(b, t2 // k),
        in_specs=[
            pl.BlockSpec((1, DFT_T1, k, d), lambda bi, j: (bi, 0, j, 0)),
            _mod_spec(layer, lambda bi: bi),
            _const_spec((1, d)),
            _const_spec((2 * rows, rows)),
            _const_spec((2 * FOURIER_GROUP_DIM, 2 * FOURIER_GROUP_DIM)),
            pl.BlockSpec((k // SUBLANES, rows, LANES), lambda bi, j: (j, 0, 0)),
            pl.BlockSpec((k // SUBLANES, rows, LANES), lambda bi, j: (j, 0, 0)),
        ],
        out_specs=pl.BlockSpec((1, 2, DFT_T1, k, d), lambda bi, j: (bi, 0, 0, j, 0)),
        out_shape=jax.ShapeDtypeStruct((b, 2, DFT_T1, t2, d), bf16),
        compiler_params=_cparams(("arbitrary", "arbitrary")),
        name="dft_a",
    )(x.reshape(b, DFT_T1, t2, d), mods, n1g, tabs["kw"], tabs["cd"], tabs["tw_cos"], tabs["tw_sin"])


def _dft_ffn_body(x_ref, z_ref, mod_ref, g_ref, wf_ref, n2g_ref, w1_ref, w3_ref, w2_ref, o_ref):
    d = D_MODEL
    tp = x_ref.shape[1]
    rows = tp * SUBLANES
    z = z_ref[0].reshape(-1, d)
    r0 = pl.multiple_of(pl.program_id(2) * rows, rows)
    f = _dot(g_ref[pl.ds(r0, rows), :], z).astype(bf16)
    out = _dot(f, wf_ref[0])
    x = x_ref[0].reshape(rows, d)
    res = _gated_ffn(x, out, mod_ref[0, 0], n2g_ref, w1_ref, w3_ref, w2_ref)
    o_ref[0] = res.reshape(tp, SUBLANES, d)


def _dft_ffn(x, z, mods, layer, tabs, wf, j, n2g, w1, w3, w2, tp):
    b, t, d = x.shape
    t2 = t // DFT_T1
    out = pl.pallas_call(
        _dft_ffn_body,
        grid=(b, DFT_T1 // SUBLANES, t2 // tp),
        in_specs=[
            pl.BlockSpec((1, tp, SUBLANES, d), lambda bi, jj, hh: (bi, hh, jj, 0)),
            pl.BlockSpec((1, 2, SUBLANES, t2, d), lambda bi, jj, hh: (bi, 0, jj, 0, 0)),
            pl.BlockSpec((1, 1, 1, 6 * d), lambda bi, jj, hh: (layer, bi, 0, 0)),
            _const_spec((SUBLANES * t2, 2 * SUBLANES * t2)),
            _const_spec(wf.shape, j),
            _const_spec((1, d)),
            _const_spec(w1.shape, layer),
            _const_spec(w3.shape, layer),
            _const_spec(w2.shape, layer),
        ],
        out_specs=pl.BlockSpec((1, tp, SUBLANES, d), lambda bi, jj, hh: (bi, hh, jj, 0)),
        out_shape=jax.ShapeDtypeStruct((b, t2, DFT_T1, d), f32),
        compiler_params=_cparams(("arbitrary", "arbitrary", "arbitrary")),
        name="dft_ffn",
    )(x.reshape(b, t2, DFT_T1, d), z, mods, tabs["g2"], wf, n2g, w1, w3, w2)
    return out.reshape(b, t, d)


def _angle(i, j, n):
    return (2.0 * math.pi / n) * ((i * j) % n).astype(f32)


def _dft_tables(t):
    t1, t2 = DFT_T1, t // DFT_T1
    eye = jnp.eye(SUBLANES, dtype=f32)
    i1 = jnp.arange(t1, dtype=jnp.int32)
    ang64 = _angle(i1[:, None], i1[None, :], t1)
    s1 = t1 ** -0.5
    kron8 = lambda m: (m[:, None, :, None] * eye[None, :, None, :]).reshape(t1 * SUBLANES, t1 * SUBLANES)
    kw = jnp.concatenate([kron8(jnp.cos(ang64) * s1), kron8(-jnp.sin(ang64) * s1)], axis=0).astype(bf16)

    ic = jnp.arange(FOURIER_GROUP_DIM, dtype=jnp.int32)
    angc = _angle(ic[:, None], ic[None, :], FOURIER_GROUP_DIM)
    sc_ = FOURIER_GROUP_DIM ** -0.5
    cc, sn = jnp.cos(angc) * sc_, jnp.sin(angc) * sc_
    cd = jnp.concatenate([jnp.concatenate([cc, -sn], axis=1),
                          jnp.concatenate([sn, cc], axis=1)], axis=0).astype(bf16)

    i2 = jnp.arange(t2, dtype=jnp.int32)
    angt = _angle(i2[:, None], i1[None, :], t)
    angt = angt.reshape(t2 // SUBLANES, SUBLANES, t1).transpose(0, 2, 1).reshape(t2 // SUBLANES, -1)
    tw = lax.optimization_barrier((jnp.cos(angt), jnp.sin(angt)))
    tw_cos, tw_sin = (jnp.broadcast_to(v[:, :, None], v.shape + (LANES,)) for v in tw)

    ang2 = _angle(i2[:, None], i2[None, :], t2)
    s2 = t2 ** -0.5
    perm = lambda m: (m[:, None, None, :] * eye[None, :, :, None]).reshape(t2 * SUBLANES, SUBLANES * t2)
    g2 = jnp.concatenate([perm(jnp.cos(ang2) * s2), perm(jnp.sin(ang2) * s2)], axis=1).astype(bf16)
    return dict(kw=kw, cd=cd, tw_cos=tw_cos, tw_sin=tw_sin, g2=g2)


def _rope_tables(t):
    rows = t // GRID_W
    half = QK_ROPE_DIM // 2
    inv = 1.0 / (ROPE_BASE ** (jnp.arange(0, half, 2, dtype=f32) / half))
    ar = jnp.arange(rows, dtype=f32)[:, None] * inv
    ac = jnp.arange(GRID_W, dtype=f32)[:, None] * inv
    pad_hi = HEAD_PAD - QK_HEAD_DIM

    def place(n, first, second, at, lead):
        z = jnp.zeros((n, half), f32)
        blocks = [jnp.full((n, QK_NOPE_DIM), lead, f32)]
        blocks += [jnp.concatenate([first, second], axis=1), z] if at == 0 else \
                  [z, jnp.concatenate([first, second], axis=1)]
        blocks.append(jnp.zeros((n, pad_hi), f32))
        return jnp.concatenate(blocks, axis=1)

    rrow = jnp.stack([place(rows, jnp.cos(ar), jnp.cos(ar), 0, 0.0),
                      place(rows, -jnp.sin(ar), jnp.sin(ar), 0, 0.0)])
    rcol = jnp.stack([place(GRID_W, jnp.cos(ac), jnp.cos(ac), 1, 1.0),
                      place(GRID_W, -jnp.sin(ac), jnp.sin(ac), 1, 0.0)])
    return rrow, rcol


def _partner(a, axis):
    parts = jnp.split(a, 4, axis=axis)
    return jnp.concatenate([parts[1], parts[0], parts[3], parts[2]], axis=axis)


def _prep_even(j, w_in, q_norm_g, kv_norm_g, w_uq, w_ukv, q_gain, k_gain, w_o):
    hh, hp = MLA_HEADS, HEAD_PAD
    d = D_MODEL
    nope, hd = QK_NOPE_DIM, QK_HEAD_DIM
    wi = w_in[j]
    w_pe = wi[:, PE_COL:CONV_OFFSET]
    place = lambda wcols: jnp.zeros((d, hp), f32).at[:, nope:hd].set(wcols)
    w_in_p = jnp.concatenate([wi[:, :PE_COL], place(w_pe), place(_partner(w_pe, 1)),
                              wi[:, CONV_OFFSET:]], axis=1).astype(bf16)
    wq = w_uq[j].reshape(Q_LORA_RANK, hh, hd)
    pad_head = lambda a: jnp.pad(a, ((0, 0), (0, 0), (0, hp - a.shape[2]))).reshape(a.shape[0], hh * hp)
    wq_sw = jnp.zeros_like(wq).at[:, :, nope:].set(_partner(wq[:, :, nope:], 2))
    w_uq_p = jnp.concatenate([pad_head(wq), pad_head(wq_sw)], axis=1).astype(bf16)
    wkv = w_ukv[j].reshape(KV_LORA_RANK, hh, nope + V_HEAD_DIM)
    wv = wkv[:, :, nope:].reshape(KV_LORA_RANK, hh // 2, 2, V_HEAD_DIM)
    zv = jnp.zeros_like(wv[:, :, 0])
    wv = jnp.stack([jnp.concatenate([wv[:, :, 0], zv], axis=-1),
                    jnp.concatenate([zv, wv[:, :, 1]], axis=-1)], axis=2).reshape(KV_LORA_RANK, hh * hp)
    w_ukv_p = jnp.concatenate([pad_head(wkv[:, :, :nope]), wv], axis=1).astype(bf16)

    def gain_rows(g):
        sw = jnp.zeros_like(g).at[nope:].set(_partner(g[nope:], 0))
        return [jnp.pad(g, (0, hp - hd)), jnp.pad(sw, (0, hp - hd))]

    zero = jnp.zeros((hp,), f32)
    gains = jnp.stack(gain_rows(q_gain[j] * (QK_SCALE * LOG2E)) + gain_rows(k_gain[j]) + [zero] * 4)
    return dict(
        w_in=w_in_p, w_uq=w_uq_p, w_ukv=w_ukv_p,
        q_norm_g=q_norm_g[j].reshape(1, -1), kv_norm_g=kv_norm_g[j].reshape(1, -1), gains=gains,
        w_o_attn=w_o[j][:hh * V_HEAD_DIM].astype(bf16), w_o_conv=w_o[j][hh * V_HEAD_DIM:].astype(bf16))


def _pick(n, pref):
    return pref if n % pref == 0 else n


def kernel(x, c, ctx, c_ctx, ada_w, ada_b, norm1_g, norm2_g, w_in, q_norm_g, kv_norm_g, w_uq, w_ukv,
           q_gain, k_gain, conv_w, w_o, w_fourier, ffn_w1, ffn_w3, ffn_w2):
    b, s, d = x.shape
    depth = ada_w.shape[0]
    cvec = jnp.zeros((8, d), f32).at[:b].set(c).at[b].set(c_ctx)
    mods = _ada(cvec, ada_w, ada_b).reshape(depth, 8, 1, 6 * d)
    lat_row = lambda bi: bi
    ctx_row = lambda bi: b
    w1 = ffn_w1.astype(bf16)
    w3 = ffn_w3.astype(bf16)
    w2 = ffn_w2.astype(bf16)
    wf = w_fourier.astype(bf16)
    for i in range(depth):
        last = i == depth - 1
        j = i // 2
        n1g = norm1_g[i].reshape(1, d)
        n2g = norm2_g[i].reshape(1, d)
        if i % 2 == 0:
            w = _prep_even(j, w_in, q_norm_g, kv_norm_g, w_uq, w_ukv, q_gain, k_gain, w_o)
            tm = _pick(s, 512)
            lc = ctx.shape[1]
            q_l, k_l, v_l, pc_l = _qkv(x, mods, i, lat_row, n1g, w, _rope_tables(s), tm)
            q_c, k_c, v_c, pc_c = _qkv(ctx, mods, i, ctx_row, n1g, w, None, lc)
            bound = (QK_SCALE * LOG2E * QK_HEAD_DIM * BF16_ROUNDING_SLACK
                     * jnp.max(jnp.abs(q_gain[j])) * jnp.max(jnp.abs(k_gain[j])))
            tq, tk = _pick(s, 512), _pick(s, 1024)
            a_l = lax.cond(
                bound < EXP2_SAFE_SCORE,
                lambda *ops: _attention(ops[0], [ops[1:3], ops[3:5]], _pick(s, 1024), _pick(s, 256),
                                        online=False),
                lambda *ops: _attention(ops[0], [ops[1:3], ops[3:5]], tq, tk, online=True),
                q_l, k_l, v_l, k_c, v_c)
            x_new = _mix_ffn(x, a_l, pc_l, mods, i, lat_row, conv_w, j, w, n2g, w1, w3, w2, tm)
            if not last:
                a_c = _attention(q_c, [(k_c, v_c)], lc, lc)
                ctx = _mix_ffn(ctx, a_c, pc_c, mods, i, ctx_row, conv_w, j, w, n2g, w1, w3, w2, lc)
            x = x_new
        else:
            tabs = _dft_tables(s)
            t2 = s // DFT_T1
            z = _dft_a(x, mods, i, n1g, tabs)
            x = _dft_ffn(x, z, mods, i, tabs, wf, j, n2g, w1, w3, w2, _pick(t2, 64))
            assert last, "odd non-final layers are not implemented"
    return x
```

```python
import functools
import math

import jax
import jax.numpy as jnp
from jax import lax
from jax.experimental import pallas as pl
from jax.experimental.pallas import tpu as pltpu

D_MODEL = 1024
GRID_W = 64
MLA_HEADS = 8
QK_NOPE_DIM = 64
QK_ROPE_DIM = 32
QK_HEAD_DIM = QK_NOPE_DIM + QK_ROPE_DIM
V_HEAD_DIM = 64
Q_LORA_RANK = 384
KV_LORA_RANK = 256
QK_SCALE = QK_HEAD_DIM ** -0.5
ROPE_BASE = 10000.0
CONV_DIM = 512
CONV_OFFSET = Q_LORA_RANK + KV_LORA_RANK + QK_ROPE_DIM
FOURIER_GROUPS = 4
FOURIER_GROUP_DIM = D_MODEL // FOURIER_GROUPS
EPS = 1e-6

LANES = 128
BF16_SUBLANES = 16
MXU_TILE = 256
VMEM_LIMIT_BYTES = 56 * 1024 * 1024

HEAD_PAD = LANES
ALL_HEADS = MLA_HEADS * HEAD_PAD
PE_COL = Q_LORA_RANK + KV_LORA_RANK
CONV_COL = PE_COL + HEAD_PAD
IN_PROJ_PAD = CONV_COL + 3 * CONV_DIM
QKV_ROW_BLOCK = 128
DFT_T1 = 64
LOG2E = math.log2(math.e)
EXP2_SAFE_SCORE = 64.0
BF16_ROUNDING_SLACK = 1.02
NT_DIMS = (((1,), (1,)), ((), ()))

bf16 = jnp.bfloat16
f32 = jnp.float32


def _cparams(sem):
    return pltpu.CompilerParams(dimension_semantics=sem, vmem_limit_bytes=VMEM_LIMIT_BYTES)


def _const_spec(shape, lead=None):
    if lead is None:
        nd = len(shape)
        return pl.BlockSpec(shape, lambda *_: (0,) * nd, pipeline_mode=pl.Buffered(1))
    nd = len(shape) - 1
    return pl.BlockSpec((1,) + tuple(shape[1:]), lambda *_: (lead,) + (0,) * nd,
                        pipeline_mode=pl.Buffered(1))


def _mod_spec(layer, mod_row):
    return pl.BlockSpec((1, 1, 1, 6 * D_MODEL), lambda bi, i: (layer, mod_row(bi), 0, 0))


def _dot(a, b):
    return jnp.dot(a, b, preferred_element_type=f32)


def _rms_scale(x, width):
    return lax.rsqrt(jnp.sum(x * x, axis=-1, keepdims=True) * (1.0 / width) + EPS)


def _ada_body(c_ref, w_ref, b_ref, o_ref):
    c = c_ref[...]
    s = c / (1.0 + jnp.exp(-c))
    o_ref[0] = _dot(s.astype(bf16), w_ref[0].astype(bf16)) + b_ref[0]


def _ada(cvec, ada_w, ada_b):
    depth, d, n = ada_w.shape
    tn = 1536
    return pl.pallas_call(
        _ada_body,
        grid=(depth, n // tn),
        in_specs=[
            pl.BlockSpec((8, d), lambda l, j: (0, 0)),
            pl.BlockSpec((1, d, tn), lambda l, j: (l, 0, j)),
            pl.BlockSpec((1, 1, tn), lambda l, j: (l, 0, j)),
        ],
        out_specs=pl.BlockSpec((1, 8, tn), lambda l, j: (l, 0, j)),
        out_shape=jax.ShapeDtypeStruct((depth, 8, n), f32),
        compiler_params=_cparams(("arbitrary", "arbitrary")),
        name="ada",
    )(cvec, ada_w, ada_b.reshape(depth, 1, n))


def _modulated_norm(x, mod, g_ref, lo):
    d = D_MODEL
    shift = mod[:, lo:lo + d]
    scale = mod[:, lo + d:lo + 2 * d]
    gain = g_ref[...] * (1.0 + scale)
    return (x * _rms_scale(x, d)) * gain + shift


def _qkv_body(*refs, rope):
    (x_ref, mod_ref, n1g_ref, win_ref, qng_ref, kvng_ref, wuq_ref, wukv_ref, gains_ref) = refs[:9]
    if rope:
        rrow_ref, rcol_ref = refs[9:11]
        refs = refs[11:]
    else:
        refs = refs[9:]
    q_out, k_out, v_out, pc_out, q_scr, kv_scr, pe_scr = refs
    x = x_ref[0]
    tm = x.shape[0]
    h = _modulated_norm(x, mod_ref[0, 0], n1g_ref, 0)
    p = _dot(h.astype(bf16), win_ref[...])
    pc_out[0] = p[:, CONV_COL:].astype(bf16)
    pe_scr[...] = p[:, PE_COL:CONV_COL]
    cq = p[:, :Q_LORA_RANK]
    cqn = cq * _rms_scale(cq, Q_LORA_RANK) * qng_ref[...]
    q_scr[...] = _dot(cqn.astype(bf16), wuq_ref[...])
    ckv = p[:, Q_LORA_RANK:PE_COL]
    ckvn = ckv * _rms_scale(ckv, KV_LORA_RANK) * kvng_ref[...]
    kv_scr[...] = _dot(ckvn.astype(bf16), wukv_ref[...])

    gains = gains_ref[...]
    lane = lax.broadcasted_iota(jnp.int32, (1, HEAD_PAD), 1)
    real = (lane < QK_HEAD_DIM).astype(f32)
    v_keep = [(lane < V_HEAD_DIM).astype(f32), (lane >= V_HEAD_DIM).astype(f32)]
    ones_col = [(lane == _denominator_lane(par)).astype(f32) for par in range(2)]
    rb = min(QKV_ROW_BLOCK, tm)
    per = rb // GRID_W
    grid_row0 = pl.program_id(1) * (tm // GRID_W)
    to_rope = HEAD_PAD - QK_ROPE_DIM

    def qk_scale(t):
        return lax.rsqrt(jnp.sum(t * t * real, axis=-1, keepdims=True) * (1.0 / QK_HEAD_DIM) + EPS)

    def block(ib, carry):
        r0 = pl.multiple_of(ib * rb, rb)
        rows = pl.ds(r0, rb)
        if rope:
            tabs = []
            for comp in range(2):
                parts = []
                for g in range(per):
                    rr = rrow_ref[comp, pl.ds(grid_row0 + ib * per + g, 1), :]
                    parts.append(rcol_ref[comp] + rr)
                tabs.append(jnp.concatenate(parts, axis=0))
            cos, sin = tabs
            qa, qb = cos * gains[0:1], sin * gains[1:2]
            ka, kb = cos * gains[2:3], sin * gains[3:4]
        pe = pe_scr[rows, :]
        if rope:
            pe_rot = pltpu.roll(pe, to_rope, axis=1) * kb
        for hd in range(MLA_HEADS):
            sl = slice(hd * HEAD_PAD, (hd + 1) * HEAD_PAD)
            qh = q_scr[rows, sl]
            kh = kv_scr[rows, sl] + pe
            rq = qk_scale(qh)
            rk = qk_scale(kh)
            if rope:
                partner = slice(ALL_HEADS + hd * HEAD_PAD, ALL_HEADS + (hd + 1) * HEAD_PAD)
                qo = (qh * qa + q_scr[rows, partner] * qb) * rq
                ko = (kh * ka + pe_rot) * rk
            else:
                qo = qh * gains[0:1] * rq
                ko = kh * gains[2:3] * rk
            q_out[0, hd, rows, :] = qo.astype(bf16)
            k_out[0, hd, rows, :] = ko.astype(bf16)
            pair = slice(ALL_HEADS + (hd // 2) * HEAD_PAD, ALL_HEADS + (hd // 2 + 1) * HEAD_PAD)
            v_out[0, hd, rows, :] = (kv_scr[rows, pair] * v_keep[hd % 2] + ones_col[hd % 2]).astype(bf16)
        return carry

    lax.fori_loop(0, tm // rb, block, 0)


def _qkv(x, mods, layer, mod_row, n1g, w, rope_tabs, tm):
    b, t, d = x.shape
    hh, hp = MLA_HEADS, HEAD_PAD
    rope = rope_tabs is not None
    in_specs = [
        pl.BlockSpec((1, tm, d), lambda bi, i: (bi, i, 0)),
        _mod_spec(layer, mod_row),
        _const_spec((1, d)),
        _const_spec(w["w_in"].shape),
        _const_spec((1, Q_LORA_RANK)),
        _const_spec((1, KV_LORA_RANK)),
        _const_spec(w["w_uq"].shape),
        _const_spec(w["w_ukv"].shape),
        _const_spec((8, hp)),
    ]
    args = [x, mods, n1g, w["w_in"], w["q_norm_g"], w["kv_norm_g"], w["w_uq"], w["w_ukv"], w["gains"]]
    if rope:
        rrow, rcol = rope_tabs
        in_specs += [_const_spec(rrow.shape), _const_spec(rcol.shape)]
        args += [rrow, rcol]
    head_spec = pl.BlockSpec((1, hh, tm, hp), lambda bi, i: (bi, 0, i, 0))
    head_shape = jax.ShapeDtypeStruct((b, hh, t, hp), bf16)
    return pl.pallas_call(
        functools.partial(_qkv_body, rope=rope),
        grid=(b, t // tm),
        in_specs=in_specs,
        out_specs=[head_spec, head_spec, head_spec,
                   pl.BlockSpec((1, tm, 3 * CONV_DIM), lambda bi, i: (bi, i, 0))],
        out_shape=[head_shape, head_shape, head_shape,
                   jax.ShapeDtypeStruct((b, t, 3 * CONV_DIM), bf16)],
        scratch_shapes=[pltpu.VMEM((tm, 2 * ALL_HEADS), f32),
                        pltpu.VMEM((tm, ALL_HEADS + hh * V_HEAD_DIM), f32),
                        pltpu.VMEM((tm, hp), f32)],
        compiler_params=_cparams(("arbitrary", "arbitrary")),
        name="qkv_rope" if rope else "qkv_ctx",
    )(*args)


def _denominator_lane(parity):
    return V_HEAD_DIM if parity == 0 else 0


def _attn_head(q, k_refs, v_refs, par, chunks, online):
    m = None
    acc = None
    for src, (n_chunks, tk) in enumerate(chunks):
        for c in range(n_chunks):
            ks = slice(c * tk, (c + 1) * tk)
            s = lax.dot_general(q, k_refs[src][0, par, ks, :], NT_DIMS, preferred_element_type=f32)
            if online:
                m_cur = jnp.max(s, axis=1, keepdims=True)
                m_new = m_cur if m is None else jnp.maximum(m, m_cur)
                pv = _dot(jnp.exp2(s - m_new).astype(bf16), v_refs[src][0, par, ks, :])
                acc = pv if m is None else jnp.exp2(m - m_new) * acc + pv
                m = m_new
            else:
                pv = _dot(jnp.exp2(s).astype(bf16), v_refs[src][0, par, ks, :])
                acc = pv if acc is None else acc + pv
    lane = _denominator_lane(par)
    return acc / acc[:, lane:lane + 1]


def _attn_body(q_ref, *refs, chunks, online):
    o_ref = refs[-1]
    k_refs, v_refs = refs[0:-1:2], refs[1:-1:2]
    even = _attn_head(q_ref[0, 0], k_refs, v_refs, 0, chunks, online)
    odd = _attn_head(q_ref[0, 1], k_refs, v_refs, 1, chunks, online)
    lane = lax.broadcasted_iota(jnp.int32, even.shape, 1)
    o_ref[0] = jnp.where(lane < V_HEAD_DIM, even, odd).astype(bf16)


def _attention(q, kv_sources, tq, tk, online=True):
    b, hh, t, hp = q.shape
    pair_spec = lambda n: pl.BlockSpec((1, 2, n, hp), lambda bi, pi, i: (bi, pi, 0, 0))
    in_specs = [pl.BlockSpec((1, 2, tq, hp), lambda bi, pi, i: (bi, pi, i, 0))]
    args = [q]
    chunks = []
    for k, v in kv_sources:
        n = k.shape[2]
        step = min(tk, n)
        chunks.append((n // step, step))
        in_specs += [pair_spec(n), pair_spec(n)]
        args += [k, v]
    return pl.pallas_call(
        functools.partial(_attn_body, chunks=tuple(chunks), online=online),
        grid=(b, hh // 2, t // tq),
        in_specs=in_specs,
        out_specs=pl.BlockSpec((1, tq, hp), lambda bi, pi, i: (bi, i, pi)),
        out_shape=jax.ShapeDtypeStruct((b, t, hh * V_HEAD_DIM), bf16),
        compiler_params=_cparams(("arbitrary", "arbitrary", "arbitrary")),
        name="attn%d%s" % (len(kv_sources), "" if online else "_bounded"),
    )(*args)


def _gated_ffn(x, out, mod, n2g_ref, w1_ref, w3_ref, w2_ref):
    d = D_MODEL
    x1 = x + mod[:, 2 * d:3 * d] * out
    h2 = _modulated_norm(x1, mod, n2g_ref, 3 * d).astype(bf16)
    y = None
    for lo, hi in _ff_chunks(w1_ref.shape[2]):
        sl = slice(lo, hi)
        u = _dot(h2, w1_ref[0, :, sl])
        g = _dot(h2, w3_ref[0, :, sl])
        act = (u / (1.0 + jnp.exp(-u)) * g).astype(bf16)
        part = _dot(act, w2_ref[0, sl, :])
        y = part if y is None else y + part
    return x1 + mod[:, 5 * d:6 * d] * y


FFN_CHUNKS = 4


def _ff_chunks(d_ff):
    step = -(-d_ff // (FFN_CHUNKS * MXU_TILE)) * MXU_TILE
    return [(lo, min(lo + step, d_ff)) for lo in range(0, d_ff, step)]


def _mix_ffn_body(*refs, halo):
    x_ref, a_ref, pc_ref = refs[:3]
    if halo:
        pprev_ref, pnext_ref = refs[3:5]
        refs = refs[5:]
    else:
        refs = refs[3:]
    (mod_ref, convw_ref, woa_ref, woc_ref, n2g_ref, w1_ref, w3_ref, w2_ref, o_ref) = refs
    x = x_ref[0]
    mod = mod_ref[0, 0]
    tm = x.shape[0]

    out = _dot(a_ref[0], woa_ref[...])

    cd = CONV_DIM
    pc = pc_ref[0].astype(f32)
    z = pc[:, cd:2 * cd] * pc[:, 2 * cd:]
    row = lax.broadcasted_iota(jnp.int32, (tm, cd), 0)
    if halo:
        i = pl.program_id(1)
        last = pl.num_programs(1) - 1
        pp = pprev_ref[0, BF16_SUBLANES - 1:BF16_SUBLANES, :].astype(f32)
        pn = pnext_ref[0, 0:1, :].astype(f32)
        z_prev = jnp.where(i > 0, pp[:, cd:2 * cd] * pp[:, 2 * cd:], 0.0)
        z_next = jnp.where(i < last, pn[:, cd:2 * cd] * pn[:, 2 * cd:], 0.0)
    else:
        z_prev = jnp.zeros((1, cd), f32)
        z_next = jnp.zeros((1, cd), f32)
    z_up = jnp.where(row == 0, z_prev, pltpu.roll(z, 1, axis=0))
    z_dn = jnp.where(row == tm - 1, z_next, pltpu.roll(z, tm - 1, axis=0))
    cw = convw_ref[0]
    y = z_up * cw[0:1, :] + z * cw[1:2, :] + z_dn * cw[2:3, :]
    out = out + _dot((pc[:, :cd] * y).astype(bf16), woc_ref[...])

    o_ref[0] = _gated_ffn(x, out, mod, n2g_ref, w1_ref, w3_ref, w2_ref)


def _mix_ffn(x, a, pc, mods, layer, mod_row, conv_w, j, w, n2g, w1, w3, w2, tm):
    b, t, d = x.shape
    hh, hp = MLA_HEADS, HEAD_PAD
    halo = t > tm
    in_specs = [
        pl.BlockSpec((1, tm, d), lambda bi, i: (bi, i, 0)),
        pl.BlockSpec((1, tm, hh * V_HEAD_DIM), lambda bi, i: (bi, i, 0)),
        pl.BlockSpec((1, tm, 3 * CONV_DIM), lambda bi, i: (bi, i, 0)),
    ]
    args = [x, a, pc]
    if halo:
        per = tm // BF16_SUBLANES
        nblk = t // BF16_SUBLANES
        in_specs += [
            pl.BlockSpec((1, BF16_SUBLANES, 3 * CONV_DIM),
                         lambda bi, i: (bi, jnp.maximum(i * per - 1, 0), 0)),
            pl.BlockSpec((1, BF16_SUBLANES, 3 * CONV_DIM),
                         lambda bi, i: (bi, jnp.minimum((i + 1) * per, nblk - 1), 0)),
        ]
        args += [pc, pc]
    in_specs += [
        _mod_spec(layer, mod_row),
        _const_spec(conv_w.shape, j),
        _const_spec(w["w_o_attn"].shape),
        _const_spec(w["w_o_conv"].shape),
        _const_spec((1, d)),
        _const_spec(w1.shape, layer),
        _const_spec(w3.shape, layer),
        _const_spec(w2.shape, layer),
    ]
    args += [mods, conv_w, w["w_o_attn"], w["w_o_conv"], n2g, w1, w3, w2]
    return pl.pallas_call(
        functools.partial(_mix_ffn_body, halo=halo),
        grid=(b, t // tm),
        in_specs=in_specs,
        out_specs=pl.BlockSpec((1, tm, d), lambda bi, i: (bi, i, 0)),
        out_shape=jax.ShapeDtypeStruct((b, t, d), f32),
        compiler_params=_cparams(("arbitrary", "arbitrary")),
        name="mix_ffn_halo" if halo else "mix_ffn",
    )(*args)


SUBLANES = 8


def _dft_a_body(x_ref, mod_ref, n1g_ref, kw_ref, cd_ref, twc_ref, tws_ref, o_ref):
    d = D_MODEL
    gd = FOURIER_GROUP_DIM
    rows = DFT_T1 * SUBLANES
    mod = mod_ref[0, 0]
    halves = [[], []]
    for half in range(x_ref.shape[2] // SUBLANES):
        sub = slice(half * SUBLANES, (half + 1) * SUBLANES)
        x = x_ref[0, :, sub, :].reshape(rows, d)
        h = _modulated_norm(x, mod, n1g_ref, 0).astype(bf16)
        a = _dot(kw_ref[...], h).astype(bf16)
        tw_c = twc_ref[half]
        tw_s = tws_ref[half]
        re_cols, im_cols = [], []
        for g in range(FOURIER_GROUPS):
            sl = slice(g * gd, (g + 1) * gd)
            z = _dot(jnp.concatenate([a[:rows, sl], a[rows:, sl]], axis=1), cd_ref[...])
            z_re, z_im = z[:, :gd], z[:, gd:]
            c = jnp.tile(tw_c, (1, gd // LANES))
            s = jnp.tile(tw_s, (1, gd // LANES))
            re_cols.append(z_re * c + z_im * s)
            im_cols.append(z_im * c - z_re * s)
        halves[0].append(jnp.concatenate(re_cols, axis=1).reshape(DFT_T1, SUBLANES, d))
        halves[1].append(jnp.concatenate(im_cols, axis=1).reshape(DFT_T1, SUBLANES, d))
    for comp in range(2):
        o_ref[0, comp] = jnp.concatenate(halves[comp], axis=1).astype(bf16)


def _dft_a(x, mods, layer, n1g, tabs):
    b, t, d = x.shape
    t2 = t // DFT_T1
    k = BF16_SUBLANES
    rows = DFT_T1 * SUBLANES
    return pl.pallas_call(
        _dft_a_body,
        grid=(b, t2 // k),
        in_specs=[
            pl.BlockSpec((1, DFT_T1, k, d), lambda bi, j: (bi, 0, j, 0)),
            _mod_spec(layer, lambda bi: bi),
            _const_spec((1, d)),
            _const_spec((2 * rows, rows)),
            _const_spec((2 * FOURIER_GROUP_DIM, 2 * FOURIER_GROUP_DIM)),
            pl.BlockSpec((k // SUBLANES, rows, LANES), lambda bi, j: (j, 0, 0)),
            pl.BlockSpec((k // SUBLANES, rows, LANES), lambda bi, j: (j, 0, 0)),
        ],
        out_specs=pl.BlockSpec((1, 2, DFT_T1, k, d), lambda bi, j: (bi, 0, 0, j, 0)),
        out_shape=jax.ShapeDtypeStruct((b, 2, DFT_T1, t2, d), bf16),
        compiler_params=_cparams(("arbitrary", "arbitrary")),
        name="dft_a",
    )(x.reshape(b, DFT_T1, t2, d), mods, n1g, tabs["kw"], tabs["cd"], tabs["tw_cos"], tabs["tw_sin"])


def _dft_ffn_body(x_ref, z_ref, mod_ref, g_ref, wf_ref, n2g_ref, w1_ref, w3_ref, w2_ref, o_ref):
    d = D_MODEL
    tp = x_ref.shape[1]
    rows = tp * SUBLANES
    r0 = pl.multiple_of(pl.program_id(2) * tp, tp)
    cs = g_ref[pl.ds(r0, tp), :]
    t2 = z_ref.shape[3]
    f = jnp.stack([_dot(cs, z_ref[0, :, a].reshape(2 * t2, d)) for a in range(SUBLANES)], axis=0)
    f = jnp.swapaxes(f, 0, 1).reshape(rows, d).astype(bf16)
    out = _dot(f, wf_ref[0])
    x = x_ref[0].reshape(rows, d)
    res = _gated_ffn(x, out, mod_ref[0, 0], n2g_ref, w1_ref, w3_ref, w2_ref)
    o_ref[0] = res.reshape(tp, SUBLANES, d)


def _dft_ffn(x, z, mods, layer, tabs, wf, j, n2g, w1, w3, w2, tp):
    b, t, d = x.shape
    t2 = t // DFT_T1
    out = pl.pallas_call(
        _dft_ffn_body,
        grid=(b, DFT_T1 // SUBLANES, t2 // tp),
        in_specs=[
            pl.BlockSpec((1, tp, SUBLANES, d), lambda bi, jj, hh: (bi, hh, jj, 0)),
            pl.BlockSpec((1, 2, SUBLANES, t2, d), lambda bi, jj, hh: (bi, 0, jj, 0, 0)),
            pl.BlockSpec((1, 1, 1, 6 * d), lambda bi, jj, hh: (layer, bi, 0, 0)),
            _const_spec((t2, 2 * t2)),
            _const_spec(wf.shape, j),
            _const_spec((1, d)),
            _const_spec(w1.shape, layer),
            _const_spec(w3.shape, layer),
            _const_spec(w2.shape, layer),
        ],
        out_specs=pl.BlockSpec((1, tp, SUBLANES, d), lambda bi, jj, hh: (bi, hh, jj, 0)),
        out_shape=jax.ShapeDtypeStruct((b, t2, DFT_T1, d), f32),
        compiler_params=_cparams(("arbitrary", "arbitrary", "arbitrary")),
        name="dft_ffn",
    )(x.reshape(b, t2, DFT_T1, d), z, mods, tabs["g2"], wf, n2g, w1, w3, w2)
    return out.reshape(b, t, d)


def _angle(i, j, n):
    return (2.0 * math.pi / n) * ((i * j) % n).astype(f32)


def _dft_tables(t):
    t1, t2 = DFT_T1, t // DFT_T1
    i1 = jnp.arange(t1, dtype=jnp.int32)
    r = jnp.arange(t1 * SUBLANES, dtype=jnp.int32)
    ang64 = _angle(r[:, None] // SUBLANES, r[None, :] // SUBLANES, t1)
    keep = jnp.where(r[:, None] % SUBLANES == r[None, :] % SUBLANES, t1 ** -0.5, 0.0)
    kw = jnp.concatenate([jnp.cos(ang64) * keep, -jnp.sin(ang64) * keep], axis=0).astype(bf16)

    ic = jnp.arange(FOURIER_GROUP_DIM, dtype=jnp.int32)
    angc = _angle(ic[:, None], ic[None, :], FOURIER_GROUP_DIM)
    sc_ = FOURIER_GROUP_DIM ** -0.5
    cc, sn = jnp.cos(angc) * sc_, jnp.sin(angc) * sc_
    cd = jnp.concatenate([jnp.concatenate([cc, -sn], axis=1),
                          jnp.concatenate([sn, cc], axis=1)], axis=0).astype(bf16)

    i2 = jnp.arange(t2, dtype=jnp.int32)
    angt = _angle(i2[:, None], i1[None, :], t)
    angt = angt.reshape(t2 // SUBLANES, SUBLANES, t1).transpose(0, 2, 1).reshape(t2 // SUBLANES, -1)
    tw = lax.optimization_barrier((jnp.cos(angt), jnp.sin(angt)))
    tw_cos, tw_sin = (jnp.broadcast_to(v[:, :, None], v.shape + (LANES,)) for v in tw)

    ang2 = _angle(i2[:, None], i2[None, :], t2)
    s2 = t2 ** -0.5
    g2 = jnp.concatenate([jnp.cos(ang2) * s2, jnp.sin(ang2) * s2], axis=1).astype(bf16)
    return dict(kw=kw, cd=cd, tw_cos=tw_cos, tw_sin=tw_sin, g2=g2)


def _rope_tables(t):
    rows = t // GRID_W
    half = QK_ROPE_DIM // 2
    inv = 1.0 / (ROPE_BASE ** (jnp.arange(0, half, 2, dtype=f32) / half))
    ar = jnp.arange(rows, dtype=f32)[:, None] * inv
    ac = jnp.arange(GRID_W, dtype=f32)[:, None] * inv
    pad_hi = HEAD_PAD - QK_HEAD_DIM

    def place(n, first, second, at, lead):
        z = jnp.zeros((n, half), f32)
        blocks = [jnp.full((n, QK_NOPE_DIM), lead, f32)]
        blocks += [jnp.concatenate([first, second], axis=1), z] if at == 0 else \
                  [z, jnp.concatenate([first, second], axis=1)]
        blocks.append(jnp.zeros((n, pad_hi), f32))
        return jnp.concatenate(blocks, axis=1)

    rrow = jnp.stack([place(rows, jnp.cos(ar), jnp.cos(ar), 0, 0.0),
                      place(rows, -jnp.sin(ar), jnp.sin(ar), 0, 0.0)])
    rcol = jnp.stack([place(GRID_W, jnp.cos(ac), jnp.cos(ac), 1, 1.0),
                      place(GRID_W, -jnp.sin(ac), jnp.sin(ac), 1, 0.0)])
    return rrow, rcol


def _partner(a, axis):
    parts = jnp.split(a, 4, axis=axis)
    return jnp.concatenate([parts[1], parts[0], parts[3], parts[2]], axis=axis)


def _prep_even(j, w_in, q_norm_g, kv_norm_g, w_uq, w_ukv, q_gain, k_gain, w_o):
    hh, hp = MLA_HEADS, HEAD_PAD
    d = D_MODEL
    nope, hd = QK_NOPE_DIM, QK_HEAD_DIM
    wi = w_in[j]
    w_pe = wi[:, PE_COL:CONV_OFFSET]
    w_in_p = jnp.concatenate([wi[:, :PE_COL], jnp.zeros((d, nope), f32), w_pe, _partner(w_pe, 1),
                              wi[:, CONV_OFFSET:]], axis=1).astype(bf16)
    wq = w_uq[j].reshape(Q_LORA_RANK, hh, hd)
    pad_head = lambda a: jnp.pad(a, ((0, 0), (0, 0), (0, hp - hd))).reshape(Q_LORA_RANK, hh * hp)
    wq_sw = jnp.zeros_like(wq).at[:, :, nope:].set(_partner(wq[:, :, nope:], 2))
    w_uq_p = jnp.concatenate([pad_head(wq), pad_head(wq_sw)], axis=1)
    wkv = w_ukv[j].reshape(KV_LORA_RANK, hh, nope + V_HEAD_DIM)
    wk = jnp.pad(wkv[:, :, :nope], ((0, 0), (0, 0), (0, hp - nope))).reshape(KV_LORA_RANK, hh * hp)
    wv = wkv[:, :, nope:].reshape(KV_LORA_RANK, hh * V_HEAD_DIM)
    w_ukv_p = jnp.concatenate([wk, wv], axis=1).astype(bf16)
    w_uq_p = w_uq_p.astype(bf16)

    def gain_rows(g):
        sw = jnp.zeros_like(g).at[nope:].set(_partner(g[nope:], 0))
        return [jnp.pad(g, (0, hp - hd)), jnp.pad(sw, (0, hp - hd))]

    zero = jnp.zeros((hp,), f32)
    gains = jnp.stack(gain_rows(q_gain[j] * (QK_SCALE * LOG2E)) + gain_rows(k_gain[j]) + [zero] * 4)
    return dict(
        w_in=w_in_p, w_uq=w_uq_p, w_ukv=w_ukv_p,
        q_norm_g=q_norm_g[j].reshape(1, -1), kv_norm_g=kv_norm_g[j].reshape(1, -1), gains=gains,
        w_o_attn=w_o[j][:hh * V_HEAD_DIM].astype(bf16), w_o_conv=w_o[j][hh * V_HEAD_DIM:].astype(bf16))


def _pick(n, pref):
    return pref if n % pref == 0 else n


def kernel(x, c, ctx, c_ctx, ada_w, ada_b, norm1_g, norm2_g, w_in, q_norm_g, kv_norm_g, w_uq, w_ukv,
           q_gain, k_gain, conv_w, w_o, w_fourier, ffn_w1, ffn_w3, ffn_w2):
    b, s, d = x.shape
    depth = ada_w.shape[0]
    cvec = jnp.zeros((8, d), f32).at[:b].set(c).at[b].set(c_ctx)
    mods = _ada(cvec, ada_w, ada_b).reshape(depth, 8, 1, 6 * d)
    lat_row = lambda bi: bi
    ctx_row = lambda bi: b
    w1 = ffn_w1.astype(bf16)
    w3 = ffn_w3.astype(bf16)
    w2 = ffn_w2.astype(bf16)
    wf = w_fourier.astype(bf16)
    for i in range(depth):
        last = i == depth - 1
        j = i // 2
        n1g = norm1_g[i].reshape(1, d)
        n2g = norm2_g[i].reshape(1, d)
        if i % 2 == 0:
            w = _prep_even(j, w_in, q_norm_g, kv_norm_g, w_uq, w_ukv, q_gain, k_gain, w_o)
            tm = _pick(s, 512)
            lc = ctx.shape[1]
            q_l, k_l, v_l, pc_l = _qkv(x, mods, i, lat_row, n1g, w, _rope_tables(s), tm)
            q_c, k_c, v_c, pc_c = _qkv(ctx, mods, i, ctx_row, n1g, w, None, lc)
            bound = (QK_SCALE * LOG2E * QK_HEAD_DIM * BF16_ROUNDING_SLACK
                     * jnp.max(jnp.abs(q_gain[j])) * jnp.max(jnp.abs(k_gain[j])))
            tq, tk = _pick(s, 512), _pick(s, 1024)
            a_l = lax.cond(
                bound < EXP2_SAFE_SCORE,
                lambda *ops: _attention(ops[0], [ops[1:3], ops[3:5]], _pick(s, 1024), _pick(s, 256),
                                        online=False),
                lambda *ops: _attention(ops[0], [ops[1:3], ops[3:5]], tq, tk, online=True),
                q_l, k_l, v_l, k_c, v_c)
            x_new = _mix_ffn(x, a_l, pc_l, mods, i, lat_row, conv_w, j, w, n2g, w1, w3, w2, tm)
            if not last:
                a_c = _attention(q_c, [(k_c, v_c)], lc, lc)
                ctx = _mix_ffn(ctx, a_c, pc_c, mods, i, ctx_row, conv_w, j, w, n2g, w1, w3, w2, lc)
            x = x_new
        else:
            tabs = _dft_tables(s)
            t2 = s // DFT_T1
            z = _dft_a(x, mods, i, n1g, tabs)
            x = _dft_ffn(x, z, mods, i, tabs, wf, j, n2g, w1, w3, w2, _pick(t2, 64))
            assert last, "odd non-final layers are not implemented"
    return x
```

```python
import functools
import math

import jax
import jax.numpy as jnp
from jax import lax
from jax.experimental import pallas as pl
from jax.experimental.pallas import tpu as pltpu

D_MODEL = 1024
GRID_W = 64
MLA_HEADS = 8
QK_NOPE_DIM = 64
QK_ROPE_DIM = 32
QK_HEAD_DIM = QK_NOPE_DIM + QK_ROPE_DIM
V_HEAD_DIM = 64
Q_LORA_RANK = 384
KV_LORA_RANK = 256
QK_SCALE = QK_HEAD_DIM ** -0.5
ROPE_BASE = 10000.0
CONV_DIM = 512
CONV_OFFSET = Q_LORA_RANK + KV_LORA_RANK + QK_ROPE_DIM
FOURIER_GROUPS = 4
FOURIER_GROUP_DIM = D_MODEL // FOURIER_GROUPS
EPS = 1e-6

LANES = 128
BF16_SUBLANES = 16
MXU_TILE = 256
VMEM_LIMIT_BYTES = 56 * 1024 * 1024

HEAD_PAD = LANES
ALL_HEADS = MLA_HEADS * HEAD_PAD
PE_COL = Q_LORA_RANK + KV_LORA_RANK
CONV_COL = PE_COL + HEAD_PAD
IN_PROJ_PAD = CONV_COL + 3 * CONV_DIM
QKV_ROW_BLOCK = 128
DFT_T1 = 64
LOG2E = math.log2(math.e)
EXP2_SAFE_SCORE = 64.0
BF16_ROUNDING_SLACK = 1.02
NT_DIMS = (((1,), (1,)), ((), ()))
V_T_ROWS = V_HEAD_DIM + BF16_SUBLANES

bf16 = jnp.bfloat16
f32 = jnp.float32


def _cparams(sem):
    return pltpu.CompilerParams(dimension_semantics=sem, vmem_limit_bytes=VMEM_LIMIT_BYTES)


def _const_spec(shape, lead=None):
    if lead is None:
        nd = len(shape)
        return pl.BlockSpec(shape, lambda *_: (0,) * nd, pipeline_mode=pl.Buffered(1))
    nd = len(shape) - 1
    return pl.BlockSpec((1,) + tuple(shape[1:]), lambda *_: (lead,) + (0,) * nd,
                        pipeline_mode=pl.Buffered(1))


def _mod_spec(layer, mod_row):
    return pl.BlockSpec((1, 1, 1, 6 * D_MODEL), lambda bi, i: (layer, mod_row(bi), 0, 0))


def _dot(a, b):
    return jnp.dot(a, b, preferred_element_type=f32)


def _rms_scale(x, width):
    return lax.rsqrt(jnp.sum(x * x, axis=-1, keepdims=True) * (1.0 / width) + EPS)


def _ada_body(c_ref, w_ref, b_ref, o_ref):
    c = c_ref[...]
    s = c / (1.0 + jnp.exp(-c))
    o_ref[0] = _dot(s.astype(bf16), w_ref[0].astype(bf16)) + b_ref[0]


def _ada(cvec, ada_w, ada_b):
    depth, d, n = ada_w.shape
    tn = 1536
    return pl.pallas_call(
        _ada_body,
        grid=(depth, n // tn),
        in_specs=[
            pl.BlockSpec((8, d), lambda l, j: (0, 0)),
            pl.BlockSpec((1, d, tn), lambda l, j: (l, 0, j)),
            pl.BlockSpec((1, 1, tn), lambda l, j: (l, 0, j)),
        ],
        out_specs=pl.BlockSpec((1, 8, tn), lambda l, j: (l, 0, j)),
        out_shape=jax.ShapeDtypeStruct((depth, 8, n), f32),
        compiler_params=_cparams(("arbitrary", "arbitrary")),
        name="ada",
    )(cvec, ada_w, ada_b.reshape(depth, 1, n))


def _modulated_norm(x, mod, g_ref, lo):
    d = D_MODEL
    shift = mod[:, lo:lo + d]
    scale = mod[:, lo + d:lo + 2 * d]
    gain = g_ref[...] * (1.0 + scale)
    return (x * _rms_scale(x, d)) * gain + shift


def _qkv_body(*refs, rope):
    (x_ref, mod_ref, n1g_ref, win_ref, qng_ref, kvng_ref, wuq_ref, wukv_ref, wvt_ref,
     gains_ref) = refs[:10]
    if rope:
        rrow_ref, rcol_ref = refs[10:12]
        refs = refs[12:]
    else:
        refs = refs[10:]
    q_out, k_out, v_out, vt_out, pc_out, q_scr, kv_scr, pe_scr = refs
    x = x_ref[0]
    tm = x.shape[0]
    h = _modulated_norm(x, mod_ref[0, 0], n1g_ref, 0)
    p = _dot(h.astype(bf16), win_ref[...])
    pc_out[0] = p[:, CONV_COL:].astype(bf16)
    pe_scr[...] = p[:, PE_COL:CONV_COL]
    cq = p[:, :Q_LORA_RANK]
    cqn = cq * _rms_scale(cq, Q_LORA_RANK) * qng_ref[...]
    q_scr[...] = _dot(cqn.astype(bf16), wuq_ref[...])
    ckv = p[:, Q_LORA_RANK:PE_COL]
    ckvn = ckv * _rms_scale(ckv, KV_LORA_RANK) * kvng_ref[...]
    ckvn = ckvn.astype(bf16)
    kv_scr[...] = _dot(ckvn, wukv_ref[...])
    vt = lax.dot_general(wvt_ref[...], ckvn, NT_DIMS, preferred_element_type=f32)
    ones_rows = (lax.broadcasted_iota(jnp.int32, (V_T_ROWS - V_HEAD_DIM, tm), 0) == 0).astype(f32)
    for hd in range(MLA_HEADS):
        vt_h = vt[hd * V_HEAD_DIM:(hd + 1) * V_HEAD_DIM]
        vt_out[0, hd] = jnp.concatenate([vt_h, ones_rows], axis=0).astype(bf16)

    gains = gains_ref[...]
    lane = lax.broadcasted_iota(jnp.int32, (1, HEAD_PAD), 1)
    real = (lane < QK_HEAD_DIM).astype(f32)
    v_keep = [(lane < V_HEAD_DIM).astype(f32), (lane >= V_HEAD_DIM).astype(f32)]
    ones_col = [(lane == _denominator_lane(par)).astype(f32) for par in range(2)]
    rb = min(QKV_ROW_BLOCK, tm)
    per = rb // GRID_W
    grid_row0 = pl.program_id(1) * (tm // GRID_W)
    to_rope = HEAD_PAD - QK_ROPE_DIM

    def qk_scale(t):
        return lax.rsqrt(jnp.sum(t * t * real, axis=-1, keepdims=True) * (1.0 / QK_HEAD_DIM) + EPS)

    def block(ib, carry):
        r0 = pl.multiple_of(ib * rb, rb)
        rows = pl.ds(r0, rb)
        if rope:
            tabs = []
            for comp in range(2):
                parts = []
                for g in range(per):
                    rr = rrow_ref[comp, pl.ds(grid_row0 + ib * per + g, 1), :]
                    parts.append(rcol_ref[comp] + rr)
                tabs.append(jnp.concatenate(parts, axis=0))
            cos, sin = tabs
            qa, qb = cos * gains[0:1], sin * gains[1:2]
            ka, kb = cos * gains[2:3], sin * gains[3:4]
        pe = pe_scr[rows, :]
        if rope:
            pe_rot = pltpu.roll(pe, to_rope, axis=1) * kb
        for hd in range(MLA_HEADS):
            sl = slice(hd * HEAD_PAD, (hd + 1) * HEAD_PAD)
            qh = q_scr[rows, sl]
            kh = kv_scr[rows, sl] + pe
            rq = qk_scale(qh)
            rk = qk_scale(kh)
            if rope:
                partner = slice(ALL_HEADS + hd * HEAD_PAD, ALL_HEADS + (hd + 1) * HEAD_PAD)
                qo = (qh * qa + q_scr[rows, partner] * qb) * rq
                ko = (kh * ka + pe_rot) * rk
            else:
                qo = qh * gains[0:1] * rq
                ko = kh * gains[2:3] * rk
            q_out[0, hd, rows, :] = qo.astype(bf16)
            k_out[0, hd, rows, :] = ko.astype(bf16)
            pair = slice(ALL_HEADS + (hd // 2) * HEAD_PAD, ALL_HEADS + (hd // 2 + 1) * HEAD_PAD)
            v_out[0, hd, rows, :] = (kv_scr[rows, pair] * v_keep[hd % 2] + ones_col[hd % 2]).astype(bf16)
        return carry

    lax.fori_loop(0, tm // rb, block, 0)


def _qkv(x, mods, layer, mod_row, n1g, w, rope_tabs, tm):
    b, t, d = x.shape
    hh, hp = MLA_HEADS, HEAD_PAD
    rope = rope_tabs is not None
    in_specs = [
        pl.BlockSpec((1, tm, d), lambda bi, i: (bi, i, 0)),
        _mod_spec(layer, mod_row),
        _const_spec((1, d)),
        _const_spec(w["w_in"].shape),
        _const_spec((1, Q_LORA_RANK)),
        _const_spec((1, KV_LORA_RANK)),
        _const_spec(w["w_uq"].shape),
        _const_spec(w["w_ukv"].shape),
        _const_spec(w["w_vt"].shape),
        _const_spec((8, hp)),
    ]
    args = [x, mods, n1g, w["w_in"], w["q_norm_g"], w["kv_norm_g"], w["w_uq"], w["w_ukv"], w["w_vt"],
            w["gains"]]
    if rope:
        rrow, rcol = rope_tabs
        in_specs += [_const_spec(rrow.shape), _const_spec(rcol.shape)]
        args += [rrow, rcol]
    head_spec = pl.BlockSpec((1, hh, tm, hp), lambda bi, i: (bi, 0, i, 0))
    head_shape = jax.ShapeDtypeStruct((b, hh, t, hp), bf16)
    return pl.pallas_call(
        functools.partial(_qkv_body, rope=rope),
        grid=(b, t // tm),
        in_specs=in_specs,
        out_specs=[head_spec, head_spec, head_spec,
                   pl.BlockSpec((1, hh, V_T_ROWS, tm), lambda bi, i: (bi, 0, 0, i)),
                   pl.BlockSpec((1, tm, 3 * CONV_DIM), lambda bi, i: (bi, i, 0))],
        out_shape=[head_shape, head_shape, head_shape,
                   jax.ShapeDtypeStruct((b, hh, V_T_ROWS, t), bf16),
                   jax.ShapeDtypeStruct((b, t, 3 * CONV_DIM), bf16)],
        scratch_shapes=[pltpu.VMEM((tm, 2 * ALL_HEADS), f32),
                        pltpu.VMEM((tm, ALL_HEADS + hh * V_HEAD_DIM), f32),
                        pltpu.VMEM((tm, hp), f32)],
        compiler_params=_cparams(("arbitrary", "arbitrary")),
        name="qkv_rope" if rope else "qkv_ctx",
    )(*args)


def _denominator_lane(parity):
    return V_HEAD_DIM if parity == 0 else 0


def _attn_head(q, k_refs, v_refs, par, chunks):
    m = None
    acc = None
    for src, (n_chunks, tk) in enumerate(chunks):
        for c in range(n_chunks):
            ks = slice(c * tk, (c + 1) * tk)
            s = lax.dot_general(q, k_refs[src][0, par, ks, :], NT_DIMS, preferred_element_type=f32)
            m_cur = jnp.max(s, axis=1, keepdims=True)
            m_new = m_cur if m is None else jnp.maximum(m, m_cur)
            pv = _dot(jnp.exp2(s - m_new).astype(bf16), v_refs[src][0, par, ks, :])
            acc = pv if m is None else jnp.exp2(m - m_new) * acc + pv
            m = m_new
    lane = _denominator_lane(par)
    return acc / acc[:, lane:lane + 1]


def _attn_body(q_ref, *refs, chunks):
    o_ref = refs[-1]
    k_refs, v_refs = refs[0:-1:2], refs[1:-1:2]
    even = _attn_head(q_ref[0, 0], k_refs, v_refs, 0, chunks)
    odd = _attn_head(q_ref[0, 1], k_refs, v_refs, 1, chunks)
    lane = lax.broadcasted_iota(jnp.int32, even.shape, 1)
    o_ref[0] = jnp.where(lane < V_HEAD_DIM, even, odd).astype(bf16)


def _attention(q, kv_sources, tq, tk):
    b, hh, t, hp = q.shape
    pair_spec = lambda n: pl.BlockSpec((1, 2, n, hp), lambda bi, pi, i: (bi, pi, 0, 0))
    in_specs = [pl.BlockSpec((1, 2, tq, hp), lambda bi, pi, i: (bi, pi, i, 0))]
    args = [q]
    chunks = []
    for k, v in kv_sources:
        n = k.shape[2]
        step = min(tk, n)
        chunks.append((n // step, step))
        in_specs += [pair_spec(n), pair_spec(n)]
        args += [k, v]
    return pl.pallas_call(
        functools.partial(_attn_body, chunks=tuple(chunks)),
        grid=(b, hh // 2, t // tq),
        in_specs=in_specs,
        out_specs=pl.BlockSpec((1, tq, hp), lambda bi, pi, i: (bi, i, pi)),
        out_shape=jax.ShapeDtypeStruct((b, t, hh * V_HEAD_DIM), bf16),
        compiler_params=_cparams(("arbitrary", "arbitrary", "arbitrary")),
        name="attn%d" % len(kv_sources),
    )(*args)


def _attn_t_body(q_ref, *refs, chunks):
    o_ref = refs[-1]
    k_refs, vt_refs = refs[0:-1:2], refs[1:-1:2]
    outs = []
    for par in range(2):
        q = q_ref[0, par]
        acc = None
        for src, (n_chunks, tk) in enumerate(chunks):
            for c in range(n_chunks):
                ks = slice(c * tk, (c + 1) * tk)
                st = lax.dot_general(k_refs[src][0, par, ks, :], q, NT_DIMS, preferred_element_type=f32)
                part = _dot(vt_refs[src][0, par, :, ks], jnp.exp2(st).astype(bf16))
                acc = part if acc is None else acc + part
        outs.append(acc[:V_HEAD_DIM] / acc[V_HEAD_DIM:V_HEAD_DIM + 1])
    o_ref[0] = jnp.concatenate(outs, axis=0).T.astype(bf16)


def _attention_t(q, kv_sources, tq, tk):
    b, hh, t, hp = q.shape
    in_specs = [pl.BlockSpec((1, 2, tq, hp), lambda bi, pi, i: (bi, pi, i, 0))]
    args = [q]
    chunks = []
    for k, vt in kv_sources:
        n = k.shape[2]
        step = min(tk, n)
        chunks.append((n // step, step))
        in_specs += [pl.BlockSpec((1, 2, n, hp), lambda bi, pi, i: (bi, pi, 0, 0)),
                     pl.BlockSpec((1, 2, V_T_ROWS, n), lambda bi, pi, i: (bi, pi, 0, 0))]
        args += [k, vt]
    return pl.pallas_call(
        functools.partial(_attn_t_body, chunks=tuple(chunks)),
        grid=(b, hh // 2, t // tq),
        in_specs=in_specs,
        out_specs=pl.BlockSpec((1, tq, hp), lambda bi, pi, i: (bi, i, pi)),
        out_shape=jax.ShapeDtypeStruct((b, t, hh * V_HEAD_DIM), bf16),
        compiler_params=_cparams(("arbitrary", "arbitrary", "arbitrary")),
        name="attn%d_bounded" % len(kv_sources),
    )(*args)


def _gated_ffn(x, out, mod, n2g_ref, w1_ref, w3_ref, w2_ref):
    d = D_MODEL
    x1 = x + mod[:, 2 * d:3 * d] * out
    h2 = _modulated_norm(x1, mod, n2g_ref, 3 * d).astype(bf16)
    y = None
    for lo, hi in _ff_chunks(w1_ref.shape[2]):
        sl = slice(lo, hi)
        u = _dot(h2, w1_ref[0, :, sl])
        g = _dot(h2, w3_ref[0, :, sl])
        act = (u / (1.0 + jnp.exp(-u)) * g).astype(bf16)
        part = _dot(act, w2_ref[0, sl, :])
        y = part if y is None else y + part
    return x1 + mod[:, 5 * d:6 * d] * y


FFN_CHUNKS = 4


def _ff_chunks(d_ff):
    step = -(-d_ff // (FFN_CHUNKS * MXU_TILE)) * MXU_TILE
    return [(lo, min(lo + step, d_ff)) for lo in range(0, d_ff, step)]


def _mix_ffn_body(*refs, halo):
    x_ref, a_ref, pc_ref = refs[:3]
    if halo:
        pprev_ref, pnext_ref = refs[3:5]
        refs = refs[5:]
    else:
        refs = refs[3:]
    (mod_ref, convw_ref, woa_ref, woc_ref, n2g_ref, w1_ref, w3_ref, w2_ref, o_ref) = refs
    x = x_ref[0]
    mod = mod_ref[0, 0]
    tm = x.shape[0]

    out = _dot(a_ref[0], woa_ref[...])

    cd = CONV_DIM
    pc = pc_ref[0].astype(f32)
    z = pc[:, cd:2 * cd] * pc[:, 2 * cd:]
    row = lax.broadcasted_iota(jnp.int32, (tm, cd), 0)
    if halo:
        i = pl.program_id(1)
        last = pl.num_programs(1) - 1
        pp = pprev_ref[0, BF16_SUBLANES - 1:BF16_SUBLANES, :].astype(f32)
        pn = pnext_ref[0, 0:1, :].astype(f32)
        z_prev = jnp.where(i > 0, pp[:, cd:2 * cd] * pp[:, 2 * cd:], 0.0)
        z_next = jnp.where(i < last, pn[:, cd:2 * cd] * pn[:, 2 * cd:], 0.0)
    else:
        z_prev = jnp.zeros((1, cd), f32)
        z_next = jnp.zeros((1, cd), f32)
    z_up = jnp.where(row == 0, z_prev, pltpu.roll(z, 1, axis=0))
    z_dn = jnp.where(row == tm - 1, z_next, pltpu.roll(z, tm - 1, axis=0))
    cw = convw_ref[0]
    y = z_up * cw[0:1, :] + z * cw[1:2, :] + z_dn * cw[2:3, :]
    out = out + _dot((pc[:, :cd] * y).astype(bf16), woc_ref[...])

    o_ref[0] = _gated_ffn(x, out, mod, n2g_ref, w1_ref, w3_ref, w2_ref)


def _mix_ffn(x, a, pc, mods, layer, mod_row, conv_w, j, w, n2g, w1, w3, w2, tm):
    b, t, d = x.shape
    hh, hp = MLA_HEADS, HEAD_PAD
    halo = t > tm
    in_specs = [
        pl.BlockSpec((1, tm, d), lambda bi, i: (bi, i, 0)),
        pl.BlockSpec((1, tm, hh * V_HEAD_DIM), lambda bi, i: (bi, i, 0)),
        pl.BlockSpec((1, tm, 3 * CONV_DIM), lambda bi, i: (bi, i, 0)),
    ]
    args = [x, a, pc]
    if halo:
        per = tm // BF16_SUBLANES
        nblk = t // BF16_SUBLANES
        in_specs += [
            pl.BlockSpec((1, BF16_SUBLANES, 3 * CONV_DIM),
                         lambda bi, i: (bi, jnp.maximum(i * per - 1, 0), 0)),
            pl.BlockSpec((1, BF16_SUBLANES, 3 * CONV_DIM),
                         lambda bi, i: (bi, jnp.minimum((i + 1) * per, nblk - 1), 0)),
        ]
        args += [pc, pc]
    in_specs += [
        _mod_spec(layer, mod_row),
        _const_spec(conv_w.shape, j),
        _const_spec(w["w_o_attn"].shape),
        _const_spec(w["w_o_conv"].shape),
        _const_spec((1, d)),
        _const_spec(w1.shape, layer),
        _const_spec(w3.shape, layer),
        _const_spec(w2.shape, layer),
    ]
    args += [mods, conv_w, w["w_o_attn"], w["w_o_conv"], n2g, w1, w3, w2]
    return pl.pallas_call(
        functools.partial(_mix_ffn_body, halo=halo),
        grid=(b, t // tm),
        in_specs=in_specs,
        out_specs=pl.BlockSpec((1, tm, d), lambda bi, i: (bi, i, 0)),
        out_shape=jax.ShapeDtypeStruct((b, t, d), f32),
        compiler_params=_cparams(("arbitrary", "arbitrary")),
        name="mix_ffn_halo" if halo else "mix_ffn",
    )(*args)


SUBLANES = 8


def _dft_a_body(x_ref, mod_ref, n1g_ref, kw_ref, cd_ref, twc_ref, tws_ref, o_ref):
    d = D_MODEL
    gd = FOURIER_GROUP_DIM
    rows = DFT_T1 * SUBLANES
    mod = mod_ref[0, 0]
    halves = [[], []]
    for half in range(x_ref.shape[2] // SUBLANES):
        sub = slice(half * SUBLANES, (half + 1) * SUBLANES)
        x = x_ref[0, :, sub, :].reshape(rows, d)
        h = _modulated_norm(x, mod, n1g_ref, 0).astype(bf16)
        a = _dot(kw_ref[...], h).astype(bf16)
        tw_c = twc_ref[half]
        tw_s = tws_ref[half]
        re_cols, im_cols = [], []
        for g in range(FOURIER_GROUPS):
            sl = slice(g * gd, (g + 1) * gd)
            z = _dot(jnp.concatenate([a[:rows, sl], a[rows:, sl]], axis=1), cd_ref[...])
            z_re, z_im = z[:, :gd], z[:, gd:]
            c = jnp.tile(tw_c, (1, gd // LANES))
            s = jnp.tile(tw_s, (1, gd // LANES))
            re_cols.append(z_re * c + z_im * s)
            im_cols.append(z_im * c - z_re * s)
        halves[0].append(jnp.concatenate(re_cols, axis=1).reshape(DFT_T1, SUBLANES, d))
        halves[1].append(jnp.concatenate(im_cols, axis=1).reshape(DFT_T1, SUBLANES, d))
    for comp in range(2):
        o_ref[0, comp] = jnp.concatenate(halves[comp], axis=1).astype(bf16)


def _dft_a(x, mods, layer, n1g, tabs):
    b, t, d = x.shape
    t2 = t // DFT_T1
    k = BF16_SUBLANES
    rows = DFT_T1 * SUBLANES
    return pl.pallas_call(
        _dft_a_body,
        grid=(b, t2 // k),
        in_specs=[
            pl.BlockSpec((1, DFT_T1, k, d), lambda bi, j: (bi, 0, j, 0)),
            _mod_spec(layer, lambda bi: bi),
            _const_spec((1, d)),
            _const_spec((2 * rows, rows)),
            _const_spec((2 * FOURIER_GROUP_DIM, 2 * FOURIER_GROUP_DIM)),
            pl.BlockSpec((k // SUBLANES, rows, LANES), lambda bi, j: (j, 0, 0)),
            pl.BlockSpec((k // SUBLANES, rows, LANES), lambda bi, j: (j, 0, 0)),
        ],
        out_specs=pl.BlockSpec((1, 2, DFT_T1, k, d), lambda bi, j: (bi, 0, 0, j, 0)),
        out_shape=jax.ShapeDtypeStruct((b, 2, DFT_T1, t2, d), bf16),
        compiler_params=_cparams(("arbitrary", "arbitrary")),
        name="dft_a",
    )(x.reshape(b, DFT_T1, t2, d), mods, n1g, tabs["kw"], tabs["cd"], tabs["tw_cos"], tabs["tw_sin"])


def _dft_ffn_body(x_ref, z_ref, mod_ref, g_ref, wf_ref, n2g_ref, w1_ref, w3_ref, w2_ref, o_ref):
    d = D_MODEL
    tp = x_ref.shape[1]
    rows = tp * SUBLANES
    r0 = pl.multiple_of(pl.program_id(2) * tp, tp)
    cs = g_ref[pl.ds(r0, tp), :]
    t2 = z_ref.shape[3]
    f = jnp.stack([_dot(cs, z_ref[0, :, a].reshape(2 * t2, d)) for a in range(SUBLANES)], axis=0)
    f = jnp.swapaxes(f, 0, 1).reshape(rows, d).astype(bf16)
    out = _dot(f, wf_ref[0])
    x = x_ref[0].reshape(rows, d)
    res = _gated_ffn(x, out, mod_ref[0, 0], n2g_ref, w1_ref, w3_ref, w2_ref)
    o_ref[0] = res.reshape(tp, SUBLANES, d)


def _dft_ffn(x, z, mods, layer, tabs, wf, j, n2g, w1, w3, w2, tp):
    b, t, d = x.shape
    t2 = t // DFT_T1
    out = pl.pallas_call(
        _dft_ffn_body,
        grid=(b, DFT_T1 // SUBLANES, t2 // tp),
        in_specs=[
            pl.BlockSpec((1, tp, SUBLANES, d), lambda bi, jj, hh: (bi, hh, jj, 0)),
            pl.BlockSpec((1, 2, SUBLANES, t2, d), lambda bi, jj, hh: (bi, 0, jj, 0, 0)),
            pl.BlockSpec((1, 1, 1, 6 * d), lambda bi, jj, hh: (layer, bi, 0, 0)),
            _const_spec((t2, 2 * t2)),
            _const_spec(wf.shape, j),
            _const_spec((1, d)),
            _const_spec(w1.shape, layer),
            _const_spec(w3.shape, layer),
            _const_spec(w2.shape, layer),
        ],
        out_specs=pl.BlockSpec((1, tp, SUBLANES, d), lambda bi, jj, hh: (bi, hh, jj, 0)),
        out_shape=jax.ShapeDtypeStruct((b, t2, DFT_T1, d), f32),
        compiler_params=_cparams(("arbitrary", "arbitrary", "arbitrary")),
        name="dft_ffn",
    )(x.reshape(b, t2, DFT_T1, d), z, mods, tabs["g2"], wf, n2g, w1, w3, w2)
    return out.reshape(b, t, d)


def _angle(i, j, n):
    return (2.0 * math.pi / n) * ((i * j) % n).astype(f32)


def _dft_tables(t):
    t1, t2 = DFT_T1, t // DFT_T1
    i1 = jnp.arange(t1, dtype=jnp.int32)
    r = jnp.arange(t1 * SUBLANES, dtype=jnp.int32)
    ang64 = _angle(r[:, None] // SUBLANES, r[None, :] // SUBLANES, t1)
    keep = jnp.where(r[:, None] % SUBLANES == r[None, :] % SUBLANES, t1 ** -0.5, 0.0)
    kw = jnp.concatenate([jnp.cos(ang64) * keep, -jnp.sin(ang64) * keep], axis=0).astype(bf16)

    ic = jnp.arange(FOURIER_GROUP_DIM, dtype=jnp.int32)
    angc = _angle(ic[:, None], ic[None, :], FOURIER_GROUP_DIM)
    sc_ = FOURIER_GROUP_DIM ** -0.5
    cc, sn = jnp.cos(angc) * sc_, jnp.sin(angc) * sc_
    cd = jnp.concatenate([jnp.concatenate([cc, -sn], axis=1),
                          jnp.concatenate([sn, cc], axis=1)], axis=0).astype(bf16)

    i2 = jnp.arange(t2, dtype=jnp.int32)
    angt = _angle(i2[:, None], i1[None, :], t)
    angt = angt.reshape(t2 // SUBLANES, SUBLANES, t1).transpose(0, 2, 1).reshape(t2 // SUBLANES, -1)
    tw = lax.optimization_barrier((jnp.cos(angt), jnp.sin(angt)))
    tw_cos, tw_sin = (jnp.broadcast_to(v[:, :, None], v.shape + (LANES,)) for v in tw)

    ang2 = _angle(i2[:, None], i2[None, :], t2)
    s2 = t2 ** -0.5
    g2 = jnp.concatenate([jnp.cos(ang2) * s2, jnp.sin(ang2) * s2], axis=1).astype(bf16)
    return dict(kw=kw, cd=cd, tw_cos=tw_cos, tw_sin=tw_sin, g2=g2)


def _rope_tables(t):
    rows = t // GRID_W
    half = QK_ROPE_DIM // 2
    inv = 1.0 / (ROPE_BASE ** (jnp.arange(0, half, 2, dtype=f32) / half))
    ar = jnp.arange(rows, dtype=f32)[:, None] * inv
    ac = jnp.arange(GRID_W, dtype=f32)[:, None] * inv
    pad_hi = HEAD_PAD - QK_HEAD_DIM

    def place(n, first, second, at, lead):
        z = jnp.zeros((n, half), f32)
        blocks = [jnp.full((n, QK_NOPE_DIM), lead, f32)]
        blocks += [jnp.concatenate([first, second], axis=1), z] if at == 0 else \
                  [z, jnp.concatenate([first, second], axis=1)]
        blocks.append(jnp.zeros((n, pad_hi), f32))
        return jnp.concatenate(blocks, axis=1)

    rrow = jnp.stack([place(rows, jnp.cos(ar), jnp.cos(ar), 0, 0.0),
                      place(rows, -jnp.sin(ar), jnp.sin(ar), 0, 0.0)])
    rcol = jnp.stack([place(GRID_W, jnp.cos(ac), jnp.cos(ac), 1, 1.0),
                      place(GRID_W, -jnp.sin(ac), jnp.sin(ac), 1, 0.0)])
    return rrow, rcol


def _partner(a, axis):
    parts = jnp.split(a, 4, axis=axis)
    return jnp.concatenate([parts[1], parts[0], parts[3], parts[2]], axis=axis)


def _prep_even(j, w_in, q_norm_g, kv_norm_g, w_uq, w_ukv, q_gain, k_gain, w_o):
    hh, hp = MLA_HEADS, HEAD_PAD
    d = D_MODEL
    nope, hd = QK_NOPE_DIM, QK_HEAD_DIM
    wi = w_in[j]
    w_pe = wi[:, PE_COL:CONV_OFFSET]
    w_in_p = jnp.concatenate([wi[:, :PE_COL], jnp.zeros((d, nope), f32), w_pe, _partner(w_pe, 1),
                              wi[:, CONV_OFFSET:]], axis=1).astype(bf16)
    wq = w_uq[j].reshape(Q_LORA_RANK, hh, hd)
    pad_head = lambda a: jnp.pad(a, ((0, 0), (0, 0), (0, hp - hd))).reshape(Q_LORA_RANK, hh * hp)
    wq_sw = jnp.zeros_like(wq).at[:, :, nope:].set(_partner(wq[:, :, nope:], 2))
    w_uq_p = jnp.concatenate([pad_head(wq), pad_head(wq_sw)], axis=1)
    wkv = w_ukv[j].reshape(KV_LORA_RANK, hh, nope + V_HEAD_DIM)
    wk = jnp.pad(wkv[:, :, :nope], ((0, 0), (0, 0), (0, hp - nope))).reshape(KV_LORA_RANK, hh * hp)
    wv = wkv[:, :, nope:].reshape(KV_LORA_RANK, hh * V_HEAD_DIM)
    w_ukv_p = jnp.concatenate([wk, wv], axis=1).astype(bf16)
    w_vt = wv.T.astype(bf16)
    w_uq_p = w_uq_p.astype(bf16)

    def gain_rows(g):
        sw = jnp.zeros_like(g).at[nope:].set(_partner(g[nope:], 0))
        return [jnp.pad(g, (0, hp - hd)), jnp.pad(sw, (0, hp - hd))]

    zero = jnp.zeros((hp,), f32)
    gains = jnp.stack(gain_rows(q_gain[j] * (QK_SCALE * LOG2E)) + gain_rows(k_gain[j]) + [zero] * 4)
    return dict(
        w_in=w_in_p, w_uq=w_uq_p, w_ukv=w_ukv_p, w_vt=w_vt,
        q_norm_g=q_norm_g[j].reshape(1, -1), kv_norm_g=kv_norm_g[j].reshape(1, -1), gains=gains,
        w_o_attn=w_o[j][:hh * V_HEAD_DIM].astype(bf16), w_o_conv=w_o[j][hh * V_HEAD_DIM:].astype(bf16))


def _pick(n, pref):
    return pref if n % pref == 0 else n


def kernel(x, c, ctx, c_ctx, ada_w, ada_b, norm1_g, norm2_g, w_in, q_norm_g, kv_norm_g, w_uq, w_ukv,
           q_gain, k_gain, conv_w, w_o, w_fourier, ffn_w1, ffn_w3, ffn_w2):
    b, s, d = x.shape
    depth = ada_w.shape[0]
    cvec = jnp.zeros((8, d), f32).at[:b].set(c).at[b].set(c_ctx)
    mods = _ada(cvec, ada_w, ada_b).reshape(depth, 8, 1, 6 * d)
    lat_row = lambda bi: bi
    ctx_row = lambda bi: b
    w1 = ffn_w1.astype(bf16)
    w3 = ffn_w3.astype(bf16)
    w2 = ffn_w2.astype(bf16)
    wf = w_fourier.astype(bf16)
    for i in range(depth):
        last = i == depth - 1
        j = i // 2
        n1g = norm1_g[i].reshape(1, d)
        n2g = norm2_g[i].reshape(1, d)
        if i % 2 == 0:
            w = _prep_even(j, w_in, q_norm_g, kv_norm_g, w_uq, w_ukv, q_gain, k_gain, w_o)
            tm = _pick(s, 512)
            lc = ctx.shape[1]
            q_l, k_l, v_l, vt_l, pc_l = _qkv(x, mods, i, lat_row, n1g, w, _rope_tables(s), tm)
            q_c, k_c, v_c, vt_c, pc_c = _qkv(ctx, mods, i, ctx_row, n1g, w, None, lc)
            bound = (QK_SCALE * LOG2E * QK_HEAD_DIM * BF16_ROUNDING_SLACK
                     * jnp.max(jnp.abs(q_gain[j])) * jnp.max(jnp.abs(k_gain[j])))
            tq, tk = _pick(s, 512), _pick(s, 1024)
            a_l = lax.cond(
                bound < EXP2_SAFE_SCORE,
                lambda q_, kl, vl, vtl, kc, vc, vtc: _attention_t(q_, [(kl, vtl), (kc, vtc)],
                                                                  _pick(s, 1024), _pick(s, 512)),
                lambda q_, kl, vl, vtl, kc, vc, vtc: _attention(q_, [(kl, vl), (kc, vc)], tq, tk),
                q_l, k_l, v_l, vt_l, k_c, v_c, vt_c)
            x_new = _mix_ffn(x, a_l, pc_l, mods, i, lat_row, conv_w, j, w, n2g, w1, w3, w2, tm)
            if not last:
                a_c = _attention(q_c, [(k_c, v_c)], lc, lc)
                ctx = _mix_ffn(ctx, a_c, pc_c, mods, i, ctx_row, conv_w, j, w, n2g, w1, w3, w2, lc)
            x = x_new
        else:
            tabs = _dft_tables(s)
            t2 = s // DFT_T1
            z = _dft_a(x, mods, i, n1g, tabs)
            x = _dft_ffn(x, z, mods, i, tabs, wf, j, n2g, w1, w3, w2, _pick(t2, 64))
            assert last, "odd non-final layers are not implemented"
    return x
```

```python
import functools
import math

import jax
import jax.numpy as jnp
from jax import lax
from jax.experimental import pallas as pl
from jax.experimental.pallas import tpu as pltpu

D_MODEL = 1024
GRID_W = 64
MLA_HEADS = 8
QK_NOPE_DIM = 64
QK_ROPE_DIM = 32
QK_HEAD_DIM = QK_NOPE_DIM + QK_ROPE_DIM
V_HEAD_DIM = 64
Q_LORA_RANK = 384
KV_LORA_RANK = 256
QK_SCALE = QK_HEAD_DIM ** -0.5
ROPE_BASE = 10000.0
CONV_DIM = 512
CONV_OFFSET = Q_LORA_RANK + KV_LORA_RANK + QK_ROPE_DIM
FOURIER_GROUPS = 4
FOURIER_GROUP_DIM = D_MODEL // FOURIER_GROUPS
EPS = 1e-6

LANES = 128
BF16_SUBLANES = 16
MXU_TILE = 256
VMEM_LIMIT_BYTES = 56 * 1024 * 1024

HEAD_PAD = LANES
ALL_HEADS = MLA_HEADS * HEAD_PAD
PE_COL = Q_LORA_RANK + KV_LORA_RANK
CONV_COL = PE_COL + HEAD_PAD
IN_PROJ_PAD = CONV_COL + 3 * CONV_DIM
QKV_ROW_BLOCK = 128
DFT_T1 = 64
LOG2E = math.log2(math.e)
EXP2_SAFE_SCORE = 64.0
BF16_ROUNDING_SLACK = 1.02
NT_DIMS = (((1,), (1,)), ((), ()))
V_T_ROWS = V_HEAD_DIM + BF16_SUBLANES

bf16 = jnp.bfloat16
f32 = jnp.float32


def _cparams(sem):
    return pltpu.CompilerParams(dimension_semantics=sem, vmem_limit_bytes=VMEM_LIMIT_BYTES)


def _const_spec(shape, lead=None):
    if lead is None:
        nd = len(shape)
        return pl.BlockSpec(shape, lambda *_: (0,) * nd, pipeline_mode=pl.Buffered(1))
    nd = len(shape) - 1
    return pl.BlockSpec((1,) + tuple(shape[1:]), lambda *_: (lead,) + (0,) * nd,
                        pipeline_mode=pl.Buffered(1))


def _mod_spec(layer, mod_row):
    return pl.BlockSpec((1, 1, 1, 6 * D_MODEL), lambda bi, i: (layer, mod_row(bi), 0, 0))


def _dot(a, b):
    return jnp.dot(a, b, preferred_element_type=f32)


def _rms_scale(x, width):
    return lax.rsqrt(jnp.sum(x * x, axis=-1, keepdims=True) * (1.0 / width) + EPS)


def _ada_body(c_ref, w_ref, b_ref, o_ref):
    c = c_ref[...]
    s = c / (1.0 + jnp.exp(-c))
    o_ref[0] = _dot(s.astype(bf16), w_ref[0].astype(bf16)) + b_ref[0]


def _ada(cvec, ada_w, ada_b):
    depth, d, n = ada_w.shape
    tn = 1536
    return pl.pallas_call(
        _ada_body,
        grid=(depth, n // tn),
        in_specs=[
            pl.BlockSpec((8, d), lambda l, j: (0, 0)),
            pl.BlockSpec((1, d, tn), lambda l, j: (l, 0, j)),
            pl.BlockSpec((1, 1, tn), lambda l, j: (l, 0, j)),
        ],
        out_specs=pl.BlockSpec((1, 8, tn), lambda l, j: (l, 0, j)),
        out_shape=jax.ShapeDtypeStruct((depth, 8, n), f32),
        compiler_params=_cparams(("arbitrary", "arbitrary")),
        name="ada",
    )(cvec, ada_w, ada_b.reshape(depth, 1, n))


def _modulated_norm(x, mod, g_ref, lo):
    d = D_MODEL
    shift = mod[:, lo:lo + d]
    scale = mod[:, lo + d:lo + 2 * d]
    gain = g_ref[...] * (1.0 + scale)
    return (x * _rms_scale(x, d)) * gain + shift


def _qkv_body(*refs, rope):
    (x_ref, mod_ref, n1g_ref, win_ref, qng_ref, kvng_ref, wuq_ref, wukv_ref, wvt_ref,
     gains_ref) = refs[:10]
    if rope:
        rrow_ref, rcol_ref = refs[10:12]
        refs = refs[12:]
    else:
        refs = refs[10:]
    q_out, k_out, v_out, vt_out, pc_out, q_scr, kv_scr, pe_scr = refs
    x = x_ref[0]
    tm = x.shape[0]
    h = _modulated_norm(x, mod_ref[0, 0], n1g_ref, 0)
    p = _dot(h.astype(bf16), win_ref[...])
    pc_out[0] = p[:, CONV_COL:].astype(bf16)
    pe_scr[...] = p[:, PE_COL:CONV_COL]
    cq = p[:, :Q_LORA_RANK]
    cqn = cq * _rms_scale(cq, Q_LORA_RANK) * qng_ref[...]
    q_scr[...] = _dot(cqn.astype(bf16), wuq_ref[...])
    ckv = p[:, Q_LORA_RANK:PE_COL]
    ckvn = ckv * _rms_scale(ckv, KV_LORA_RANK) * kvng_ref[...]
    ckvn = ckvn.astype(bf16)
    kv_scr[...] = _dot(ckvn, wukv_ref[...])
    vt = lax.dot_general(wvt_ref[...], ckvn, NT_DIMS, preferred_element_type=f32)
    ones_rows = (lax.broadcasted_iota(jnp.int32, (V_T_ROWS - V_HEAD_DIM, tm), 0) == 0).astype(f32)
    for hd in range(MLA_HEADS):
        vt_h = vt[hd * V_HEAD_DIM:(hd + 1) * V_HEAD_DIM]
        vt_out[0, hd] = jnp.concatenate([vt_h, ones_rows], axis=0).astype(bf16)

    gains = gains_ref[...]
    lane = lax.broadcasted_iota(jnp.int32, (1, HEAD_PAD), 1)
    real = (lane < QK_HEAD_DIM).astype(f32)
    v_keep = [(lane < V_HEAD_DIM).astype(f32), (lane >= V_HEAD_DIM).astype(f32)]
    ones_col = [(lane == _denominator_lane(par)).astype(f32) for par in range(2)]
    rb = min(QKV_ROW_BLOCK, tm)
    per = rb // GRID_W
    grid_row0 = pl.program_id(1) * (tm // GRID_W)
    to_rope = HEAD_PAD - QK_ROPE_DIM

    def qk_scale(t):
        return lax.rsqrt(jnp.sum(t * t * real, axis=-1, keepdims=True) * (1.0 / QK_HEAD_DIM) + EPS)

    def block(ib, carry):
        r0 = pl.multiple_of(ib * rb, rb)
        rows = pl.ds(r0, rb)
        if rope:
            tabs = []
            for comp in range(2):
                parts = []
                for g in range(per):
                    rr = rrow_ref[comp, pl.ds(grid_row0 + ib * per + g, 1), :]
                    parts.append(rcol_ref[comp] + rr)
                tabs.append(jnp.concatenate(parts, axis=0))
            cos, sin = tabs
            qa, qb = cos * gains[0:1], sin * gains[1:2]
            ka, kb = cos * gains[2:3], sin * gains[3:4]
        pe = pe_scr[rows, :]
        if rope:
            pe_rot = pltpu.roll(pe, to_rope, axis=1) * kb
        for hd in range(MLA_HEADS):
            sl = slice(hd * HEAD_PAD, (hd + 1) * HEAD_PAD)
            qh = q_scr[rows, sl]
            kh = kv_scr[rows, sl] + pe
            rq = qk_scale(qh)
            rk = qk_scale(kh)
            if rope:
                partner = slice(ALL_HEADS + hd * HEAD_PAD, ALL_HEADS + (hd + 1) * HEAD_PAD)
                qo = (qh * qa + q_scr[rows, partner] * qb) * rq
                ko = (kh * ka + pe_rot) * rk
            else:
                qo = qh * gains[0:1] * rq
                ko = kh * gains[2:3] * rk
            q_out[0, hd, rows, :] = qo.astype(bf16)
            k_out[0, hd, rows, :] = ko.astype(bf16)
            pair = slice(ALL_HEADS + (hd // 2) * HEAD_PAD, ALL_HEADS + (hd // 2 + 1) * HEAD_PAD)
            v_out[0, hd, rows, :] = (kv_scr[rows, pair] * v_keep[hd % 2] + ones_col[hd % 2]).astype(bf16)
        return carry

    lax.fori_loop(0, tm // rb, block, 0)


def _qkv(x, mods, layer, mod_row, n1g, w, rope_tabs, tm):
    b, t, d = x.shape
    hh, hp = MLA_HEADS, HEAD_PAD
    rope = rope_tabs is not None
    in_specs = [
        pl.BlockSpec((1, tm, d), lambda bi, i: (bi, i, 0)),
        _mod_spec(layer, mod_row),
        _const_spec((1, d)),
        _const_spec(w["w_in"].shape),
        _const_spec((1, Q_LORA_RANK)),
        _const_spec((1, KV_LORA_RANK)),
        _const_spec(w["w_uq"].shape),
        _const_spec(w["w_ukv"].shape),
        _const_spec(w["w_vt"].shape),
        _const_spec((8, hp)),
    ]
    args = [x, mods, n1g, w["w_in"], w["q_norm_g"], w["kv_norm_g"], w["w_uq"], w["w_ukv"], w["w_vt"],
            w["gains"]]
    if rope:
        rrow, rcol = rope_tabs
        in_specs += [_const_spec(rrow.shape), _const_spec(rcol.shape)]
        args += [rrow, rcol]
    head_spec = pl.BlockSpec((1, hh, tm, hp), lambda bi, i: (bi, 0, i, 0))
    head_shape = jax.ShapeDtypeStruct((b, hh, t, hp), bf16)
    return pl.pallas_call(
        functools.partial(_qkv_body, rope=rope),
        grid=(b, t // tm),
        in_specs=in_specs,
        out_specs=[head_spec, head_spec, head_spec,
                   pl.BlockSpec((1, hh, V_T_ROWS, tm), lambda bi, i: (bi, 0, 0, i)),
                   pl.BlockSpec((1, tm, 3 * CONV_DIM), lambda bi, i: (bi, i, 0))],
        out_shape=[head_shape, head_shape, head_shape,
                   jax.ShapeDtypeStruct((b, hh, V_T_ROWS, t), bf16),
                   jax.ShapeDtypeStruct((b, t, 3 * CONV_DIM), bf16)],
        scratch_shapes=[pltpu.VMEM((tm, 2 * ALL_HEADS), f32),
                        pltpu.VMEM((tm, ALL_HEADS + hh * V_HEAD_DIM), f32),
                        pltpu.VMEM((tm, hp), f32)],
        compiler_params=_cparams(("arbitrary", "arbitrary")),
        name="qkv_rope" if rope else "qkv_ctx",
    )(*args)


def _denominator_lane(parity):
    return V_HEAD_DIM if parity == 0 else 0


def _attn_head(q, k_refs, v_refs, par, chunks):
    m = None
    acc = None
    for src, (n_chunks, tk) in enumerate(chunks):
        for c in range(n_chunks):
            ks = slice(c * tk, (c + 1) * tk)
            s = lax.dot_general(q, k_refs[src][0, par, ks, :], NT_DIMS, preferred_element_type=f32)
            m_cur = jnp.max(s, axis=1, keepdims=True)
            m_new = m_cur if m is None else jnp.maximum(m, m_cur)
            pv = _dot(jnp.exp2(s - m_new).astype(bf16), v_refs[src][0, par, ks, :])
            acc = pv if m is None else jnp.exp2(m - m_new) * acc + pv
            m = m_new
    lane = _denominator_lane(par)
    return acc / acc[:, lane:lane + 1]


def _attn_body(q_ref, *refs, chunks):
    o_ref = refs[-1]
    k_refs, v_refs = refs[0:-1:2], refs[1:-1:2]
    even = _attn_head(q_ref[0, 0], k_refs, v_refs, 0, chunks)
    odd = _attn_head(q_ref[0, 1], k_refs, v_refs, 1, chunks)
    lane = lax.broadcasted_iota(jnp.int32, even.shape, 1)
    o_ref[0] = jnp.where(lane < V_HEAD_DIM, even, odd).astype(bf16)


def _attention(q, kv_sources, tq, tk):
    b, hh, t, hp = q.shape
    pair_spec = lambda n: pl.BlockSpec((1, 2, n, hp), lambda bi, pi, i: (bi, pi, 0, 0))
    in_specs = [pl.BlockSpec((1, 2, tq, hp), lambda bi, pi, i: (bi, pi, i, 0))]
    args = [q]
    chunks = []
    for k, v in kv_sources:
        n = k.shape[2]
        step = min(tk, n)
        chunks.append((n // step, step))
        in_specs += [pair_spec(n), pair_spec(n)]
        args += [k, v]
    return pl.pallas_call(
        functools.partial(_attn_body, chunks=tuple(chunks)),
        grid=(b, hh // 2, t // tq),
        in_specs=in_specs,
        out_specs=pl.BlockSpec((1, tq, hp), lambda bi, pi, i: (bi, i, pi)),
        out_shape=jax.ShapeDtypeStruct((b, t, hh * V_HEAD_DIM), bf16),
        compiler_params=_cparams(("arbitrary", "arbitrary", "arbitrary")),
        name="attn%d" % len(kv_sources),
    )(*args)


def _attn_t_body(q_ref, *refs, chunks):
    o_ref = refs[-1]
    k_refs, vt_refs = refs[0:-1:2], refs[1:-1:2]
    outs = []
    for par in range(2):
        q = q_ref[0, par]
        acc = None
        for src, (n_chunks, tk) in enumerate(chunks):
            for c in range(n_chunks):
                ks = slice(c * tk, (c + 1) * tk)
                st = lax.dot_general(k_refs[src][0, par, ks, :], q, NT_DIMS, preferred_element_type=f32)
                part = _dot(vt_refs[src][0, par, :, ks], jnp.exp2(st).astype(bf16))
                acc = part if acc is None else acc + part
        outs.append(acc[:V_HEAD_DIM] / acc[V_HEAD_DIM:V_HEAD_DIM + 1])
    o_ref[0] = jnp.concatenate(outs, axis=0).T.astype(bf16)


def _attention_t(q, kv_sources, tq, tk):
    b, hh, t, hp = q.shape
    in_specs = [pl.BlockSpec((1, 2, tq, hp), lambda bi, pi, i: (bi, pi, i, 0))]
    args = [q]
    chunks = []
    for k, vt in kv_sources:
        n = k.shape[2]
        step = min(tk, n)
        chunks.append((n // step, step))
        in_specs += [pl.BlockSpec((1, 2, n, hp), lambda bi, pi, i: (bi, pi, 0, 0)),
                     pl.BlockSpec((1, 2, V_T_ROWS, n), lambda bi, pi, i: (bi, pi, 0, 0))]
        args += [k, vt]
    return pl.pallas_call(
        functools.partial(_attn_t_body, chunks=tuple(chunks)),
        grid=(b, hh // 2, t // tq),
        in_specs=in_specs,
        out_specs=pl.BlockSpec((1, tq, hp), lambda bi, pi, i: (bi, i, pi)),
        out_shape=jax.ShapeDtypeStruct((b, t, hh * V_HEAD_DIM), bf16),
        compiler_params=_cparams(("arbitrary", "arbitrary", "arbitrary")),
        name="attn%d_bounded" % len(kv_sources),
    )(*args)


def _gated_ffn(x, out, mod, n2g_ref, w1_ref, w3_ref, w2_ref):
    d = D_MODEL
    x1 = x + mod[:, 2 * d:3 * d] * out
    h2 = _modulated_norm(x1, mod, n2g_ref, 3 * d).astype(bf16)
    y = None
    for lo, hi in _ff_chunks(w1_ref.shape[2]):
        sl = slice(lo, hi)
        u = _dot(h2, w1_ref[0, :, sl])
        g = _dot(h2, w3_ref[0, :, sl])
        act = (u / (1.0 + jnp.exp(-u)) * g).astype(bf16)
        part = _dot(act, w2_ref[0, sl, :])
        y = part if y is None else y + part
    return x1 + mod[:, 5 * d:6 * d] * y


FFN_CHUNKS = 4


def _ff_chunks(d_ff):
    step = -(-d_ff // (FFN_CHUNKS * MXU_TILE)) * MXU_TILE
    return [(lo, min(lo + step, d_ff)) for lo in range(0, d_ff, step)]


def _mix_ffn_body(*refs, halo):
    x_ref, a_ref, pc_ref = refs[:3]
    if halo:
        pprev_ref, pnext_ref = refs[3:5]
        refs = refs[5:]
    else:
        refs = refs[3:]
    (mod_ref, convw_ref, woa_ref, woc_ref, n2g_ref, w1_ref, w3_ref, w2_ref, o_ref) = refs
    x = x_ref[0]
    mod = mod_ref[0, 0]
    tm = x.shape[0]

    out = _dot(a_ref[0], woa_ref[...])

    cd = CONV_DIM
    pc = pc_ref[0].astype(f32)
    z = pc[:, cd:2 * cd] * pc[:, 2 * cd:]
    row = lax.broadcasted_iota(jnp.int32, (tm, cd), 0)
    if halo:
        i = pl.program_id(1)
        last = pl.num_programs(1) - 1
        pp = pprev_ref[0, BF16_SUBLANES - 1:BF16_SUBLANES, :].astype(f32)
        pn = pnext_ref[0, 0:1, :].astype(f32)
        z_prev = jnp.where(i > 0, pp[:, cd:2 * cd] * pp[:, 2 * cd:], 0.0)
        z_next = jnp.where(i < last, pn[:, cd:2 * cd] * pn[:, 2 * cd:], 0.0)
    else:
        z_prev = jnp.zeros((1, cd), f32)
        z_next = jnp.zeros((1, cd), f32)
    z_up = jnp.where(row == 0, z_prev, pltpu.roll(z, 1, axis=0))
    z_dn = jnp.where(row == tm - 1, z_next, pltpu.roll(z, tm - 1, axis=0))
    cw = convw_ref[0]
    y = z_up * cw[0:1, :] + z * cw[1:2, :] + z_dn * cw[2:3, :]
    out = out + _dot((pc[:, :cd] * y).astype(bf16), woc_ref[...])

    o_ref[0] = _gated_ffn(x, out, mod, n2g_ref, w1_ref, w3_ref, w2_ref)


def _mix_ffn(x, a, pc, mods, layer, mod_row, conv_w, j, w, n2g, w1, w3, w2, tm):
    b, t, d = x.shape
    hh, hp = MLA_HEADS, HEAD_PAD
    halo = t > tm
    in_specs = [
        pl.BlockSpec((1, tm, d), lambda bi, i: (bi, i, 0)),
        pl.BlockSpec((1, tm, hh * V_HEAD_DIM), lambda bi, i: (bi, i, 0)),
        pl.BlockSpec((1, tm, 3 * CONV_DIM), lambda bi, i: (bi, i, 0)),
    ]
    args = [x, a, pc]
    if halo:
        per = tm // BF16_SUBLANES
        nblk = t // BF16_SUBLANES
        in_specs += [
            pl.BlockSpec((1, BF16_SUBLANES, 3 * CONV_DIM),
                         lambda bi, i: (bi, jnp.maximum(i * per - 1, 0), 0)),
            pl.BlockSpec((1, BF16_SUBLANES, 3 * CONV_DIM),
                         lambda bi, i: (bi, jnp.minimum((i + 1) * per, nblk - 1), 0)),
        ]
        args += [pc, pc]
    in_specs += [
        _mod_spec(layer, mod_row),
        _const_spec(conv_w.shape, j),
        _const_spec(w["w_o_attn"].shape),
        _const_spec(w["w_o_conv"].shape),
        _const_spec((1, d)),
        _const_spec(w1.shape, layer),
        _const_spec(w3.shape, layer),
        _const_spec(w2.shape, layer),
    ]
    args += [mods, conv_w, w["w_o_attn"], w["w_o_conv"], n2g, w1, w3, w2]
    return pl.pallas_call(
        functools.partial(_mix_ffn_body, halo=halo),
        grid=(b, t // tm),
        in_specs=in_specs,
        out_specs=pl.BlockSpec((1, tm, d), lambda bi, i: (bi, i, 0)),
        out_shape=jax.ShapeDtypeStruct((b, t, d), f32),
        compiler_params=_cparams(("arbitrary", "arbitrary")),
        name="mix_ffn_halo" if halo else "mix_ffn",
    )(*args)


SUBLANES = 8


def _dft_a_body(x_ref, mod_ref, n1g_ref, kw_ref, cd_ref, twc_ref, tws_ref, o_ref):
    d = D_MODEL
    gd = FOURIER_GROUP_DIM
    rows = DFT_T1 * SUBLANES
    mod = mod_ref[0, 0]
    halves = [[], []]
    for half in range(x_ref.shape[2] // SUBLANES):
        sub = slice(half * SUBLANES, (half + 1) * SUBLANES)
        x = x_ref[0, :, sub, :].reshape(rows, d)
        h = _modulated_norm(x, mod, n1g_ref, 0).astype(bf16)
        a = _dot(kw_ref[...], h).astype(bf16)
        tw_c = twc_ref[half]
        tw_s = tws_ref[half]
        re_cols, im_cols = [], []
        for g in range(FOURIER_GROUPS):
            sl = slice(g * gd, (g + 1) * gd)
            z = _dot(jnp.concatenate([a[:rows, sl], a[rows:, sl]], axis=1), cd_ref[...])
            z_re, z_im = z[:, :gd], z[:, gd:]
            c = jnp.tile(tw_c, (1, gd // LANES))
            s = jnp.tile(tw_s, (1, gd // LANES))
            re_cols.append(z_re * c + z_im * s)
            im_cols.append(z_im * c - z_re * s)
        halves[0].append(jnp.concatenate(re_cols, axis=1).reshape(DFT_T1, SUBLANES, d))
        halves[1].append(jnp.concatenate(im_cols, axis=1).reshape(DFT_T1, SUBLANES, d))
    for comp in range(2):
        o_ref[0, comp] = jnp.concatenate(halves[comp], axis=1).astype(bf16)


def _dft_a(x, mods, layer, n1g, tabs):
    b, t, d = x.shape
    t2 = t // DFT_T1
    k = BF16_SUBLANES
    rows = DFT_T1 * SUBLANES
    return pl.pallas_call(
        _dft_a_body,
        grid=(b, t2 // k),
        in_specs=[
            pl.BlockSpec((1, DFT_T1, k, d), lambda bi, j: (bi, 0, j, 0)),
            _mod_spec(layer, lambda bi: bi),
            _const_spec((1, d)),
            _const_spec((2 * rows, rows)),
            _const_spec((2 * FOURIER_GROUP_DIM, 2 * FOURIER_GROUP_DIM)),
            pl.BlockSpec((k // SUBLANES, rows, LANES), lambda bi, j: (j, 0, 0)),
            pl.BlockSpec((k // SUBLANES, rows, LANES), lambda bi, j: (j, 0, 0)),
        ],
        out_specs=pl.BlockSpec((1, 2, DFT_T1, k, d), lambda bi, j: (bi, 0, 0, j, 0)),
        out_shape=jax.ShapeDtypeStruct((b, 2, DFT_T1, t2, d), bf16),
        compiler_params=_cparams(("arbitrary", "arbitrary")),
        name="dft_a",
    )(x.reshape(b, DFT_T1, t2, d), mods, n1g, tabs["kw"], tabs["cd"], tabs["tw_cos"], tabs["tw_sin"])


def _dft_ffn_body(x_ref, z_ref, mod_ref, g_ref, wf_ref, n2g_ref, w1_ref, w3_ref, w2_ref, o_ref):
    d = D_MODEL
    tp = x_ref.shape[1]
    rows = tp * SUBLANES
    r0 = pl.multiple_of(pl.program_id(2) * tp, tp)
    cs = g_ref[pl.ds(r0, tp), :]
    t2 = z_ref.shape[3]
    f = jnp.stack([_dot(cs, z_ref[0, :, a].reshape(2 * t2, d)) for a in range(SUBLANES)], axis=0)
    f = jnp.swapaxes(f, 0, 1).reshape(rows, d).astype(bf16)
    out = _dot(f, wf_ref[0])
    x = x_ref[0].reshape(rows, d)
    res = _gated_ffn(x, out, mod_ref[0, 0], n2g_ref, w1_ref, w3_ref, w2_ref)
    o_ref[0] = res.reshape(tp, SUBLANES, d)


def _dft_ffn(x, z, mods, layer, tabs, wf, j, n2g, w1, w3, w2, tp):
    b, t, d = x.shape
    t2 = t // DFT_T1
    out = pl.pallas_call(
        _dft_ffn_body,
        grid=(b, DFT_T1 // SUBLANES, t2 // tp),
        in_specs=[
            pl.BlockSpec((1, tp, SUBLANES, d), lambda bi, jj, hh: (bi, hh, jj, 0)),
            pl.BlockSpec((1, 2, SUBLANES, t2, d), lambda bi, jj, hh: (bi, 0, jj, 0, 0)),
            pl.BlockSpec((1, 1, 1, 6 * d), lambda bi, jj, hh: (layer, bi, 0, 0)),
            _const_spec((t2, 2 * t2)),
            _const_spec(wf.shape, j),
            _const_spec((1, d)),
            _const_spec(w1.shape, layer),
            _const_spec(w3.shape, layer),
            _const_spec(w2.shape, layer),
        ],
        out_specs=pl.BlockSpec((1, tp, SUBLANES, d), lambda bi, jj, hh: (bi, hh, jj, 0)),
        out_shape=jax.ShapeDtypeStruct((b, t2, DFT_T1, d), f32),
        compiler_params=_cparams(("arbitrary", "arbitrary", "arbitrary")),
        name="dft_ffn",
    )(x.reshape(b, t2, DFT_T1, d), z, mods, tabs["g2"], wf, n2g, w1, w3, w2)
    return out.reshape(b, t, d)


def _angle(i, j, n):
    return (2.0 * math.pi / n) * ((i * j) % n).astype(f32)


def _dft_tables(t):
    t1, t2 = DFT_T1, t // DFT_T1
    i1 = jnp.arange(t1, dtype=jnp.int32)
    r = jnp.arange(t1 * SUBLANES, dtype=jnp.int32)
    ang64 = _angle(r[:, None] // SUBLANES, r[None, :] // SUBLANES, t1)
    keep = jnp.where(r[:, None] % SUBLANES == r[None, :] % SUBLANES, t1 ** -0.5, 0.0)
    kw = jnp.concatenate([jnp.cos(ang64) * keep, -jnp.sin(ang64) * keep], axis=0).astype(bf16)

    ic = jnp.arange(FOURIER_GROUP_DIM, dtype=jnp.int32)
    angc = _angle(ic[:, None], ic[None, :], FOURIER_GROUP_DIM)
    sc_ = FOURIER_GROUP_DIM ** -0.5
    cc, sn = jnp.cos(angc) * sc_, jnp.sin(angc) * sc_
    cd = jnp.concatenate([jnp.concatenate([cc, -sn], axis=1),
                          jnp.concatenate([sn, cc], axis=1)], axis=0).astype(bf16)

    i2 = jnp.arange(t2, dtype=jnp.int32)
    angt = _angle(i2[:, None], i1[None, :], t)
    angt = angt.reshape(t2 // SUBLANES, SUBLANES, t1).transpose(0, 2, 1).reshape(t2 // SUBLANES, -1)
    tw = lax.optimization_barrier((jnp.cos(angt), jnp.sin(angt)))
    tw_cos, tw_sin = (jnp.broadcast_to(v[:, :, None], v.shape + (LANES,)) for v in tw)

    ang2 = _angle(i2[:, None], i2[None, :], t2)
    s2 = t2 ** -0.5
    g2 = jnp.concatenate([jnp.cos(ang2) * s2, jnp.sin(ang2) * s2], axis=1).astype(bf16)
    return dict(kw=kw, cd=cd, tw_cos=tw_cos, tw_sin=tw_sin, g2=g2)


def _rope_tables(t):
    rows = t // GRID_W
    half = QK_ROPE_DIM // 2
    inv = 1.0 / (ROPE_BASE ** (jnp.arange(0, half, 2, dtype=f32) / half))
    ar = jnp.arange(rows, dtype=f32)[:, None] * inv
    ac = jnp.arange(GRID_W, dtype=f32)[:, None] * inv
    pad_hi = HEAD_PAD - QK_HEAD_DIM

    def place(n, first, second, at, lead):
        z = jnp.zeros((n, half), f32)
        blocks = [jnp.full((n, QK_NOPE_DIM), lead, f32)]
        blocks += [jnp.concatenate([first, second], axis=1), z] if at == 0 else \
                  [z, jnp.concatenate([first, second], axis=1)]
        blocks.append(jnp.zeros((n, pad_hi), f32))
        return jnp.concatenate(blocks, axis=1)

    rrow = jnp.stack([place(rows, jnp.cos(ar), jnp.cos(ar), 0, 0.0),
                      place(rows, -jnp.sin(ar), jnp.sin(ar), 0, 0.0)])
    rcol = jnp.stack([place(GRID_W, jnp.cos(ac), jnp.cos(ac), 1, 1.0),
                      place(GRID_W, -jnp.sin(ac), jnp.sin(ac), 1, 0.0)])
    return rrow, rcol


def _partner(a, axis):
    parts = jnp.split(a, 4, axis=axis)
    return jnp.concatenate([parts[1], parts[0], parts[3], parts[2]], axis=axis)


def _prep_even(j, w_in, q_norm_g, kv_norm_g, w_uq, w_ukv, q_gain, k_gain, w_o):
    hh, hp = MLA_HEADS, HEAD_PAD
    d = D_MODEL
    nope, hd = QK_NOPE_DIM, QK_HEAD_DIM
    wi = w_in[j]
    w_pe = wi[:, PE_COL:CONV_OFFSET]
    w_in_p = jnp.concatenate([wi[:, :PE_COL], jnp.zeros((d, nope), f32), w_pe, _partner(w_pe, 1),
                              wi[:, CONV_OFFSET:]], axis=1).astype(bf16)
    wq = w_uq[j].reshape(Q_LORA_RANK, hh, hd)
    pad_head = lambda a: jnp.pad(a, ((0, 0), (0, 0), (0, hp - hd))).reshape(Q_LORA_RANK, hh * hp)
    wq_sw = jnp.zeros_like(wq).at[:, :, nope:].set(_partner(wq[:, :, nope:], 2))
    w_uq_p = jnp.concatenate([pad_head(wq), pad_head(wq_sw)], axis=1)
    wkv = w_ukv[j].reshape(KV_LORA_RANK, hh, nope + V_HEAD_DIM)
    wk = jnp.pad(wkv[:, :, :nope], ((0, 0), (0, 0), (0, hp - nope))).reshape(KV_LORA_RANK, hh * hp)
    wv = wkv[:, :, nope:].reshape(KV_LORA_RANK, hh * V_HEAD_DIM)
    w_ukv_p = jnp.concatenate([wk, wv], axis=1).astype(bf16)
    w_vt = wv.T.astype(bf16)
    w_uq_p = w_uq_p.astype(bf16)

    def gain_rows(g):
        sw = jnp.zeros_like(g).at[nope:].set(_partner(g[nope:], 0))
        return [jnp.pad(g, (0, hp - hd)), jnp.pad(sw, (0, hp - hd))]

    zero = jnp.zeros((hp,), f32)
    gains = jnp.stack(gain_rows(q_gain[j] * (QK_SCALE * LOG2E)) + gain_rows(k_gain[j]) + [zero] * 4)
    return dict(
        w_in=w_in_p, w_uq=w_uq_p, w_ukv=w_ukv_p, w_vt=w_vt,
        q_norm_g=q_norm_g[j].reshape(1, -1), kv_norm_g=kv_norm_g[j].reshape(1, -1), gains=gains,
        w_o_attn=w_o[j][:hh * V_HEAD_DIM].astype(bf16), w_o_conv=w_o[j][hh * V_HEAD_DIM:].astype(bf16))


def _pick(n, pref):
    return pref if n % pref == 0 else n


def kernel(x, c, ctx, c_ctx, ada_w, ada_b, norm1_g, norm2_g, w_in, q_norm_g, kv_norm_g, w_uq, w_ukv,
           q_gain, k_gain, conv_w, w_o, w_fourier, ffn_w1, ffn_w3, ffn_w2):
    b, s, d = x.shape
    depth = ada_w.shape[0]
    cvec = jnp.zeros((8, d), f32).at[:b].set(c).at[b].set(c_ctx)
    mods = _ada(cvec, ada_w, ada_b).reshape(depth, 8, 1, 6 * d)
    lat_row = lambda bi: bi
    ctx_row = lambda bi: b
    w1 = ffn_w1.astype(bf16)
    w3 = ffn_w3.astype(bf16)
    w2 = ffn_w2.astype(bf16)
    wf = w_fourier.astype(bf16)
    for i in range(depth):
        last = i == depth - 1
        j = i // 2
        n1g = norm1_g[i].reshape(1, d)
        n2g = norm2_g[i].reshape(1, d)
        if i % 2 == 0:
            w = _prep_even(j, w_in, q_norm_g, kv_norm_g, w_uq, w_ukv, q_gain, k_gain, w_o)
            tm = _pick(s, 512)
            lc = ctx.shape[1]
            q_l, k_l, v_l, vt_l, pc_l = _qkv(x, mods, i, lat_row, n1g, w, _rope_tables(s), tm)
            q_c, k_c, v_c, vt_c, pc_c = _qkv(ctx, mods, i, ctx_row, n1g, w, None, lc)
            bound = (QK_SCALE * LOG2E * QK_HEAD_DIM * BF16_ROUNDING_SLACK
                     * jnp.max(jnp.abs(q_gain[j])) * jnp.max(jnp.abs(k_gain[j])))
            tq, tk = _pick(s, 512), _pick(s, 1024)
            a_l = lax.cond(
                bound < EXP2_SAFE_SCORE,
                lambda q_, kl, vl, vtl, kc, vc, vtc: _attention_t(q_, [(kl, vtl), (kc, vtc)],
                                                                  _pick(s, 2048), _pick(s, 1024)),
                lambda q_, kl, vl, vtl, kc, vc, vtc: _attention(q_, [(kl, vl), (kc, vc)], tq, tk),
                q_l, k_l, v_l, vt_l, k_c, v_c, vt_c)
            x_new = _mix_ffn(x, a_l, pc_l, mods, i, lat_row, conv_w, j, w, n2g, w1, w3, w2, tm)
            if not last:
                a_c = _attention(q_c, [(k_c, v_c)], lc, lc)
                ctx = _mix_ffn(ctx, a_c, pc_c, mods, i, ctx_row, conv_w, j, w, n2g, w1, w3, w2, lc)
            x = x_new
        else:
            tabs = _dft_tables(s)
            t2 = s // DFT_T1
            z = _dft_a(x, mods, i, n1g, tabs)
            x = _dft_ffn(x, z, mods, i, tabs, wf, j, n2g, w1, w3, w2, _pick(t2, 64))
            assert last, "odd non-final layers are not implemented"
    return x
```

```python
import functools
import math

import jax
import jax.numpy as jnp
from jax import lax
from jax.experimental import pallas as pl
from jax.experimental.pallas import tpu as pltpu

D_MODEL = 1024
GRID_W = 64
MLA_HEADS = 8
QK_NOPE_DIM = 64
QK_ROPE_DIM = 32
QK_HEAD_DIM = QK_NOPE_DIM + QK_ROPE_DIM
V_HEAD_DIM = 64
Q_LORA_RANK = 384
KV_LORA_RANK = 256
QK_SCALE = QK_HEAD_DIM ** -0.5
ROPE_BASE = 10000.0
CONV_DIM = 512
CONV_OFFSET = Q_LORA_RANK + KV_LORA_RANK + QK_ROPE_DIM
FOURIER_GROUPS = 4
FOURIER_GROUP_DIM = D_MODEL // FOURIER_GROUPS
EPS = 1e-6

LANES = 128
BF16_SUBLANES = 16
MXU_TILE = 256
VMEM_LIMIT_BYTES = 56 * 1024 * 1024

HEAD_PAD = LANES
ALL_HEADS = MLA_HEADS * HEAD_PAD
PE_COL = Q_LORA_RANK + KV_LORA_RANK
CONV_COL = PE_COL + HEAD_PAD
IN_PROJ_PAD = CONV_COL + 3 * CONV_DIM
QKV_ROW_BLOCK = 128
QKV_SUBTILES = 2
DFT_T1 = 64
LOG2E = math.log2(math.e)
EXP2_SAFE_SCORE = 64.0
BF16_ROUNDING_SLACK = 1.02
NT_DIMS = (((1,), (1,)), ((), ()))
V_T_ROWS = V_HEAD_DIM + BF16_SUBLANES

bf16 = jnp.bfloat16
f32 = jnp.float32


def _cparams(sem):
    return pltpu.CompilerParams(dimension_semantics=sem, vmem_limit_bytes=VMEM_LIMIT_BYTES)


def _const_spec(shape, lead=None):
    if lead is None:
        nd = len(shape)
        return pl.BlockSpec(shape, lambda *_: (0,) * nd, pipeline_mode=pl.Buffered(1))
    nd = len(shape) - 1
    return pl.BlockSpec((1,) + tuple(shape[1:]), lambda *_: (lead,) + (0,) * nd,
                        pipeline_mode=pl.Buffered(1))


def _mod_spec(layer, mod_row):
    return pl.BlockSpec((1, 1, 1, 6 * D_MODEL), lambda bi, i: (layer, mod_row(bi), 0, 0))


def _dot(a, b):
    return jnp.dot(a, b, preferred_element_type=f32)


def _rms_scale(x, width):
    return lax.rsqrt(jnp.sum(x * x, axis=-1, keepdims=True) * (1.0 / width) + EPS)


def _ada_body(c_ref, w_ref, b_ref, o_ref):
    c = c_ref[...]
    s = c / (1.0 + jnp.exp(-c))
    o_ref[0] = _dot(s.astype(bf16), w_ref[0].astype(bf16)) + b_ref[0]


def _ada(cvec, ada_w, ada_b):
    depth, d, n = ada_w.shape
    tn = 1536
    return pl.pallas_call(
        _ada_body,
        grid=(depth, n // tn),
        in_specs=[
            pl.BlockSpec((8, d), lambda l, j: (0, 0)),
            pl.BlockSpec((1, d, tn), lambda l, j: (l, 0, j)),
            pl.BlockSpec((1, 1, tn), lambda l, j: (l, 0, j)),
        ],
        out_specs=pl.BlockSpec((1, 8, tn), lambda l, j: (l, 0, j)),
        out_shape=jax.ShapeDtypeStruct((depth, 8, n), f32),
        compiler_params=_cparams(("arbitrary", "arbitrary")),
        name="ada",
    )(cvec, ada_w, ada_b.reshape(depth, 1, n))


def _modulated_norm(x, mod, g_ref, lo):
    d = D_MODEL
    shift = mod[:, lo:lo + d]
    scale = mod[:, lo + d:lo + 2 * d]
    gain = g_ref[...] * (1.0 + scale)
    return (x * _rms_scale(x, d)) * gain + shift


def _qkv_body(*refs, rope):
    (x_ref, mod_ref, n1g_ref, win_ref, qng_ref, kvng_ref, wuq_ref, wukv_ref, wvt_ref,
     gains_ref) = refs[:10]
    if rope:
        rrow_ref, rcol_ref = refs[10:12]
        refs = refs[12:]
    else:
        refs = refs[10:]
    q_out, k_out, v_out, vt_out, pc_out, q_scr, kv_scr, pe_scr = refs
    tm = x_ref.shape[1]
    mod = mod_ref[0, 0]
    gains = gains_ref[...]
    lane = lax.broadcasted_iota(jnp.int32, (1, HEAD_PAD), 1)
    real = (lane < QK_HEAD_DIM).astype(f32)
    v_keep = [(lane < V_HEAD_DIM).astype(f32), (lane >= V_HEAD_DIM).astype(f32)]
    ones_col = [(lane == _denominator_lane(par)).astype(f32) for par in range(2)]
    rb = min(QKV_ROW_BLOCK, tm)
    per = rb // GRID_W
    grid_row0 = pl.program_id(1) * (tm // GRID_W)
    to_rope = HEAD_PAD - QK_ROPE_DIM
    sub = max(rb, tm // QKV_SUBTILES)

    def qk_scale(t):
        return lax.rsqrt(jnp.sum(t * t * real, axis=-1, keepdims=True) * (1.0 / QK_HEAD_DIM) + EPS)

    def project(lo):
        rows = slice(lo, lo + sub)
        h = _modulated_norm(x_ref[0, rows, :], mod, n1g_ref, 0)
        p = _dot(h.astype(bf16), win_ref[...])
        pc_out[0, rows, :] = p[:, CONV_COL:].astype(bf16)
        pe_scr[rows, :] = p[:, PE_COL:CONV_COL]
        cq = p[:, :Q_LORA_RANK]
        cqn = cq * _rms_scale(cq, Q_LORA_RANK) * qng_ref[...]
        q_scr[rows, :] = _dot(cqn.astype(bf16), wuq_ref[...])
        ckv = p[:, Q_LORA_RANK:PE_COL]
        ckvn = (ckv * _rms_scale(ckv, KV_LORA_RANK) * kvng_ref[...]).astype(bf16)
        kv_scr[rows, :] = _dot(ckvn, wukv_ref[...])
        vt = lax.dot_general(wvt_ref[...], ckvn, NT_DIMS, preferred_element_type=f32)
        ones_rows = (lax.broadcasted_iota(jnp.int32, (V_T_ROWS - V_HEAD_DIM, sub), 0) == 0).astype(f32)
        for hd in range(MLA_HEADS):
            vt_h = vt[hd * V_HEAD_DIM:(hd + 1) * V_HEAD_DIM]
            vt_out[0, hd, :, rows] = jnp.concatenate([vt_h, ones_rows], axis=0).astype(bf16)

    def finish(ib):
        rows = slice(ib * rb, (ib + 1) * rb)
        if rope:
            tabs = []
            for comp in range(2):
                parts = []
                for g in range(per):
                    rr = rrow_ref[comp, pl.ds(grid_row0 + ib * per + g, 1), :]
                    parts.append(rcol_ref[comp] + rr)
                tabs.append(jnp.concatenate(parts, axis=0))
            cos, sin = tabs
            qa, qb = cos * gains[0:1], sin * gains[1:2]
            ka, kb = cos * gains[2:3], sin * gains[3:4]
        pe = pe_scr[rows, :]
        if rope:
            pe_rot = pltpu.roll(pe, to_rope, axis=1) * kb
        for hd in range(MLA_HEADS):
            sl = slice(hd * HEAD_PAD, (hd + 1) * HEAD_PAD)
            qh = q_scr[rows, sl]
            kh = kv_scr[rows, sl] + pe
            rq = qk_scale(qh)
            rk = qk_scale(kh)
            if rope:
                partner = slice(ALL_HEADS + hd * HEAD_PAD, ALL_HEADS + (hd + 1) * HEAD_PAD)
                qo = (qh * qa + q_scr[rows, partner] * qb) * rq
                ko = (kh * ka + pe_rot) * rk
            else:
                qo = qh * gains[0:1] * rq
                ko = kh * gains[2:3] * rk
            q_out[0, hd, rows, :] = qo.astype(bf16)
            k_out[0, hd, rows, :] = ko.astype(bf16)
            pair = slice(ALL_HEADS + (hd // 2) * HEAD_PAD, ALL_HEADS + (hd // 2 + 1) * HEAD_PAD)
            v_out[0, hd, rows, :] = (kv_scr[rows, pair] * v_keep[hd % 2] + ones_col[hd % 2]).astype(bf16)

    for st in range(tm // sub):
        project(st * sub)
        for ib in range(st * sub // rb, (st + 1) * sub // rb):
            finish(ib)


def _qkv(x, mods, layer, mod_row, n1g, w, rope_tabs, tm):
    b, t, d = x.shape
    hh, hp = MLA_HEADS, HEAD_PAD
    rope = rope_tabs is not None
    in_specs = [
        pl.BlockSpec((1, tm, d), lambda bi, i: (bi, i, 0)),
        _mod_spec(layer, mod_row),
        _const_spec((1, d)),
        _const_spec(w["w_in"].shape),
        _const_spec((1, Q_LORA_RANK)),
        _const_spec((1, KV_LORA_RANK)),
        _const_spec(w["w_uq"].shape),
        _const_spec(w["w_ukv"].shape),
        _const_spec(w["w_vt"].shape),
        _const_spec((8, hp)),
    ]
    args = [x, mods, n1g, w["w_in"], w["q_norm_g"], w["kv_norm_g"], w["w_uq"], w["w_ukv"], w["w_vt"],
            w["gains"]]
    if rope:
        rrow, rcol = rope_tabs
        in_specs += [_const_spec(rrow.shape), _const_spec(rcol.shape)]
        args += [rrow, rcol]
    head_spec = pl.BlockSpec((1, hh, tm, hp), lambda bi, i: (bi, 0, i, 0))
    head_shape = jax.ShapeDtypeStruct((b, hh, t, hp), bf16)
    return pl.pallas_call(
        functools.partial(_qkv_body, rope=rope),
        grid=(b, t // tm),
        in_specs=in_specs,
        out_specs=[head_spec, head_spec, head_spec,
                   pl.BlockSpec((1, hh, V_T_ROWS, tm), lambda bi, i: (bi, 0, 0, i)),
                   pl.BlockSpec((1, tm, 3 * CONV_DIM), lambda bi, i: (bi, i, 0))],
        out_shape=[head_shape, head_shape, head_shape,
                   jax.ShapeDtypeStruct((b, hh, V_T_ROWS, t), bf16),
                   jax.ShapeDtypeStruct((b, t, 3 * CONV_DIM), bf16)],
        scratch_shapes=[pltpu.VMEM((tm, 2 * ALL_HEADS), f32),
                        pltpu.VMEM((tm, ALL_HEADS + hh * V_HEAD_DIM), f32),
                        pltpu.VMEM((tm, hp), f32)],
        compiler_params=_cparams(("arbitrary", "arbitrary")),
        name="qkv_rope" if rope else "qkv_ctx",
    )(*args)


def _denominator_lane(parity):
    return V_HEAD_DIM if parity == 0 else 0


def _attn_head(q, k_refs, v_refs, par, chunks):
    m = None
    acc = None
    for src, (n_chunks, tk) in enumerate(chunks):
        for c in range(n_chunks):
            ks = slice(c * tk, (c + 1) * tk)
            s = lax.dot_general(q, k_refs[src][0, par, ks, :], NT_DIMS, preferred_element_type=f32)
            m_cur = jnp.max(s, axis=1, keepdims=True)
            m_new = m_cur if m is None else jnp.maximum(m, m_cur)
            pv = _dot(jnp.exp2(s - m_new).astype(bf16), v_refs[src][0, par, ks, :])
            acc = pv if m is None else jnp.exp2(m - m_new) * acc + pv
            m = m_new
    lane = _denominator_lane(par)
    return acc / acc[:, lane:lane + 1]


def _attn_body(q_ref, *refs, chunks):
    o_ref = refs[-1]
    k_refs, v_refs = refs[0:-1:2], refs[1:-1:2]
    even = _attn_head(q_ref[0, 0], k_refs, v_refs, 0, chunks)
    odd = _attn_head(q_ref[0, 1], k_refs, v_refs, 1, chunks)
    lane = lax.broadcasted_iota(jnp.int32, even.shape, 1)
    o_ref[0] = jnp.where(lane < V_HEAD_DIM, even, odd).astype(bf16)


def _attention(q, kv_sources, tq, tk):
    b, hh, t, hp = q.shape
    pair_spec = lambda n: pl.BlockSpec((1, 2, n, hp), lambda bi, pi, i: (bi, pi, 0, 0))
    in_specs = [pl.BlockSpec((1, 2, tq, hp), lambda bi, pi, i: (bi, pi, i, 0))]
    args = [q]
    chunks = []
    for k, v in kv_sources:
        n = k.shape[2]
        step = min(tk, n)
        chunks.append((n // step, step))
        in_specs += [pair_spec(n), pair_spec(n)]
        args += [k, v]
    return pl.pallas_call(
        functools.partial(_attn_body, chunks=tuple(chunks)),
        grid=(b, hh // 2, t // tq),
        in_specs=in_specs,
        out_specs=pl.BlockSpec((1, tq, hp), lambda bi, pi, i: (bi, i, pi)),
        out_shape=jax.ShapeDtypeStruct((b, t, hh * V_HEAD_DIM), bf16),
        compiler_params=_cparams(("arbitrary", "arbitrary", "arbitrary")),
        name="attn%d" % len(kv_sources),
    )(*args)


def _attn_t_body(q_ref, *refs, chunks):
    o_ref = refs[-1]
    k_refs, vt_refs = refs[0:-1:2], refs[1:-1:2]
    outs = []
    for par in range(2):
        q = q_ref[0, par]
        acc = None
        for src, (n_chunks, tk) in enumerate(chunks):
            for c in range(n_chunks):
                ks = slice(c * tk, (c + 1) * tk)
                st = lax.dot_general(k_refs[src][0, par, ks, :], q, NT_DIMS, preferred_element_type=f32)
                part = _dot(vt_refs[src][0, par, :, ks], jnp.exp2(st).astype(bf16))
                acc = part if acc is None else acc + part
        outs.append(acc[:V_HEAD_DIM] / acc[V_HEAD_DIM:V_HEAD_DIM + 1])
    o_ref[0] = jnp.concatenate(outs, axis=0).T.astype(bf16)


def _attention_t(q, kv_sources, tq, tk):
    b, hh, t, hp = q.shape
    in_specs = [pl.BlockSpec((1, 2, tq, hp), lambda bi, pi, i: (bi, pi, i, 0))]
    args = [q]
    chunks = []
    for k, vt in kv_sources:
        n = k.shape[2]
        step = min(tk, n)
        chunks.append((n // step, step))
        in_specs += [pl.BlockSpec((1, 2, n, hp), lambda bi, pi, i: (bi, pi, 0, 0)),
                     pl.BlockSpec((1, 2, V_T_ROWS, n), lambda bi, pi, i: (bi, pi, 0, 0))]
        args += [k, vt]
    return pl.pallas_call(
        functools.partial(_attn_t_body, chunks=tuple(chunks)),
        grid=(b, hh // 2, t // tq),
        in_specs=in_specs,
        out_specs=pl.BlockSpec((1, tq, hp), lambda bi, pi, i: (bi, i, pi)),
        out_shape=jax.ShapeDtypeStruct((b, t, hh * V_HEAD_DIM), bf16),
        compiler_params=_cparams(("arbitrary", "arbitrary", "arbitrary")),
        name="attn%d_bounded" % len(kv_sources),
    )(*args)


def _gated_ffn(x, out, mod, n2g_ref, w1_ref, w3_ref, w2_ref):
    d = D_MODEL
    x1 = x + mod[:, 2 * d:3 * d] * out
    h2 = _modulated_norm(x1, mod, n2g_ref, 3 * d).astype(bf16)
    y = None
    for lo, hi in _ff_chunks(w1_ref.shape[2]):
        sl = slice(lo, hi)
        u = _dot(h2, w1_ref[0, :, sl])
        g = _dot(h2, w3_ref[0, :, sl])
        act = (u / (1.0 + jnp.exp(-u)) * g).astype(bf16)
        part = _dot(act, w2_ref[0, sl, :])
        y = part if y is None else y + part
    return x1 + mod[:, 5 * d:6 * d] * y


FFN_CHUNKS = 4


def _ff_chunks(d_ff):
    step = -(-d_ff // (FFN_CHUNKS * MXU_TILE)) * MXU_TILE
    return [(lo, min(lo + step, d_ff)) for lo in range(0, d_ff, step)]


def _mix_ffn_body(*refs, halo):
    x_ref, a_ref, pc_ref = refs[:3]
    if halo:
        pprev_ref, pnext_ref = refs[3:5]
        refs = refs[5:]
    else:
        refs = refs[3:]
    (mod_ref, convw_ref, woa_ref, woc_ref, n2g_ref, w1_ref, w3_ref, w2_ref, o_ref) = refs
    x = x_ref[0]
    mod = mod_ref[0, 0]
    tm = x.shape[0]

    out = _dot(a_ref[0], woa_ref[...])

    cd = CONV_DIM
    pc = pc_ref[0].astype(f32)
    z = pc[:, cd:2 * cd] * pc[:, 2 * cd:]
    row = lax.broadcasted_iota(jnp.int32, (tm, cd), 0)
    if halo:
        i = pl.program_id(1)
        last = pl.num_programs(1) - 1
        pp = pprev_ref[0, BF16_SUBLANES - 1:BF16_SUBLANES, :].astype(f32)
        pn = pnext_ref[0, 0:1, :].astype(f32)
        z_prev = jnp.where(i > 0, pp[:, cd:2 * cd] * pp[:, 2 * cd:], 0.0)
        z_next = jnp.where(i < last, pn[:, cd:2 * cd] * pn[:, 2 * cd:], 0.0)
    else:
        z_prev = jnp.zeros((1, cd), f32)
        z_next = jnp.zeros((1, cd), f32)
    z_up = jnp.where(row == 0, z_prev, pltpu.roll(z, 1, axis=0))
    z_dn = jnp.where(row == tm - 1, z_next, pltpu.roll(z, tm - 1, axis=0))
    cw = convw_ref[0]
    y = z_up * cw[0:1, :] + z * cw[1:2, :] + z_dn * cw[2:3, :]
    out = out + _dot((pc[:, :cd] * y).astype(bf16), woc_ref[...])

    o_ref[0] = _gated_ffn(x, out, mod, n2g_ref, w1_ref, w3_ref, w2_ref)


def _mix_ffn(x, a, pc, mods, layer, mod_row, conv_w, j, w, n2g, w1, w3, w2, tm):
    b, t, d = x.shape
    hh, hp = MLA_HEADS, HEAD_PAD
    halo = t > tm
    in_specs = [
        pl.BlockSpec((1, tm, d), lambda bi, i: (bi, i, 0)),
        pl.BlockSpec((1, tm, hh * V_HEAD_DIM), lambda bi, i: (bi, i, 0)),
        pl.BlockSpec((1, tm, 3 * CONV_DIM), lambda bi, i: (bi, i, 0)),
    ]
    args = [x, a, pc]
    if halo:
        per = tm // BF16_SUBLANES
        nblk = t // BF16_SUBLANES
        in_specs += [
            pl.BlockSpec((1, BF16_SUBLANES, 3 * CONV_DIM),
                         lambda bi, i: (bi, jnp.maximum(i * per - 1, 0), 0)),
            pl.BlockSpec((1, BF16_SUBLANES, 3 * CONV_DIM),
                         lambda bi, i: (bi, jnp.minimum((i + 1) * per, nblk - 1), 0)),
        ]
        args += [pc, pc]
    in_specs += [
        _mod_spec(layer, mod_row),
        _const_spec(conv_w.shape, j),
        _const_spec(w["w_o_attn"].shape),
        _const_spec(w["w_o_conv"].shape),
        _const_spec((1, d)),
        _const_spec(w1.shape, layer),
        _const_spec(w3.shape, layer),
        _const_spec(w2.shape, layer),
    ]
    args += [mods, conv_w, w["w_o_attn"], w["w_o_conv"], n2g, w1, w3, w2]
    return pl.pallas_call(
        functools.partial(_mix_ffn_body, halo=halo),
        grid=(b, t // tm),
        in_specs=in_specs,
        out_specs=pl.BlockSpec((1, tm, d), lambda bi, i: (bi, i, 0)),
        out_shape=jax.ShapeDtypeStruct((b, t, d), f32),
        compiler_params=_cparams(("arbitrary", "arbitrary")),
        name="mix_ffn_halo" if halo else "mix_ffn",
    )(*args)


SUBLANES = 8


def _dft_a_body(x_ref, mod_ref, n1g_ref, kw_ref, cd_ref, twc_ref, tws_ref, o_ref):
    d = D_MODEL
    gd = FOURIER_GROUP_DIM
    rows = DFT_T1 * SUBLANES
    mod = mod_ref[0, 0]
    halves = [[], []]
    for half in range(x_ref.shape[2] // SUBLANES):
        sub = slice(half * SUBLANES, (half + 1) * SUBLANES)
        x = x_ref[0, :, sub, :].reshape(rows, d)
        h = _modulated_norm(x, mod, n1g_ref, 0).astype(bf16)
        a = _dot(kw_ref[...], h).astype(bf16)
        tw_c = twc_ref[half]
        tw_s = tws_ref[half]
        re_cols, im_cols = [], []
        for g in range(FOURIER_GROUPS):
            sl = slice(g * gd, (g + 1) * gd)
            z = _dot(jnp.concatenate([a[:rows, sl], a[rows:, sl]], axis=1), cd_ref[...])
            z_re, z_im = z[:, :gd], z[:, gd:]
            c = jnp.tile(tw_c, (1, gd // LANES))
            s = jnp.tile(tw_s, (1, gd // LANES))
            re_cols.append(z_re * c + z_im * s)
            im_cols.append(z_im * c - z_re * s)
        halves[0].append(jnp.concatenate(re_cols, axis=1).reshape(DFT_T1, SUBLANES, d))
        halves[1].append(jnp.concatenate(im_cols, axis=1).reshape(DFT_T1, SUBLANES, d))
    for comp in range(2):
        o_ref[0, comp] = jnp.concatenate(halves[comp], axis=1).astype(bf16)


def _dft_a(x, mods, layer, n1g, tabs):
    b, t, d = x.shape
    t2 = t // DFT_T1
    k = BF16_SUBLANES
    rows = DFT_T1 * SUBLANES
    return pl.pallas_call(
        _dft_a_body,
        grid=(b, t2 // k),
        in_specs=[
            pl.BlockSpec((1, DFT_T1, k, d), lambda bi, j: (bi, 0, j, 0)),
            _mod_spec(layer, lambda bi: bi),
            _const_spec((1, d)),
            _const_spec((2 * rows, rows)),
            _const_spec((2 * FOURIER_GROUP_DIM, 2 * FOURIER_GROUP_DIM)),
            pl.BlockSpec((k // SUBLANES, rows, LANES), lambda bi, j: (j, 0, 0)),
            pl.BlockSpec((k // SUBLANES, rows, LANES), lambda bi, j: (j, 0, 0)),
        ],
        out_specs=pl.BlockSpec((1, 2, DFT_T1, k, d), lambda bi, j: (bi, 0, 0, j, 0)),
        out_shape=jax.ShapeDtypeStruct((b, 2, DFT_T1, t2, d), bf16),
        compiler_params=_cparams(("arbitrary", "arbitrary")),
        name="dft_a",
    )(x.reshape(b, DFT_T1, t2, d), mods, n1g, tabs["kw"], tabs["cd"], tabs["tw_cos"], tabs["tw_sin"])


def _dft_ffn_body(x_ref, z_ref, mod_ref, g_ref, wf_ref, n2g_ref, w1_ref, w3_ref, w2_ref, o_ref):
    d = D_MODEL
    tp = x_ref.shape[1]
    rows = tp * SUBLANES
    r0 = pl.multiple_of(pl.program_id(2) * tp, tp)
    cs = g_ref[pl.ds(r0, tp), :]
    t2 = z_ref.shape[3]
    f = jnp.stack([_dot(cs, z_ref[0, :, a].reshape(2 * t2, d)) for a in range(SUBLANES)], axis=0)
    f = jnp.swapaxes(f, 0, 1).reshape(rows, d).astype(bf16)
    out = _dot(f, wf_ref[0])
    x = x_ref[0].reshape(rows, d)
    res = _gated_ffn(x, out, mod_ref[0, 0], n2g_ref, w1_ref, w3_ref, w2_ref)
    o_ref[0] = res.reshape(tp, SUBLANES, d)


def _dft_ffn(x, z, mods, layer, tabs, wf, j, n2g, w1, w3, w2, tp):
    b, t, d = x.shape
    t2 = t // DFT_T1
    out = pl.pallas_call(
        _dft_ffn_body,
        grid=(b, DFT_T1 // SUBLANES, t2 // tp),
        in_specs=[
            pl.BlockSpec((1, tp, SUBLANES, d), lambda bi, jj, hh: (bi, hh, jj, 0)),
            pl.BlockSpec((1, 2, SUBLANES, t2, d), lambda bi, jj, hh: (bi, 0, jj, 0, 0)),
            pl.BlockSpec((1, 1, 1, 6 * d), lambda bi, jj, hh: (layer, bi, 0, 0)),
            _const_spec((t2, 2 * t2)),
            _const_spec(wf.shape, j),
            _const_spec((1, d)),
            _const_spec(w1.shape, layer),
            _const_spec(w3.shape, layer),
            _const_spec(w2.shape, layer),
        ],
        out_specs=pl.BlockSpec((1, tp, SUBLANES, d), lambda bi, jj, hh: (bi, hh, jj, 0)),
        out_shape=jax.ShapeDtypeStruct((b, t2, DFT_T1, d), f32),
        compiler_params=_cparams(("arbitrary", "arbitrary", "arbitrary")),
        name="dft_ffn",
    )(x.reshape(b, t2, DFT_T1, d), z, mods, tabs["g2"], wf, n2g, w1, w3, w2)
    return out.reshape(b, t, d)


def _angle(i, j, n):
    return (2.0 * math.pi / n) * ((i * j) % n).astype(f32)


def _dft_tables(t):
    t1, t2 = DFT_T1, t // DFT_T1
    i1 = jnp.arange(t1, dtype=jnp.int32)
    r = jnp.arange(t1 * SUBLANES, dtype=jnp.int32)
    ang64 = _angle(r[:, None] // SUBLANES, r[None, :] // SUBLANES, t1)
    keep = jnp.where(r[:, None] % SUBLANES == r[None, :] % SUBLANES, t1 ** -0.5, 0.0)
    kw = jnp.concatenate([jnp.cos(ang64) * keep, -jnp.sin(ang64) * keep], axis=0).astype(bf16)

    ic = jnp.arange(FOURIER_GROUP_DIM, dtype=jnp.int32)
    angc = _angle(ic[:, None], ic[None, :], FOURIER_GROUP_DIM)
    sc_ = FOURIER_GROUP_DIM ** -0.5
    cc, sn = jnp.cos(angc) * sc_, jnp.sin(angc) * sc_
    cd = jnp.concatenate([jnp.concatenate([cc, -sn], axis=1),
                          jnp.concatenate([sn, cc], axis=1)], axis=0).astype(bf16)

    i2 = jnp.arange(t2, dtype=jnp.int32)
    angt = _angle(i2[:, None], i1[None, :], t)
    angt = angt.reshape(t2 // SUBLANES, SUBLANES, t1).transpose(0, 2, 1).reshape(t2 // SUBLANES, -1)
    tw = lax.optimization_barrier((jnp.cos(angt), jnp.sin(angt)))
    tw_cos, tw_sin = (jnp.broadcast_to(v[:, :, None], v.shape + (LANES,)) for v in tw)

    ang2 = _angle(i2[:, None], i2[None, :], t2)
    s2 = t2 ** -0.5
    g2 = jnp.concatenate([jnp.cos(ang2) * s2, jnp.sin(ang2) * s2], axis=1).astype(bf16)
    return dict(kw=kw, cd=cd, tw_cos=tw_cos, tw_sin=tw_sin, g2=g2)


def _rope_tables(t):
    rows = t // GRID_W
    half = QK_ROPE_DIM // 2
    inv = 1.0 / (ROPE_BASE ** (jnp.arange(0, half, 2, dtype=f32) / half))
    ar = jnp.arange(rows, dtype=f32)[:, None] * inv
    ac = jnp.arange(GRID_W, dtype=f32)[:, None] * inv
    pad_hi = HEAD_PAD - QK_HEAD_DIM

    def place(n, first, second, at, lead):
        z = jnp.zeros((n, half), f32)
        blocks = [jnp.full((n, QK_NOPE_DIM), lead, f32)]
        blocks += [jnp.concatenate([first, second], axis=1), z] if at == 0 else \
                  [z, jnp.concatenate([first, second], axis=1)]
        blocks.append(jnp.zeros((n, pad_hi), f32))
        return jnp.concatenate(blocks, axis=1)

    rrow = jnp.stack([place(rows, jnp.cos(ar), jnp.cos(ar), 0, 0.0),
                      place(rows, -jnp.sin(ar), jnp.sin(ar), 0, 0.0)])
    rcol = jnp.stack([place(GRID_W, jnp.cos(ac), jnp.cos(ac), 1, 1.0),
                      place(GRID_W, -jnp.sin(ac), jnp.sin(ac), 1, 0.0)])
    return rrow, rcol


def _partner(a, axis):
    parts = jnp.split(a, 4, axis=axis)
    return jnp.concatenate([parts[1], parts[0], parts[3], parts[2]], axis=axis)


def _prep_even(j, w_in, q_norm_g, kv_norm_g, w_uq, w_ukv, q_gain, k_gain, w_o):
    hh, hp = MLA_HEADS, HEAD_PAD
    d = D_MODEL
    nope, hd = QK_NOPE_DIM, QK_HEAD_DIM
    wi = w_in[j]
    w_pe = wi[:, PE_COL:CONV_OFFSET]
    w_in_p = jnp.concatenate([wi[:, :PE_COL], jnp.zeros((d, nope), f32), w_pe, _partner(w_pe, 1),
                              wi[:, CONV_OFFSET:]], axis=1).astype(bf16)
    wq = w_uq[j].reshape(Q_LORA_RANK, hh, hd)
    pad_head = lambda a: jnp.pad(a, ((0, 0), (0, 0), (0, hp - hd))).reshape(Q_LORA_RANK, hh * hp)
    wq_sw = jnp.zeros_like(wq).at[:, :, nope:].set(_partner(wq[:, :, nope:], 2))
    w_uq_p = jnp.concatenate([pad_head(wq), pad_head(wq_sw)], axis=1)
    wkv = w_ukv[j].reshape(KV_LORA_RANK, hh, nope + V_HEAD_DIM)
    wk = jnp.pad(wkv[:, :, :nope], ((0, 0), (0, 0), (0, hp - nope))).reshape(KV_LORA_RANK, hh * hp)
    wv = wkv[:, :, nope:].reshape(KV_LORA_RANK, hh * V_HEAD_DIM)
    w_ukv_p = jnp.concatenate([wk, wv], axis=1).astype(bf16)
    w_vt = wv.T.astype(bf16)
    w_uq_p = w_uq_p.astype(bf16)

    def gain_rows(g):
        sw = jnp.zeros_like(g).at[nope:].set(_partner(g[nope:], 0))
        return [jnp.pad(g, (0, hp - hd)), jnp.pad(sw, (0, hp - hd))]

    zero = jnp.zeros((hp,), f32)
    gains = jnp.stack(gain_rows(q_gain[j] * (QK_SCALE * LOG2E)) + gain_rows(k_gain[j]) + [zero] * 4)
    return dict(
        w_in=w_in_p, w_uq=w_uq_p, w_ukv=w_ukv_p, w_vt=w_vt,
        q_norm_g=q_norm_g[j].reshape(1, -1), kv_norm_g=kv_norm_g[j].reshape(1, -1), gains=gains,
        w_o_attn=w_o[j][:hh * V_HEAD_DIM].astype(bf16), w_o_conv=w_o[j][hh * V_HEAD_DIM:].astype(bf16))


def _pick(n, pref):
    return pref if n % pref == 0 else n


def kernel(x, c, ctx, c_ctx, ada_w, ada_b, norm1_g, norm2_g, w_in, q_norm_g, kv_norm_g, w_uq, w_ukv,
           q_gain, k_gain, conv_w, w_o, w_fourier, ffn_w1, ffn_w3, ffn_w2):
    b, s, d = x.shape
    depth = ada_w.shape[0]
    cvec = jnp.zeros((8, d), f32).at[:b].set(c).at[b].set(c_ctx)
    mods = _ada(cvec, ada_w, ada_b).reshape(depth, 8, 1, 6 * d)
    lat_row = lambda bi: bi
    ctx_row = lambda bi: b
    w1 = ffn_w1.astype(bf16)
    w3 = ffn_w3.astype(bf16)
    w2 = ffn_w2.astype(bf16)
    wf = w_fourier.astype(bf16)
    for i in range(depth):
        last = i == depth - 1
        j = i // 2
        n1g = norm1_g[i].reshape(1, d)
        n2g = norm2_g[i].reshape(1, d)
        if i % 2 == 0:
            w = _prep_even(j, w_in, q_norm_g, kv_norm_g, w_uq, w_ukv, q_gain, k_gain, w_o)
            tm = _pick(s, 512)
            lc = ctx.shape[1]
            q_l, k_l, v_l, vt_l, pc_l = _qkv(x, mods, i, lat_row, n1g, w, _rope_tables(s), tm)
            q_c, k_c, v_c, vt_c, pc_c = _qkv(ctx, mods, i, ctx_row, n1g, w, None, lc)
            bound = (QK_SCALE * LOG2E * QK_HEAD_DIM * BF16_ROUNDING_SLACK
                     * jnp.max(jnp.abs(q_gain[j])) * jnp.max(jnp.abs(k_gain[j])))
            tq, tk = _pick(s, 512), _pick(s, 1024)
            a_l = lax.cond(
                bound < EXP2_SAFE_SCORE,
                lambda q_, kl, vl, vtl, kc, vc, vtc: _attention_t(q_, [(kl, vtl), (kc, vtc)],
                                                                  _pick(s, 2048), _pick(s, 1024)),
                lambda q_, kl, vl, vtl, kc, vc, vtc: _attention(q_, [(kl, vl), (kc, vc)], tq, tk),
                q_l, k_l, v_l, vt_l, k_c, v_c, vt_c)
            x_new = _mix_ffn(x, a_l, pc_l, mods, i, lat_row, conv_w, j, w, n2g, w1, w3, w2, tm)
            if not last:
                a_c = _attention(q_c, [(k_c, v_c)], lc, lc)
                ctx = _mix_ffn(ctx, a_c, pc_c, mods, i, ctx_row, conv_w, j, w, n2g, w1, w3, w2, lc)
            x = x_new
        else:
            tabs = _dft_tables(s)
            t2 = s // DFT_T1
            z = _dft_a(x, mods, i, n1g, tabs)
            x = _dft_ffn(x, z, mods, i, tabs, wf, j, n2g, w1, w3, w2, _pick(t2, 64))
            assert last, "odd non-final layers are not implemented"
    return x
```

```python
import functools
import math

import jax
import jax.numpy as jnp
from jax import lax
from jax.experimental import pallas as pl
from jax.experimental.pallas import tpu as pltpu

D_MODEL = 1024
GRID_W = 64
MLA_HEADS = 8
QK_NOPE_DIM = 64
QK_ROPE_DIM = 32
QK_HEAD_DIM = QK_NOPE_DIM + QK_ROPE_DIM
V_HEAD_DIM = 64
Q_LORA_RANK = 384
KV_LORA_RANK = 256
QK_SCALE = QK_HEAD_DIM ** -0.5
ROPE_BASE = 10000.0
CONV_DIM = 512
CONV_OFFSET = Q_LORA_RANK + KV_LORA_RANK + QK_ROPE_DIM
FOURIER_GROUPS = 4
FOURIER_GROUP_DIM = D_MODEL // FOURIER_GROUPS
EPS = 1e-6

LANES = 128
BF16_SUBLANES = 16
MXU_TILE = 256
VMEM_LIMIT_BYTES = 56 * 1024 * 1024

HEAD_PAD = LANES
ALL_HEADS = MLA_HEADS * HEAD_PAD
PE_COL = Q_LORA_RANK + KV_LORA_RANK
CONV_COL = PE_COL + HEAD_PAD
IN_PROJ_PAD = CONV_COL + 3 * CONV_DIM
QKV_ROW_BLOCK = 128
QKV_SUB_ROWS = 256
DFT_T1 = 64
LOG2E = math.log2(math.e)
EXP2_SAFE_SCORE = 64.0
BF16_ROUNDING_SLACK = 1.02
NT_DIMS = (((1,), (1,)), ((), ()))
V_T_ROWS = V_HEAD_DIM + BF16_SUBLANES

bf16 = jnp.bfloat16
f32 = jnp.float32


def _cparams(sem):
    return pltpu.CompilerParams(dimension_semantics=sem, vmem_limit_bytes=VMEM_LIMIT_BYTES)


def _const_spec(shape, lead=None):
    if lead is None:
        nd = len(shape)
        return pl.BlockSpec(shape, lambda *_: (0,) * nd, pipeline_mode=pl.Buffered(1))
    nd = len(shape) - 1
    return pl.BlockSpec((1,) + tuple(shape[1:]), lambda *_: (lead,) + (0,) * nd,
                        pipeline_mode=pl.Buffered(1))


def _mod_spec(layer, mod_row):
    return pl.BlockSpec((1, 1, 1, 6 * D_MODEL), lambda bi, i: (layer, mod_row(bi), 0, 0))


def _dot(a, b):
    return jnp.dot(a, b, preferred_element_type=f32)


def _rms_scale(x, width):
    return lax.rsqrt(jnp.sum(x * x, axis=-1, keepdims=True) * (1.0 / width) + EPS)


def _ada_body(c_ref, w_ref, b_ref, o_ref):
    c = c_ref[...]
    s = c / (1.0 + jnp.exp(-c))
    o_ref[0] = _dot(s.astype(bf16), w_ref[0].astype(bf16)) + b_ref[0]


def _ada(cvec, ada_w, ada_b):
    depth, d, n = ada_w.shape
    tn = 1536
    return pl.pallas_call(
        _ada_body,
        grid=(depth, n // tn),
        in_specs=[
            pl.BlockSpec((8, d), lambda l, j: (0, 0)),
            pl.BlockSpec((1, d, tn), lambda l, j: (l, 0, j)),
            pl.BlockSpec((1, 1, tn), lambda l, j: (l, 0, j)),
        ],
        out_specs=pl.BlockSpec((1, 8, tn), lambda l, j: (l, 0, j)),
        out_shape=jax.ShapeDtypeStruct((depth, 8, n), f32),
        compiler_params=_cparams(("arbitrary", "arbitrary")),
        name="ada",
    )(cvec, ada_w, ada_b.reshape(depth, 1, n))


def _modulated_norm(x, mod, g_ref, lo):
    d = D_MODEL
    shift = mod[:, lo:lo + d]
    scale = mod[:, lo + d:lo + 2 * d]
    gain = g_ref[...] * (1.0 + scale)
    return (x * _rms_scale(x, d)) * gain + shift


def _qkv_body(*refs, rope):
    (x_ref, mod_ref, n1g_ref, win_ref, qng_ref, kvng_ref, wuq_ref, wukv_ref, wvt_ref,
     gains_ref) = refs[:10]
    if rope:
        rrow_ref, rcol_ref = refs[10:12]
        refs = refs[12:]
    else:
        refs = refs[10:]
    q_out, k_out, vt_out, pc_out, q_scr, kv_scr, pe_scr = refs
    tm = x_ref.shape[1]
    mod = mod_ref[0, 0]
    gains = gains_ref[...]
    lane = lax.broadcasted_iota(jnp.int32, (1, HEAD_PAD), 1)
    real = (lane < QK_HEAD_DIM).astype(f32)
    rb = min(QKV_ROW_BLOCK, tm)
    per = rb // GRID_W
    grid_row0 = pl.program_id(1) * (tm // GRID_W)
    to_rope = HEAD_PAD - QK_ROPE_DIM
    sub = min(QKV_SUB_ROWS, tm)

    def qk_scale(t):
        return lax.rsqrt(jnp.sum(t * t * real, axis=-1, keepdims=True) * (1.0 / QK_HEAD_DIM) + EPS)

    def project(lo):
        rows = slice(lo, lo + sub)
        h = _modulated_norm(x_ref[0, rows, :], mod, n1g_ref, 0)
        p = _dot(h.astype(bf16), win_ref[...])
        pc_out[0, rows, :] = p[:, CONV_COL:].astype(bf16)
        pe_scr[rows, :] = p[:, PE_COL:CONV_COL]
        cq = p[:, :Q_LORA_RANK]
        cqn = cq * _rms_scale(cq, Q_LORA_RANK) * qng_ref[...]
        q_scr[rows, :] = _dot(cqn.astype(bf16), wuq_ref[...])
        ckv = p[:, Q_LORA_RANK:PE_COL]
        ckvn = (ckv * _rms_scale(ckv, KV_LORA_RANK) * kvng_ref[...]).astype(bf16)
        kv_scr[rows, :] = _dot(ckvn, wukv_ref[...])
        vt = lax.dot_general(wvt_ref[...], ckvn, NT_DIMS, preferred_element_type=f32)
        ones_rows = (lax.broadcasted_iota(jnp.int32, (V_T_ROWS - V_HEAD_DIM, sub), 0) == 0).astype(f32)
        for hd in range(MLA_HEADS):
            vt_h = vt[hd * V_HEAD_DIM:(hd + 1) * V_HEAD_DIM]
            vt_out[0, hd, :, rows] = jnp.concatenate([vt_h, ones_rows], axis=0).astype(bf16)

    def finish(ib):
        rows = slice(ib * rb, (ib + 1) * rb)
        if rope:
            tabs = []
            for comp in range(2):
                parts = []
                for g in range(per):
                    rr = rrow_ref[comp, pl.ds(grid_row0 + ib * per + g, 1), :]
                    parts.append(rcol_ref[comp] + rr)
                tabs.append(jnp.concatenate(parts, axis=0))
            cos, sin = tabs
            qa, qb = cos * gains[0:1], sin * gains[1:2]
            ka, kb = cos * gains[2:3], sin * gains[3:4]
        pe = pe_scr[rows, :]
        if rope:
            pe_rot = pltpu.roll(pe, to_rope, axis=1) * kb
        for hd in range(MLA_HEADS):
            sl = slice(hd * HEAD_PAD, (hd + 1) * HEAD_PAD)
            qh = q_scr[rows, sl]
            kh = kv_scr[rows, sl] + pe
            rq = qk_scale(qh)
            rk = qk_scale(kh)
            if rope:
                partner = slice(ALL_HEADS + hd * HEAD_PAD, ALL_HEADS + (hd + 1) * HEAD_PAD)
                qo = (qh * qa + q_scr[rows, partner] * qb) * rq
                ko = (kh * ka + pe_rot) * rk
            else:
                qo = qh * gains[0:1] * rq
                ko = kh * gains[2:3] * rk
            q_out[0, hd, rows, :] = qo.astype(bf16)
            k_out[0, hd, rows, :] = ko.astype(bf16)

    for st in range(tm // sub):
        project(st * sub)
        for ib in range(st * sub // rb, (st + 1) * sub // rb):
            finish(ib)


def _qkv(x, mods, layer, mod_row, n1g, w, rope_tabs, tm):
    b, t, d = x.shape
    hh, hp = MLA_HEADS, HEAD_PAD
    rope = rope_tabs is not None
    in_specs = [
        pl.BlockSpec((1, tm, d), lambda bi, i: (bi, i, 0)),
        _mod_spec(layer, mod_row),
        _const_spec((1, d)),
        _const_spec(w["w_in"].shape),
        _const_spec((1, Q_LORA_RANK)),
        _const_spec((1, KV_LORA_RANK)),
        _const_spec(w["w_uq"].shape),
        _const_spec(w["w_ukv"].shape),
        _const_spec(w["w_vt"].shape),
        _const_spec((8, hp)),
    ]
    args = [x, mods, n1g, w["w_in"], w["q_norm_g"], w["kv_norm_g"], w["w_uq"], w["w_ukv"], w["w_vt"],
            w["gains"]]
    if rope:
        rrow, rcol = rope_tabs
        in_specs += [_const_spec(rrow.shape), _const_spec(rcol.shape)]
        args += [rrow, rcol]
    head_spec = pl.BlockSpec((1, hh, tm, hp), lambda bi, i: (bi, 0, i, 0))
    head_shape = jax.ShapeDtypeStruct((b, hh, t, hp), bf16)
    return pl.pallas_call(
        functools.partial(_qkv_body, rope=rope),
        grid=(b, t // tm),
        in_specs=in_specs,
        out_specs=[head_spec, head_spec,
                   pl.BlockSpec((1, hh, V_T_ROWS, tm), lambda bi, i: (bi, 0, 0, i)),
                   pl.BlockSpec((1, tm, 3 * CONV_DIM), lambda bi, i: (bi, i, 0))],
        out_shape=[head_shape, head_shape,
                   jax.ShapeDtypeStruct((b, hh, V_T_ROWS, t), bf16),
                   jax.ShapeDtypeStruct((b, t, 3 * CONV_DIM), bf16)],
        scratch_shapes=[pltpu.VMEM((tm, 2 * ALL_HEADS), f32),
                        pltpu.VMEM((tm, ALL_HEADS), f32),
                        pltpu.VMEM((tm, hp), f32)],
        compiler_params=_cparams(("arbitrary", "arbitrary")),
        name="qkv_rope" if rope else "qkv_ctx",
    )(*args)


def _attn_body(q_ref, *refs, chunks, online):
    o_ref = refs[-1]
    k_refs, vt_refs = refs[0:-1:2], refs[1:-1:2]
    outs = []
    for par in range(2):
        q = q_ref[0, par]
        m = None
        acc = None
        for src, (n_chunks, tk) in enumerate(chunks):
            for c in range(n_chunks):
                ks = slice(c * tk, (c + 1) * tk)
                st = lax.dot_general(k_refs[src][0, par, ks, :], q, NT_DIMS, preferred_element_type=f32)
                if online:
                    m_cur = jnp.max(st, axis=0, keepdims=True)
                    m_new = m_cur if m is None else jnp.maximum(m, m_cur)
                    part = _dot(vt_refs[src][0, par, :, ks], jnp.exp2(st - m_new).astype(bf16))
                    acc = part if m is None else jnp.exp2(m - m_new) * acc + part
                    m = m_new
                else:
                    part = _dot(vt_refs[src][0, par, :, ks], jnp.exp2(st).astype(bf16))
                    acc = part if acc is None else acc + part
        outs.append(acc[:V_HEAD_DIM] / acc[V_HEAD_DIM:V_HEAD_DIM + 1])
    o_ref[0] = jnp.concatenate(outs, axis=0).T.astype(bf16)


def _attention(q, kv_sources, tq, tk, online):
    b, hh, t, hp = q.shape
    in_specs = [pl.BlockSpec((1, 2, tq, hp), lambda bi, pi, i: (bi, pi, i, 0))]
    args = [q]
    chunks = []
    for k, vt in kv_sources:
        n = k.shape[2]
        step = min(tk, n)
        chunks.append((n // step, step))
        in_specs += [pl.BlockSpec((1, 2, n, hp), lambda bi, pi, i: (bi, pi, 0, 0)),
                     pl.BlockSpec((1, 2, V_T_ROWS, n), lambda bi, pi, i: (bi, pi, 0, 0))]
        args += [k, vt]
    return pl.pallas_call(
        functools.partial(_attn_body, chunks=tuple(chunks), online=online),
        grid=(b, hh // 2, t // tq),
        in_specs=in_specs,
        out_specs=pl.BlockSpec((1, tq, hp), lambda bi, pi, i: (bi, i, pi)),
        out_shape=jax.ShapeDtypeStruct((b, t, hh * V_HEAD_DIM), bf16),
        compiler_params=_cparams(("arbitrary", "arbitrary", "arbitrary")),
        name="attn%d%s" % (len(kv_sources), "_online" if online else "_bounded"),
    )(*args)


def _gated_ffn(x, out, mod, n2g_ref, w1_ref, w3_ref, w2_ref):
    d = D_MODEL
    x1 = x + mod[:, 2 * d:3 * d] * out
    h2 = _modulated_norm(x1, mod, n2g_ref, 3 * d).astype(bf16)
    y = None
    for lo, hi in _ff_chunks(w1_ref.shape[2]):
        sl = slice(lo, hi)
        u = _dot(h2, w1_ref[0, :, sl])
        g = _dot(h2, w3_ref[0, :, sl])
        act = (u / (1.0 + jnp.exp(-u)) * g).astype(bf16)
        part = _dot(act, w2_ref[0, sl, :])
        y = part if y is None else y + part
    return x1 + mod[:, 5 * d:6 * d] * y


FFN_CHUNKS = 4


def _ff_chunks(d_ff):
    step = -(-d_ff // (FFN_CHUNKS * MXU_TILE)) * MXU_TILE
    return [(lo, min(lo + step, d_ff)) for lo in range(0, d_ff, step)]


def _mix_ffn_body(*refs, halo):
    x_ref, a_ref, pc_ref = refs[:3]
    if halo:
        pprev_ref, pnext_ref = refs[3:5]
        refs = refs[5:]
    else:
        refs = refs[3:]
    (mod_ref, convw_ref, woa_ref, woc_ref, n2g_ref, w1_ref, w3_ref, w2_ref, o_ref) = refs
    x = x_ref[0]
    mod = mod_ref[0, 0]
    tm = x.shape[0]

    out = _dot(a_ref[0], woa_ref[...])

    cd = CONV_DIM
    pc = pc_ref[0].astype(f32)
    z = pc[:, cd:2 * cd] * pc[:, 2 * cd:]
    row = lax.broadcasted_iota(jnp.int32, (tm, cd), 0)
    if halo:
        i = pl.program_id(1)
        last = pl.num_programs(1) - 1
        pp = pprev_ref[0, BF16_SUBLANES - 1:BF16_SUBLANES, :].astype(f32)
        pn = pnext_ref[0, 0:1, :].astype(f32)
        z_prev = jnp.where(i > 0, pp[:, cd:2 * cd] * pp[:, 2 * cd:], 0.0)
        z_next = jnp.where(i < last, pn[:, cd:2 * cd] * pn[:, 2 * cd:], 0.0)
    else:
        z_prev = jnp.zeros((1, cd), f32)
        z_next = jnp.zeros((1, cd), f32)
    z_up = jnp.where(row == 0, z_prev, pltpu.roll(z, 1, axis=0))
    z_dn = jnp.where(row == tm - 1, z_next, pltpu.roll(z, tm - 1, axis=0))
    cw = convw_ref[0]
    y = z_up * cw[0:1, :] + z * cw[1:2, :] + z_dn * cw[2:3, :]
    out = out + _dot((pc[:, :cd] * y).astype(bf16), woc_ref[...])

    o_ref[0] = _gated_ffn(x, out, mod, n2g_ref, w1_ref, w3_ref, w2_ref)


def _mix_ffn(x, a, pc, mods, layer, mod_row, conv_w, j, w, n2g, w1, w3, w2, tm):
    b, t, d = x.shape
    hh, hp = MLA_HEADS, HEAD_PAD
    halo = t > tm
    in_specs = [
        pl.BlockSpec((1, tm, d), lambda bi, i: (bi, i, 0)),
        pl.BlockSpec((1, tm, hh * V_HEAD_DIM), lambda bi, i: (bi, i, 0)),
        pl.BlockSpec((1, tm, 3 * CONV_DIM), lambda bi, i: (bi, i, 0)),
    ]
    args = [x, a, pc]
    if halo:
        per = tm // BF16_SUBLANES
        nblk = t // BF16_SUBLANES
        in_specs += [
            pl.BlockSpec((1, BF16_SUBLANES, 3 * CONV_DIM),
                         lambda bi, i: (bi, jnp.maximum(i * per - 1, 0), 0)),
            pl.BlockSpec((1, BF16_SUBLANES, 3 * CONV_DIM),
                         lambda bi, i: (bi, jnp.minimum((i + 1) * per, nblk - 1), 0)),
        ]
        args += [pc, pc]
    in_specs += [
        _mod_spec(layer, mod_row),
        _const_spec(conv_w.shape, j),
        _const_spec(w["w_o_attn"].shape),
        _const_spec(w["w_o_conv"].shape),
        _const_spec((1, d)),
        _const_spec(w1.shape, layer),
        _const_spec(w3.shape, layer),
        _const_spec(w2.shape, layer),
    ]
    args += [mods, conv_w, w["w_o_attn"], w["w_o_conv"], n2g, w1, w3, w2]
    return pl.pallas_call(
        functools.partial(_mix_ffn_body, halo=halo),
        grid=(b, t // tm),
        in_specs=in_specs,
        out_specs=pl.BlockSpec((1, tm, d), lambda bi, i: (bi, i, 0)),
        out_shape=jax.ShapeDtypeStruct((b, t, d), f32),
        compiler_params=_cparams(("arbitrary", "arbitrary")),
        name="mix_ffn_halo" if halo else "mix_ffn",
    )(*args)


SUBLANES = 8


def _dft_a_body(x_ref, mod_ref, n1g_ref, kw_ref, cd_ref, twc_ref, tws_ref, o_ref):
    d = D_MODEL
    gd = FOURIER_GROUP_DIM
    rows = DFT_T1 * SUBLANES
    mod = mod_ref[0, 0]
    halves = [[], []]
    for half in range(x_ref.shape[2] // SUBLANES):
        sub = slice(half * SUBLANES, (half + 1) * SUBLANES)
        x = x_ref[0, :, sub, :].reshape(rows, d)
        h = _modulated_norm(x, mod, n1g_ref, 0).astype(bf16)
        a = _dot(kw_ref[...], h).astype(bf16)
        tw_c = twc_ref[half]
        tw_s = tws_ref[half]
        re_cols, im_cols = [], []
        for g in range(FOURIER_GROUPS):
            sl = slice(g * gd, (g + 1) * gd)
            z = _dot(jnp.concatenate([a[:rows, sl], a[rows:, sl]], axis=1), cd_ref[...])
            z_re, z_im = z[:, :gd], z[:, gd:]
            c = jnp.tile(tw_c, (1, gd // LANES))
            s = jnp.tile(tw_s, (1, gd // LANES))
            re_cols.append(z_re * c + z_im * s)
            im_cols.append(z_im * c - z_re * s)
        halves[0].append(jnp.concatenate(re_cols, axis=1).reshape(DFT_T1, SUBLANES, d))
        halves[1].append(jnp.concatenate(im_cols, axis=1).reshape(DFT_T1, SUBLANES, d))
    for comp in range(2):
        o_ref[0, comp] = jnp.concatenate(halves[comp], axis=1).astype(bf16)


def _dft_a(x, mods, layer, n1g, tabs):
    b, t, d = x.shape
    t2 = t // DFT_T1
    k = BF16_SUBLANES
    rows = DFT_T1 * SUBLANES
    return pl.pallas_call(
        _dft_a_body,
        grid=(b, t2 // k),
        in_specs=[
            pl.BlockSpec((1, DFT_T1, k, d), lambda bi, j: (bi, 0, j, 0)),
            _mod_spec(layer, lambda bi: bi),
            _const_spec((1, d)),
            _const_spec((2 * rows, rows)),
            _const_spec((2 * FOURIER_GROUP_DIM, 2 * FOURIER_GROUP_DIM)),
            pl.BlockSpec((k // SUBLANES, rows, LANES), lambda bi, j: (j, 0, 0)),
            pl.BlockSpec((k // SUBLANES, rows, LANES), lambda bi, j: (j, 0, 0)),
        ],
        out_specs=pl.BlockSpec((1, 2, DFT_T1, k, d), lambda bi, j: (bi, 0, 0, j, 0)),
        out_shape=jax.ShapeDtypeStruct((b, 2, DFT_T1, t2, d), bf16),
        compiler_params=_cparams(("arbitrary", "arbitrary")),
        name="dft_a",
    )(x.reshape(b, DFT_T1, t2, d), mods, n1g, tabs["kw"], tabs["cd"], tabs["tw_cos"], tabs["tw_sin"])


def _dft_ffn_body(x_ref, z_ref, mod_ref, g_ref, wf_ref, n2g_ref, w1_ref, w3_ref, w2_ref, o_ref):
    d = D_MODEL
    tp = x_ref.shape[1]
    rows = tp * SUBLANES
    r0 = pl.multiple_of(pl.program_id(2) * tp, tp)
    cs = g_ref[pl.ds(r0, tp), :]
    t2 = z_ref.shape[3]
    f = jnp.stack([_dot(cs, z_ref[0, :, a].reshape(2 * t2, d)) for a in range(SUBLANES)], axis=0)
    f = jnp.swapaxes(f, 0, 1).reshape(rows, d).astype(bf16)
    out = _dot(f, wf_ref[0])
    x = x_ref[0].reshape(rows, d)
    res = _gated_ffn(x, out, mod_ref[0, 0], n2g_ref, w1_ref, w3_ref, w2_ref)
    o_ref[0] = res.reshape(tp, SUBLANES, d)


def _dft_ffn(x, z, mods, layer, tabs, wf, j, n2g, w1, w3, w2, tp):
    b, t, d = x.shape
    t2 = t // DFT_T1
    out = pl.pallas_call(
        _dft_ffn_body,
        grid=(b, DFT_T1 // SUBLANES, t2 // tp),
        in_specs=[
            pl.BlockSpec((1, tp, SUBLANES, d), lambda bi, jj, hh: (bi, hh, jj, 0)),
            pl.BlockSpec((1, 2, SUBLANES, t2, d), lambda bi, jj, hh: (bi, 0, jj, 0, 0)),
            pl.BlockSpec((1, 1, 1, 6 * d), lambda bi, jj, hh: (layer, bi, 0, 0)),
            _const_spec((t2, 2 * t2)),
            _const_spec(wf.shape, j),
            _const_spec((1, d)),
            _const_spec(w1.shape, layer),
            _const_spec(w3.shape, layer),
            _const_spec(w2.shape, layer),
        ],
        out_specs=pl.BlockSpec((1, tp, SUBLANES, d), lambda bi, jj, hh: (bi, hh, jj, 0)),
        out_shape=jax.ShapeDtypeStruct((b, t2, DFT_T1, d), f32),
        compiler_params=_cparams(("arbitrary", "arbitrary", "arbitrary")),
        name="dft_ffn",
    )(x.reshape(b, t2, DFT_T1, d), z, mods, tabs["g2"], wf, n2g, w1, w3, w2)
    return out.reshape(b, t, d)


def _angle(i, j, n):
    return (2.0 * math.pi / n) * ((i * j) % n).astype(f32)


def _dft_tables(t):
    t1, t2 = DFT_T1, t // DFT_T1
    i1 = jnp.arange(t1, dtype=jnp.int32)
    r = jnp.arange(t1 * SUBLANES, dtype=jnp.int32)
    ang64 = _angle(r[:, None] // SUBLANES, r[None, :] // SUBLANES, t1)
    keep = jnp.where(r[:, None] % SUBLANES == r[None, :] % SUBLANES, t1 ** -0.5, 0.0)
    kw = jnp.concatenate([jnp.cos(ang64) * keep, -jnp.sin(ang64) * keep], axis=0).astype(bf16)

    ic = jnp.arange(FOURIER_GROUP_DIM, dtype=jnp.int32)
    angc = _angle(ic[:, None], ic[None, :], FOURIER_GROUP_DIM)
    sc_ = FOURIER_GROUP_DIM ** -0.5
    cc, sn = jnp.cos(angc) * sc_, jnp.sin(angc) * sc_
    cd = jnp.concatenate([jnp.concatenate([cc, -sn], axis=1),
                          jnp.concatenate([sn, cc], axis=1)], axis=0).astype(bf16)

    i2 = jnp.arange(t2, dtype=jnp.int32)
    angt = _angle(i2[:, None], i1[None, :], t)
    angt = angt.reshape(t2 // SUBLANES, SUBLANES, t1).transpose(0, 2, 1).reshape(t2 // SUBLANES, -1)
    tw = lax.optimization_barrier((jnp.cos(angt), jnp.sin(angt)))
    tw_cos, tw_sin = (jnp.broadcast_to(v[:, :, None], v.shape + (LANES,)) for v in tw)

    ang2 = _angle(i2[:, None], i2[None, :], t2)
    s2 = t2 ** -0.5
    g2 = jnp.concatenate([jnp.cos(ang2) * s2, jnp.sin(ang2) * s2], axis=1).astype(bf16)
    return dict(kw=kw, cd=cd, tw_cos=tw_cos, tw_sin=tw_sin, g2=g2)


def _rope_tables(t):
    rows = t // GRID_W
    half = QK_ROPE_DIM // 2
    inv = 1.0 / (ROPE_BASE ** (jnp.arange(0, half, 2, dtype=f32) / half))
    ar = jnp.arange(rows, dtype=f32)[:, None] * inv
    ac = jnp.arange(GRID_W, dtype=f32)[:, None] * inv
    pad_hi = HEAD_PAD - QK_HEAD_DIM

    def place(n, first, second, at, lead):
        z = jnp.zeros((n, half), f32)
        blocks = [jnp.full((n, QK_NOPE_DIM), lead, f32)]
        blocks += [jnp.concatenate([first, second], axis=1), z] if at == 0 else \
                  [z, jnp.concatenate([first, second], axis=1)]
        blocks.append(jnp.zeros((n, pad_hi), f32))
        return jnp.concatenate(blocks, axis=1)

    rrow = jnp.stack([place(rows, jnp.cos(ar), jnp.cos(ar), 0, 0.0),
                      place(rows, -jnp.sin(ar), jnp.sin(ar), 0, 0.0)])
    rcol = jnp.stack([place(GRID_W, jnp.cos(ac), jnp.cos(ac), 1, 1.0),
                      place(GRID_W, -jnp.sin(ac), jnp.sin(ac), 1, 0.0)])
    return rrow, rcol


def _partner(a, axis):
    parts = jnp.split(a, 4, axis=axis)
    return jnp.concatenate([parts[1], parts[0], parts[3], parts[2]], axis=axis)


def _prep_even(j, w_in, q_norm_g, kv_norm_g, w_uq, w_ukv, q_gain, k_gain, w_o):
    hh, hp = MLA_HEADS, HEAD_PAD
    d = D_MODEL
    nope, hd = QK_NOPE_DIM, QK_HEAD_DIM
    wi = w_in[j]
    w_pe = wi[:, PE_COL:CONV_OFFSET]
    w_in_p = jnp.concatenate([wi[:, :PE_COL], jnp.zeros((d, nope), f32), w_pe, _partner(w_pe, 1),
                              wi[:, CONV_OFFSET:]], axis=1).astype(bf16)
    wq = w_uq[j].reshape(Q_LORA_RANK, hh, hd)
    pad_head = lambda a: jnp.pad(a, ((0, 0), (0, 0), (0, hp - hd))).reshape(Q_LORA_RANK, hh * hp)
    wq_sw = jnp.zeros_like(wq).at[:, :, nope:].set(_partner(wq[:, :, nope:], 2))
    w_uq_p = jnp.concatenate([pad_head(wq), pad_head(wq_sw)], axis=1)
    wkv = w_ukv[j].reshape(KV_LORA_RANK, hh, nope + V_HEAD_DIM)
    wk = jnp.pad(wkv[:, :, :nope], ((0, 0), (0, 0), (0, hp - nope))).reshape(KV_LORA_RANK, hh * hp)
    w_ukv_p = wk.astype(bf16)
    w_vt = wkv[:, :, nope:].reshape(KV_LORA_RANK, hh * V_HEAD_DIM).T.astype(bf16)
    w_uq_p = w_uq_p.astype(bf16)

    def gain_rows(g):
        sw = jnp.zeros_like(g).at[nope:].set(_partner(g[nope:], 0))
        return [jnp.pad(g, (0, hp - hd)), jnp.pad(sw, (0, hp - hd))]

    zero = jnp.zeros((hp,), f32)
    gains = jnp.stack(gain_rows(q_gain[j] * (QK_SCALE * LOG2E)) + gain_rows(k_gain[j]) + [zero] * 4)
    return dict(
        w_in=w_in_p, w_uq=w_uq_p, w_ukv=w_ukv_p, w_vt=w_vt,
        q_norm_g=q_norm_g[j].reshape(1, -1), kv_norm_g=kv_norm_g[j].reshape(1, -1), gains=gains,
        w_o_attn=w_o[j][:hh * V_HEAD_DIM].astype(bf16), w_o_conv=w_o[j][hh * V_HEAD_DIM:].astype(bf16))


def _pick(n, pref):
    return pref if n % pref == 0 else n


def kernel(x, c, ctx, c_ctx, ada_w, ada_b, norm1_g, norm2_g, w_in, q_norm_g, kv_norm_g, w_uq, w_ukv,
           q_gain, k_gain, conv_w, w_o, w_fourier, ffn_w1, ffn_w3, ffn_w2):
    b, s, d = x.shape
    depth = ada_w.shape[0]
    cvec = jnp.zeros((8, d), f32).at[:b].set(c).at[b].set(c_ctx)
    mods = _ada(cvec, ada_w, ada_b).reshape(depth, 8, 1, 6 * d)
    lat_row = lambda bi: bi
    ctx_row = lambda bi: b
    w1 = ffn_w1.astype(bf16)
    w3 = ffn_w3.astype(bf16)
    w2 = ffn_w2.astype(bf16)
    wf = w_fourier.astype(bf16)
    for i in range(depth):
        last = i == depth - 1
        j = i // 2
        n1g = norm1_g[i].reshape(1, d)
        n2g = norm2_g[i].reshape(1, d)
        if i % 2 == 0:
            w = _prep_even(j, w_in, q_norm_g, kv_norm_g, w_uq, w_ukv, q_gain, k_gain, w_o)
            tm = _pick(s, 512)
            lc = ctx.shape[1]
            q_l, k_l, vt_l, pc_l = _qkv(x, mods, i, lat_row, n1g, w, _rope_tables(s), _pick(s, 1024))
            q_c, k_c, vt_c, pc_c = _qkv(ctx, mods, i, ctx_row, n1g, w, None, lc)
            bound = (QK_SCALE * LOG2E * QK_HEAD_DIM * BF16_ROUNDING_SLACK
                     * jnp.max(jnp.abs(q_gain[j])) * jnp.max(jnp.abs(k_gain[j])))
            tq, tk = _pick(s, 2048), _pick(s, 1024)
            a_l = lax.cond(
                bound < EXP2_SAFE_SCORE,
                lambda q_, kl, vtl, kc, vtc: _attention(q_, [(kl, vtl), (kc, vtc)], tq, tk, online=False),
                lambda q_, kl, vtl, kc, vtc: _attention(q_, [(kl, vtl), (kc, vtc)], _pick(s, 1024), tk,
                                                        online=True),
                q_l, k_l, vt_l, k_c, vt_c)
            x_new = _mix_ffn(x, a_l, pc_l, mods, i, lat_row, conv_w, j, w, n2g, w1, w3, w2, tm)
            if not last:
                a_c = _attention(q_c, [(k_c, vt_c)], lc, lc, online=True)
                ctx = _mix_ffn(ctx, a_c, pc_c, mods, i, ctx_row, conv_w, j, w, n2g, w1, w3, w2, lc)
            x = x_new
        else:
            tabs = _dft_tables(s)
            t2 = s // DFT_T1
            z = _dft_a(x, mods, i, n1g, tabs)
            x = _dft_ffn(x, z, mods, i, tabs, wf, j, n2g, w1, w3, w2, _pick(t2, 64))
            assert last, "odd non-final layers are not implemented"
    return x
```

```python
import functools
import math

import jax
import jax.numpy as jnp
from jax import lax
from jax.experimental import pallas as pl
from jax.experimental.pallas import tpu as pltpu

D_MODEL = 1024
GRID_W = 64
MLA_HEADS = 8
QK_NOPE_DIM = 64
QK_ROPE_DIM = 32
QK_HEAD_DIM = QK_NOPE_DIM + QK_ROPE_DIM
V_HEAD_DIM = 64
Q_LORA_RANK = 384
KV_LORA_RANK = 256
QK_SCALE = QK_HEAD_DIM ** -0.5
ROPE_BASE = 10000.0
CONV_DIM = 512
CONV_OFFSET = Q_LORA_RANK + KV_LORA_RANK + QK_ROPE_DIM
FOURIER_GROUPS = 4
FOURIER_GROUP_DIM = D_MODEL // FOURIER_GROUPS
EPS = 1e-6

LANES = 128
BF16_SUBLANES = 16
MXU_TILE = 256
VMEM_LIMIT_BYTES = 60 * 1024 * 1024

HEAD_PAD = LANES
ALL_HEADS = MLA_HEADS * HEAD_PAD
PE_COL = Q_LORA_RANK + KV_LORA_RANK
CONV_COL = PE_COL + HEAD_PAD
IN_PROJ_PAD = CONV_COL + 3 * CONV_DIM
QKV_ROW_BLOCK = 128
QKV_SUB_ROWS = 256
DFT_T1 = 64
LOG2E = math.log2(math.e)
EXP2_SAFE_SCORE = 64.0
BF16_ROUNDING_SLACK = 1.02
NT_DIMS = (((1,), (1,)), ((), ()))
V_T_ROWS = V_HEAD_DIM + BF16_SUBLANES

bf16 = jnp.bfloat16
f32 = jnp.float32


def _cparams(sem):
    return pltpu.CompilerParams(dimension_semantics=sem, vmem_limit_bytes=VMEM_LIMIT_BYTES)


def _const_spec(shape, lead=None):
    if lead is None:
        nd = len(shape)
        return pl.BlockSpec(shape, lambda *_: (0,) * nd, pipeline_mode=pl.Buffered(1))
    nd = len(shape) - 1
    return pl.BlockSpec((1,) + tuple(shape[1:]), lambda *_: (lead,) + (0,) * nd,
                        pipeline_mode=pl.Buffered(1))


def _mod_spec(layer, mod_row):
    return pl.BlockSpec((1, 1, 1, 6 * D_MODEL), lambda bi, i: (layer, mod_row(bi), 0, 0))


def _dot(a, b):
    return jnp.dot(a, b, preferred_element_type=f32)


def _rms_scale(x, width):
    return lax.rsqrt(jnp.sum(x * x, axis=-1, keepdims=True) * (1.0 / width) + EPS)


def _ada_body(c_ref, w_ref, b_ref, o_ref):
    c = c_ref[...]
    s = c / (1.0 + jnp.exp(-c))
    o_ref[0] = _dot(s.astype(bf16), w_ref[0].astype(bf16)) + b_ref[0]


def _ada(cvec, ada_w, ada_b):
    depth, d, n = ada_w.shape
    tn = 1536
    return pl.pallas_call(
        _ada_body,
        grid=(depth, n // tn),
        in_specs=[
            pl.BlockSpec((8, d), lambda l, j: (0, 0)),
            pl.BlockSpec((1, d, tn), lambda l, j: (l, 0, j)),
            pl.BlockSpec((1, 1, tn), lambda l, j: (l, 0, j)),
        ],
        out_specs=pl.BlockSpec((1, 8, tn), lambda l, j: (l, 0, j)),
        out_shape=jax.ShapeDtypeStruct((depth, 8, n), f32),
        compiler_params=_cparams(("arbitrary", "arbitrary")),
        name="ada",
    )(cvec, ada_w, ada_b.reshape(depth, 1, n))


def _modulated_norm(x, mod, g_ref, lo):
    d = D_MODEL
    shift = mod[:, lo:lo + d]
    scale = mod[:, lo + d:lo + 2 * d]
    gain = g_ref[...] * (1.0 + scale)
    return (x * _rms_scale(x, d)) * gain + shift


def _qkv_body(*refs, rope):
    (x_ref, mod_ref, n1g_ref, win_ref, qng_ref, kvng_ref, wuq_ref, wukv_ref, wvt_ref,
     gains_ref) = refs[:10]
    if rope:
        rrow_ref, rcol_ref = refs[10:12]
        refs = refs[12:]
    else:
        refs = refs[10:]
    q_out, k_out, vt_out, pc_out, q_scr, kv_scr, pe_scr = refs
    tm = x_ref.shape[1]
    mod = mod_ref[0, 0]
    gains = gains_ref[...]
    lane = lax.broadcasted_iota(jnp.int32, (1, HEAD_PAD), 1)
    real = (lane < QK_HEAD_DIM).astype(f32)
    rb = min(QKV_ROW_BLOCK, tm)
    per = rb // GRID_W
    grid_row0 = pl.program_id(1) * (tm // GRID_W)
    to_rope = HEAD_PAD - QK_ROPE_DIM
    sub = min(QKV_SUB_ROWS, tm)

    def qk_scale(t):
        return lax.rsqrt(jnp.sum(t * t * real, axis=-1, keepdims=True) * (1.0 / QK_HEAD_DIM) + EPS)

    def project(lo):
        rows = slice(lo, lo + sub)
        h = _modulated_norm(x_ref[0, rows, :], mod, n1g_ref, 0)
        p = _dot(h.astype(bf16), win_ref[...])
        pc_out[0, rows, :] = p[:, CONV_COL:].astype(bf16)
        pe_scr[rows, :] = p[:, PE_COL:CONV_COL]
        cq = p[:, :Q_LORA_RANK]
        cqn = cq * _rms_scale(cq, Q_LORA_RANK) * qng_ref[...]
        q_scr[rows, :] = _dot(cqn.astype(bf16), wuq_ref[...])
        ckv = p[:, Q_LORA_RANK:PE_COL]
        ckvn = (ckv * _rms_scale(ckv, KV_LORA_RANK) * kvng_ref[...]).astype(bf16)
        kv_scr[rows, :] = _dot(ckvn, wukv_ref[...])
        vt = lax.dot_general(wvt_ref[...], ckvn, NT_DIMS, preferred_element_type=f32)
        ones_rows = (lax.broadcasted_iota(jnp.int32, (V_T_ROWS - V_HEAD_DIM, sub), 0) == 0).astype(f32)
        for hd in range(MLA_HEADS):
            vt_h = vt[hd * V_HEAD_DIM:(hd + 1) * V_HEAD_DIM]
            vt_out[0, hd, :, rows] = jnp.concatenate([vt_h, ones_rows], axis=0).astype(bf16)

    def finish(ib):
        rows = slice(ib * rb, (ib + 1) * rb)
        if rope:
            tabs = []
            for comp in range(2):
                parts = []
                for g in range(per):
                    rr = rrow_ref[comp, pl.ds(grid_row0 + ib * per + g, 1), :]
                    parts.append(rcol_ref[comp] + rr)
                tabs.append(jnp.concatenate(parts, axis=0))
            cos, sin = tabs
            qa, qb = cos * gains[0:1], sin * gains[1:2]
            ka, kb = cos * gains[2:3], sin * gains[3:4]
        pe = pe_scr[rows, :]
        if rope:
            pe_rot = pltpu.roll(pe, to_rope, axis=1) * kb
        for hd in range(MLA_HEADS):
            sl = slice(hd * HEAD_PAD, (hd + 1) * HEAD_PAD)
            qh = q_scr[rows, sl]
            kh = kv_scr[rows, sl] + pe
            rq = qk_scale(qh)
            rk = qk_scale(kh)
            if rope:
                partner = slice(ALL_HEADS + hd * HEAD_PAD, ALL_HEADS + (hd + 1) * HEAD_PAD)
                qo = (qh * qa + q_scr[rows, partner] * qb) * rq
                ko = (kh * ka + pe_rot) * rk
            else:
                qo = qh * gains[0:1] * rq
                ko = kh * gains[2:3] * rk
            q_out[0, hd, rows, :] = qo.astype(bf16)
            k_out[0, hd, rows, :] = ko.astype(bf16)

    for st in range(tm // sub):
        project(st * sub)
        for ib in range(st * sub // rb, (st + 1) * sub // rb):
            finish(ib)


def _qkv(x, mods, layer, mod_row, n1g, w, rope_tabs, tm):
    b, t, d = x.shape
    hh, hp = MLA_HEADS, HEAD_PAD
    rope = rope_tabs is not None
    in_specs = [
        pl.BlockSpec((1, tm, d), lambda bi, i: (bi, i, 0)),
        _mod_spec(layer, mod_row),
        _const_spec((1, d)),
        _const_spec(w["w_in"].shape),
        _const_spec((1, Q_LORA_RANK)),
        _const_spec((1, KV_LORA_RANK)),
        _const_spec(w["w_uq"].shape),
        _const_spec(w["w_ukv"].shape),
        _const_spec(w["w_vt"].shape),
        _const_spec((8, hp)),
    ]
    args = [x, mods, n1g, w["w_in"], w["q_norm_g"], w["kv_norm_g"], w["w_uq"], w["w_ukv"], w["w_vt"],
            w["gains"]]
    if rope:
        rrow, rcol = rope_tabs
        in_specs += [_const_spec(rrow.shape), _const_spec(rcol.shape)]
        args += [rrow, rcol]
    head_spec = pl.BlockSpec((1, hh, tm, hp), lambda bi, i: (bi, 0, i, 0))
    head_shape = jax.ShapeDtypeStruct((b, hh, t, hp), bf16)
    return pl.pallas_call(
        functools.partial(_qkv_body, rope=rope),
        grid=(b, t // tm),
        in_specs=in_specs,
        out_specs=[head_spec, head_spec,
                   pl.BlockSpec((1, hh, V_T_ROWS, tm), lambda bi, i: (bi, 0, 0, i)),
                   pl.BlockSpec((1, tm, 3 * CONV_DIM), lambda bi, i: (bi, i, 0))],
        out_shape=[head_shape, head_shape,
                   jax.ShapeDtypeStruct((b, hh, V_T_ROWS, t), bf16),
                   jax.ShapeDtypeStruct((b, t, 3 * CONV_DIM), bf16)],
        scratch_shapes=[pltpu.VMEM((tm, 2 * ALL_HEADS), f32),
                        pltpu.VMEM((tm, ALL_HEADS), f32),
                        pltpu.VMEM((tm, hp), f32)],
        compiler_params=_cparams(("arbitrary", "arbitrary")),
        name="qkv_rope" if rope else "qkv_ctx",
    )(*args)


def _attn_body(q_ref, *refs, chunks, online):
    o_ref = refs[-1]
    k_refs, vt_refs = refs[0:-1:2], refs[1:-1:2]
    outs = []
    for par in range(2):
        q = q_ref[0, par]
        m = None
        acc = None
        for src, (n_chunks, tk) in enumerate(chunks):
            for c in range(n_chunks):
                ks = slice(c * tk, (c + 1) * tk)
                st = lax.dot_general(k_refs[src][0, par, ks, :], q, NT_DIMS, preferred_element_type=f32)
                if online:
                    m_cur = jnp.max(st, axis=0, keepdims=True)
                    m_new = m_cur if m is None else jnp.maximum(m, m_cur)
                    part = _dot(vt_refs[src][0, par, :, ks], jnp.exp2(st - m_new).astype(bf16))
                    acc = part if m is None else jnp.exp2(m - m_new) * acc + part
                    m = m_new
                else:
                    part = _dot(vt_refs[src][0, par, :, ks], jnp.exp2(st).astype(bf16))
                    acc = part if acc is None else acc + part
        outs.append(acc[:V_HEAD_DIM] / acc[V_HEAD_DIM:V_HEAD_DIM + 1])
    o_ref[0] = jnp.concatenate(outs, axis=0).T.astype(bf16)


def _attention(q, kv_sources, tq, tk, online):
    b, hh, t, hp = q.shape
    in_specs = [pl.BlockSpec((1, 2, tq, hp), lambda bi, pi, i: (bi, pi, i, 0))]
    args = [q]
    chunks = []
    for k, vt in kv_sources:
        n = k.shape[2]
        step = min(tk, n)
        chunks.append((n // step, step))
        in_specs += [pl.BlockSpec((1, 2, n, hp), lambda bi, pi, i: (bi, pi, 0, 0)),
                     pl.BlockSpec((1, 2, V_T_ROWS, n), lambda bi, pi, i: (bi, pi, 0, 0))]
        args += [k, vt]
    return pl.pallas_call(
        functools.partial(_attn_body, chunks=tuple(chunks), online=online),
        grid=(b, hh // 2, t // tq),
        in_specs=in_specs,
        out_specs=pl.BlockSpec((1, tq, hp), lambda bi, pi, i: (bi, i, pi)),
        out_shape=jax.ShapeDtypeStruct((b, t, hh * V_HEAD_DIM), bf16),
        compiler_params=_cparams(("arbitrary", "arbitrary", "arbitrary")),
        name="attn%d%s" % (len(kv_sources), "_online" if online else "_bounded"),
    )(*args)


def _gated_ffn(x, out, mod, n2g_ref, w1_ref, w3_ref, w2_ref):
    d = D_MODEL
    x1 = x + mod[:, 2 * d:3 * d] * out
    h2 = _modulated_norm(x1, mod, n2g_ref, 3 * d).astype(bf16)
    y = None
    for lo, hi in _ff_chunks(w1_ref.shape[2]):
        sl = slice(lo, hi)
        u = _dot(h2, w1_ref[0, :, sl])
        g = _dot(h2, w3_ref[0, :, sl])
        act = (u / (1.0 + jnp.exp(-u)) * g).astype(bf16)
        part = _dot(act, w2_ref[0, sl, :])
        y = part if y is None else y + part
    return x1 + mod[:, 5 * d:6 * d] * y


FFN_CHUNKS = 4


def _ff_chunks(d_ff):
    step = -(-d_ff // (FFN_CHUNKS * MXU_TILE)) * MXU_TILE
    return [(lo, min(lo + step, d_ff)) for lo in range(0, d_ff, step)]


def _mix_ffn_body(*refs, halo):
    x_ref, a_ref, pc_ref = refs[:3]
    if halo:
        pprev_ref, pnext_ref = refs[3:5]
        refs = refs[5:]
    else:
        refs = refs[3:]
    (mod_ref, convw_ref, woa_ref, woc_ref, n2g_ref, w1_ref, w3_ref, w2_ref, o_ref) = refs
    x = x_ref[0]
    mod = mod_ref[0, 0]
    tm = x.shape[0]

    out = _dot(a_ref[0], woa_ref[...])

    cd = CONV_DIM
    pc = pc_ref[0].astype(f32)
    z = pc[:, cd:2 * cd] * pc[:, 2 * cd:]
    row = lax.broadcasted_iota(jnp.int32, (tm, cd), 0)
    if halo:
        i = pl.program_id(1)
        last = pl.num_programs(1) - 1
        pp = pprev_ref[0, BF16_SUBLANES - 1:BF16_SUBLANES, :].astype(f32)
        pn = pnext_ref[0, 0:1, :].astype(f32)
        z_prev = jnp.where(i > 0, pp[:, cd:2 * cd] * pp[:, 2 * cd:], 0.0)
        z_next = jnp.where(i < last, pn[:, cd:2 * cd] * pn[:, 2 * cd:], 0.0)
    else:
        z_prev = jnp.zeros((1, cd), f32)
        z_next = jnp.zeros((1, cd), f32)
    z_up = jnp.where(row == 0, z_prev, pltpu.roll(z, 1, axis=0))
    z_dn = jnp.where(row == tm - 1, z_next, pltpu.roll(z, tm - 1, axis=0))
    cw = convw_ref[0]
    y = z_up * cw[0:1, :] + z * cw[1:2, :] + z_dn * cw[2:3, :]
    out = out + _dot((pc[:, :cd] * y).astype(bf16), woc_ref[...])

    o_ref[0] = _gated_ffn(x, out, mod, n2g_ref, w1_ref, w3_ref, w2_ref)


def _mix_ffn(x, a, pc, mods, layer, mod_row, conv_w, j, w, n2g, w1, w3, w2, tm):
    b, t, d = x.shape
    hh, hp = MLA_HEADS, HEAD_PAD
    halo = t > tm
    in_specs = [
        pl.BlockSpec((1, tm, d), lambda bi, i: (bi, i, 0)),
        pl.BlockSpec((1, tm, hh * V_HEAD_DIM), lambda bi, i: (bi, i, 0)),
        pl.BlockSpec((1, tm, 3 * CONV_DIM), lambda bi, i: (bi, i, 0)),
    ]
    args = [x, a, pc]
    if halo:
        per = tm // BF16_SUBLANES
        nblk = t // BF16_SUBLANES
        in_specs += [
            pl.BlockSpec((1, BF16_SUBLANES, 3 * CONV_DIM),
                         lambda bi, i: (bi, jnp.maximum(i * per - 1, 0), 0)),
            pl.BlockSpec((1, BF16_SUBLANES, 3 * CONV_DIM),
                         lambda bi, i: (bi, jnp.minimum((i + 1) * per, nblk - 1), 0)),
        ]
        args += [pc, pc]
    in_specs += [
        _mod_spec(layer, mod_row),
        _const_spec(conv_w.shape, j),
        _const_spec(w["w_o_attn"].shape),
        _const_spec(w["w_o_conv"].shape),
        _const_spec((1, d)),
        _const_spec(w1.shape, layer),
        _const_spec(w3.shape, layer),
        _const_spec(w2.shape, layer),
    ]
    args += [mods, conv_w, w["w_o_attn"], w["w_o_conv"], n2g, w1, w3, w2]
    return pl.pallas_call(
        functools.partial(_mix_ffn_body, halo=halo),
        grid=(b, t // tm),
        in_specs=in_specs,
        out_specs=pl.BlockSpec((1, tm, d), lambda bi, i: (bi, i, 0)),
        out_shape=jax.ShapeDtypeStruct((b, t, d), f32),
        compiler_params=_cparams(("arbitrary", "arbitrary")),
        name="mix_ffn_halo" if halo else "mix_ffn",
    )(*args)


SUBLANES = 8


def _dft_a_body(x_ref, mod_ref, n1g_ref, kw_ref, cd_ref, twc_ref, tws_ref, o_ref):
    d = D_MODEL
    gd = FOURIER_GROUP_DIM
    rows = DFT_T1 * SUBLANES
    mod = mod_ref[0, 0]
    halves = [[], []]
    for half in range(x_ref.shape[2] // SUBLANES):
        sub = slice(half * SUBLANES, (half + 1) * SUBLANES)
        x = x_ref[0, :, sub, :].reshape(rows, d)
        h = _modulated_norm(x, mod, n1g_ref, 0).astype(bf16)
        a = _dot(kw_ref[...], h).astype(bf16)
        tw_c = twc_ref[half]
        tw_s = tws_ref[half]
        re_cols, im_cols = [], []
        for g in range(FOURIER_GROUPS):
            sl = slice(g * gd, (g + 1) * gd)
            z = _dot(jnp.concatenate([a[:rows, sl], a[rows:, sl]], axis=1), cd_ref[...])
            z_re, z_im = z[:, :gd], z[:, gd:]
            c = jnp.tile(tw_c, (1, gd // LANES))
            s = jnp.tile(tw_s, (1, gd // LANES))
            re_cols.append(z_re * c + z_im * s)
            im_cols.append(z_im * c - z_re * s)
        halves[0].append(jnp.concatenate(re_cols, axis=1).reshape(DFT_T1, SUBLANES, d))
        halves[1].append(jnp.concatenate(im_cols, axis=1).reshape(DFT_T1, SUBLANES, d))
    for comp in range(2):
        o_ref[0, comp] = jnp.concatenate(halves[comp], axis=1).astype(bf16)


def _dft_a(x, mods, layer, n1g, tabs):
    b, t, d = x.shape
    t2 = t // DFT_T1
    k = BF16_SUBLANES
    rows = DFT_T1 * SUBLANES
    return pl.pallas_call(
        _dft_a_body,
        grid=(b, t2 // k),
        in_specs=[
            pl.BlockSpec((1, DFT_T1, k, d), lambda bi, j: (bi, 0, j, 0)),
            _mod_spec(layer, lambda bi: bi),
            _const_spec((1, d)),
            _const_spec((2 * rows, rows)),
            _const_spec((2 * FOURIER_GROUP_DIM, 2 * FOURIER_GROUP_DIM)),
            pl.BlockSpec((k // SUBLANES, rows, LANES), lambda bi, j: (j, 0, 0)),
            pl.BlockSpec((k // SUBLANES, rows, LANES), lambda bi, j: (j, 0, 0)),
        ],
        out_specs=pl.BlockSpec((1, 2, DFT_T1, k, d), lambda bi, j: (bi, 0, 0, j, 0)),
        out_shape=jax.ShapeDtypeStruct((b, 2, DFT_T1, t2, d), bf16),
        compiler_params=_cparams(("arbitrary", "arbitrary")),
        name="dft_a",
    )(x.reshape(b, DFT_T1, t2, d), mods, n1g, tabs["kw"], tabs["cd"], tabs["tw_cos"], tabs["tw_sin"])


def _dft_ffn_body(x_ref, z_ref, mod_ref, g_ref, wf_ref, n2g_ref, w1_ref, w3_ref, w2_ref, o_ref):
    d = D_MODEL
    tp = x_ref.shape[1]
    rows = tp * SUBLANES
    r0 = pl.multiple_of(pl.program_id(2) * tp, tp)
    cs = g_ref[pl.ds(r0, tp), :]
    t2 = z_ref.shape[3]
    f = jnp.stack([_dot(cs, z_ref[0, :, a].reshape(2 * t2, d)) for a in range(SUBLANES)], axis=0)
    f = jnp.swapaxes(f, 0, 1).reshape(rows, d).astype(bf16)
    out = _dot(f, wf_ref[0])
    x = x_ref[0].reshape(rows, d)
    res = _gated_ffn(x, out, mod_ref[0, 0], n2g_ref, w1_ref, w3_ref, w2_ref)
    o_ref[0] = res.reshape(tp, SUBLANES, d)


def _dft_ffn(x, z, mods, layer, tabs, wf, j, n2g, w1, w3, w2, tp):
    b, t, d = x.shape
    t2 = t // DFT_T1
    out = pl.pallas_call(
        _dft_ffn_body,
        grid=(b, DFT_T1 // SUBLANES, t2 // tp),
        in_specs=[
            pl.BlockSpec((1, tp, SUBLANES, d), lambda bi, jj, hh: (bi, hh, jj, 0)),
            pl.BlockSpec((1, 2, SUBLANES, t2, d), lambda bi, jj, hh: (bi, 0, jj, 0, 0)),
            pl.BlockSpec((1, 1, 1, 6 * d), lambda bi, jj, hh: (layer, bi, 0, 0)),
            _const_spec((t2, 2 * t2)),
            _const_spec(wf.shape, j),
            _const_spec((1, d)),
            _const_spec(w1.shape, layer),
            _const_spec(w3.shape, layer),
            _const_spec(w2.shape, layer),
        ],
        out_specs=pl.BlockSpec((1, tp, SUBLANES, d), lambda bi, jj, hh: (bi, hh, jj, 0)),
        out_shape=jax.ShapeDtypeStruct((b, t2, DFT_T1, d), f32),
        compiler_params=_cparams(("arbitrary", "arbitrary", "arbitrary")),
        name="dft_ffn",
    )(x.reshape(b, t2, DFT_T1, d), z, mods, tabs["g2"], wf, n2g, w1, w3, w2)
    return out.reshape(b, t, d)


def _angle(i, j, n):
    return (2.0 * math.pi / n) * ((i * j) % n).astype(f32)


def _dft_tables(t):
    t1, t2 = DFT_T1, t // DFT_T1
    i1 = jnp.arange(t1, dtype=jnp.int32)
    r = jnp.arange(t1 * SUBLANES, dtype=jnp.int32)
    ang64 = _angle(r[:, None] // SUBLANES, r[None, :] // SUBLANES, t1)
    keep = jnp.where(r[:, None] % SUBLANES == r[None, :] % SUBLANES, t1 ** -0.5, 0.0)
    kw = jnp.concatenate([jnp.cos(ang64) * keep, -jnp.sin(ang64) * keep], axis=0).astype(bf16)

    ic = jnp.arange(FOURIER_GROUP_DIM, dtype=jnp.int32)
    angc = _angle(ic[:, None], ic[None, :], FOURIER_GROUP_DIM)
    sc_ = FOURIER_GROUP_DIM ** -0.5
    cc, sn = jnp.cos(angc) * sc_, jnp.sin(angc) * sc_
    cd = jnp.concatenate([jnp.concatenate([cc, -sn], axis=1),
                          jnp.concatenate([sn, cc], axis=1)], axis=0).astype(bf16)

    i2 = jnp.arange(t2, dtype=jnp.int32)
    angt = _angle(i2[:, None], i1[None, :], t)
    angt = angt.reshape(t2 // SUBLANES, SUBLANES, t1).transpose(0, 2, 1).reshape(t2 // SUBLANES, -1)
    tw = lax.optimization_barrier((jnp.cos(angt), jnp.sin(angt)))
    tw_cos, tw_sin = (jnp.broadcast_to(v[:, :, None], v.shape + (LANES,)) for v in tw)

    ang2 = _angle(i2[:, None], i2[None, :], t2)
    s2 = t2 ** -0.5
    g2 = jnp.concatenate([jnp.cos(ang2) * s2, jnp.sin(ang2) * s2], axis=1).astype(bf16)
    return dict(kw=kw, cd=cd, tw_cos=tw_cos, tw_sin=tw_sin, g2=g2)


def _rope_tables(t):
    rows = t // GRID_W
    half = QK_ROPE_DIM // 2
    inv = 1.0 / (ROPE_BASE ** (jnp.arange(0, half, 2, dtype=f32) / half))
    ar = jnp.arange(rows, dtype=f32)[:, None] * inv
    ac = jnp.arange(GRID_W, dtype=f32)[:, None] * inv
    pad_hi = HEAD_PAD - QK_HEAD_DIM

    def place(n, first, second, at, lead):
        z = jnp.zeros((n, half), f32)
        blocks = [jnp.full((n, QK_NOPE_DIM), lead, f32)]
        blocks += [jnp.concatenate([first, second], axis=1), z] if at == 0 else \
                  [z, jnp.concatenate([first, second], axis=1)]
        blocks.append(jnp.zeros((n, pad_hi), f32))
        return jnp.concatenate(blocks, axis=1)

    rrow = jnp.stack([place(rows, jnp.cos(ar), jnp.cos(ar), 0, 0.0),
                      place(rows, -jnp.sin(ar), jnp.sin(ar), 0, 0.0)])
    rcol = jnp.stack([place(GRID_W, jnp.cos(ac), jnp.cos(ac), 1, 1.0),
                      place(GRID_W, -jnp.sin(ac), jnp.sin(ac), 1, 0.0)])
    return rrow, rcol


def _partner(a, axis):
    parts = jnp.split(a, 4, axis=axis)
    return jnp.concatenate([parts[1], parts[0], parts[3], parts[2]], axis=axis)


def _prep_even(j, w_in, q_norm_g, kv_norm_g, w_uq, w_ukv, q_gain, k_gain, w_o):
    hh, hp = MLA_HEADS, HEAD_PAD
    d = D_MODEL
    nope, hd = QK_NOPE_DIM, QK_HEAD_DIM
    wi = w_in[j]
    w_pe = wi[:, PE_COL:CONV_OFFSET]
    w_in_p = jnp.concatenate([wi[:, :PE_COL], jnp.zeros((d, nope), f32), w_pe, _partner(w_pe, 1),
                              wi[:, CONV_OFFSET:]], axis=1).astype(bf16)
    wq = w_uq[j].reshape(Q_LORA_RANK, hh, hd)
    pad_head = lambda a: jnp.pad(a, ((0, 0), (0, 0), (0, hp - hd))).reshape(Q_LORA_RANK, hh * hp)
    wq_sw = jnp.zeros_like(wq).at[:, :, nope:].set(_partner(wq[:, :, nope:], 2))
    w_uq_p = jnp.concatenate([pad_head(wq), pad_head(wq_sw)], axis=1)
    wkv = w_ukv[j].reshape(KV_LORA_RANK, hh, nope + V_HEAD_DIM)
    wk = jnp.pad(wkv[:, :, :nope], ((0, 0), (0, 0), (0, hp - nope))).reshape(KV_LORA_RANK, hh * hp)
    w_ukv_p = wk.astype(bf16)
    w_vt = wkv[:, :, nope:].reshape(KV_LORA_RANK, hh * V_HEAD_DIM).T.astype(bf16)
    w_uq_p = w_uq_p.astype(bf16)

    def gain_rows(g):
        sw = jnp.zeros_like(g).at[nope:].set(_partner(g[nope:], 0))
        return [jnp.pad(g, (0, hp - hd)), jnp.pad(sw, (0, hp - hd))]

    zero = jnp.zeros((hp,), f32)
    gains = jnp.stack(gain_rows(q_gain[j] * (QK_SCALE * LOG2E)) + gain_rows(k_gain[j]) + [zero] * 4)
    return dict(
        w_in=w_in_p, w_uq=w_uq_p, w_ukv=w_ukv_p, w_vt=w_vt,
        q_norm_g=q_norm_g[j].reshape(1, -1), kv_norm_g=kv_norm_g[j].reshape(1, -1), gains=gains,
        w_o_attn=w_o[j][:hh * V_HEAD_DIM].astype(bf16), w_o_conv=w_o[j][hh * V_HEAD_DIM:].astype(bf16))


def _pick(n, pref):
    return pref if n % pref == 0 else n


def kernel(x, c, ctx, c_ctx, ada_w, ada_b, norm1_g, norm2_g, w_in, q_norm_g, kv_norm_g, w_uq, w_ukv,
           q_gain, k_gain, conv_w, w_o, w_fourier, ffn_w1, ffn_w3, ffn_w2):
    b, s, d = x.shape
    depth = ada_w.shape[0]
    cvec = jnp.zeros((8, d), f32).at[:b].set(c).at[b].set(c_ctx)
    mods = _ada(cvec, ada_w, ada_b).reshape(depth, 8, 1, 6 * d)
    lat_row = lambda bi: bi
    ctx_row = lambda bi: b
    w1 = ffn_w1.astype(bf16)
    w3 = ffn_w3.astype(bf16)
    w2 = ffn_w2.astype(bf16)
    wf = w_fourier.astype(bf16)
    for i in range(depth):
        last = i == depth - 1
        j = i // 2
        n1g = norm1_g[i].reshape(1, d)
        n2g = norm2_g[i].reshape(1, d)
        if i % 2 == 0:
            w = _prep_even(j, w_in, q_norm_g, kv_norm_g, w_uq, w_ukv, q_gain, k_gain, w_o)
            tm = _pick(s, 1024)
            lc = ctx.shape[1]
            q_l, k_l, vt_l, pc_l = _qkv(x, mods, i, lat_row, n1g, w, _rope_tables(s), tm)
            q_c, k_c, vt_c, pc_c = _qkv(ctx, mods, i, ctx_row, n1g, w, None, lc)
            bound = (QK_SCALE * LOG2E * QK_HEAD_DIM * BF16_ROUNDING_SLACK
                     * jnp.max(jnp.abs(q_gain[j])) * jnp.max(jnp.abs(k_gain[j])))
            tq, tk = _pick(s, 2048), _pick(s, 1024)
            a_l = lax.cond(
                bound < EXP2_SAFE_SCORE,
                lambda q_, kl, vtl, kc, vtc: _attention(q_, [(kl, vtl), (kc, vtc)], tq, tk, online=False),
                lambda q_, kl, vtl, kc, vtc: _attention(q_, [(kl, vtl), (kc, vtc)], tm, tk, online=True),
                q_l, k_l, vt_l, k_c, vt_c)
            x_new = _mix_ffn(x, a_l, pc_l, mods, i, lat_row, conv_w, j, w, n2g, w1, w3, w2, tm)
            if not last:
                a_c = _attention(q_c, [(k_c, vt_c)], lc, lc, online=True)
                ctx = _mix_ffn(ctx, a_c, pc_c, mods, i, ctx_row, conv_w, j, w, n2g, w1, w3, w2, lc)
            x = x_new
        else:
            tabs = _dft_tables(s)
            t2 = s // DFT_T1
            z = _dft_a(x, mods, i, n1g, tabs)
            x = _dft_ffn(x, z, mods, i, tabs, wf, j, n2g, w1, w3, w2, _pick(t2, 128))
            assert last, "odd non-final layers are not implemented"
    return x
```

```python
import functools
import math

import jax
import jax.numpy as jnp
from jax import lax
from jax.experimental import pallas as pl
from jax.experimental.pallas import tpu as pltpu

D_MODEL = 1024
GRID_W = 64
MLA_HEADS = 8
QK_NOPE_DIM = 64
QK_ROPE_DIM = 32
QK_HEAD_DIM = QK_NOPE_DIM + QK_ROPE_DIM
V_HEAD_DIM = 64
Q_LORA_RANK = 384
KV_LORA_RANK = 256
QK_SCALE = QK_HEAD_DIM ** -0.5
ROPE_BASE = 10000.0
CONV_DIM = 512
CONV_OFFSET = Q_LORA_RANK + KV_LORA_RANK + QK_ROPE_DIM
FOURIER_GROUPS = 4
FOURIER_GROUP_DIM = D_MODEL // FOURIER_GROUPS
EPS = 1e-6

LANES = 128
BF16_SUBLANES = 16
MXU_TILE = 256
VMEM_LIMIT_BYTES = 60 * 1024 * 1024

HEAD_PAD = LANES
ALL_HEADS = MLA_HEADS * HEAD_PAD
PE_COL = Q_LORA_RANK + KV_LORA_RANK
CONV_COL = PE_COL + HEAD_PAD
IN_PROJ_PAD = CONV_COL + 3 * CONV_DIM
QKV_ROW_BLOCK = 128
QKV_SUB_ROWS = 256
DFT_T1 = 64
LOG2E = math.log2(math.e)
EXP2_SAFE_SCORE = 64.0
BF16_ROUNDING_SLACK = 1.02
NT_DIMS = (((1,), (1,)), ((), ()))
V_T_ROWS = V_HEAD_DIM + BF16_SUBLANES

bf16 = jnp.bfloat16
f32 = jnp.float32


def _cparams(sem):
    return pltpu.CompilerParams(dimension_semantics=sem, vmem_limit_bytes=VMEM_LIMIT_BYTES)


def _const_spec(shape, lead=None):
    if lead is None:
        nd = len(shape)
        return pl.BlockSpec(shape, lambda *_: (0,) * nd, pipeline_mode=pl.Buffered(1))
    nd = len(shape) - 1
    return pl.BlockSpec((1,) + tuple(shape[1:]), lambda *_: (lead,) + (0,) * nd,
                        pipeline_mode=pl.Buffered(1))


def _mod_spec(layer, mod_row):
    return pl.BlockSpec((1, 1, 1, 6 * D_MODEL), lambda bi, i: (layer, mod_row(bi), 0, 0))


def _dot(a, b):
    return jnp.dot(a, b, preferred_element_type=f32)


def _rms_scale(x, width):
    return lax.rsqrt(jnp.sum(x * x, axis=-1, keepdims=True) * (1.0 / width) + EPS)


def _ada_body(c_ref, w_ref, b_ref, o_ref):
    c = c_ref[...]
    s = c / (1.0 + jnp.exp(-c))
    o_ref[0] = _dot(s.astype(bf16), w_ref[0].astype(bf16)) + b_ref[0]


def _ada(cvec, ada_w, ada_b):
    depth, d, n = ada_w.shape
    tn = 1536
    return pl.pallas_call(
        _ada_body,
        grid=(depth, n // tn),
        in_specs=[
            pl.BlockSpec((8, d), lambda l, j: (0, 0)),
            pl.BlockSpec((1, d, tn), lambda l, j: (l, 0, j)),
            pl.BlockSpec((1, 1, tn), lambda l, j: (l, 0, j)),
        ],
        out_specs=pl.BlockSpec((1, 8, tn), lambda l, j: (l, 0, j)),
        out_shape=jax.ShapeDtypeStruct((depth, 8, n), f32),
        compiler_params=_cparams(("arbitrary", "arbitrary")),
        name="ada",
    )(cvec, ada_w, ada_b.reshape(depth, 1, n))


def _modulated_norm(x, mod, g_ref, lo):
    d = D_MODEL
    shift = mod[:, lo:lo + d]
    scale = mod[:, lo + d:lo + 2 * d]
    gain = g_ref[...] * (1.0 + scale)
    return (x * _rms_scale(x, d)) * gain + shift


def _qkv_body(*refs, rope):
    (x_ref, mod_ref, n1g_ref, win_ref, qng_ref, kvng_ref, wuq_ref, wukv_ref, wvt_ref,
     gains_ref) = refs[:10]
    if rope:
        rrow_ref, rcol_ref = refs[10:12]
        refs = refs[12:]
    else:
        refs = refs[10:]
    q_out, k_out, vt_out, pc_out, q_scr, kv_scr, pe_scr = refs
    tm = x_ref.shape[1]
    mod = mod_ref[0, 0]
    gains = gains_ref[...]
    lane = lax.broadcasted_iota(jnp.int32, (1, HEAD_PAD), 1)
    real = (lane < QK_HEAD_DIM).astype(f32)
    rb = min(QKV_ROW_BLOCK, tm)
    per = rb // GRID_W
    grid_row0 = pl.program_id(1) * (tm // GRID_W)
    to_rope = HEAD_PAD - QK_ROPE_DIM
    sub = min(QKV_SUB_ROWS, tm)

    def qk_scale(t):
        return lax.rsqrt(jnp.sum(t * t * real, axis=-1, keepdims=True) * (1.0 / QK_HEAD_DIM) + EPS)

    def project(lo):
        rows = slice(lo, lo + sub)
        h = _modulated_norm(x_ref[0, rows, :], mod, n1g_ref, 0)
        p = _dot(h.astype(bf16), win_ref[...])
        pc_out[0, rows, :] = p[:, CONV_COL:].astype(bf16)
        pe_scr[rows, :] = p[:, PE_COL:CONV_COL]
        cq = p[:, :Q_LORA_RANK]
        cqn = cq * _rms_scale(cq, Q_LORA_RANK) * qng_ref[...]
        q_scr[rows, :] = _dot(cqn.astype(bf16), wuq_ref[...])
        ckv = p[:, Q_LORA_RANK:PE_COL]
        ckvn = (ckv * _rms_scale(ckv, KV_LORA_RANK) * kvng_ref[...]).astype(bf16)
        kv_scr[rows, :] = _dot(ckvn, wukv_ref[...])
        vt = lax.dot_general(wvt_ref[...], ckvn, NT_DIMS, preferred_element_type=f32)
        ones_rows = (lax.broadcasted_iota(jnp.int32, (V_T_ROWS - V_HEAD_DIM, sub), 0) == 0).astype(f32)
        for hd in range(MLA_HEADS):
            vt_h = vt[hd * V_HEAD_DIM:(hd + 1) * V_HEAD_DIM]
            vt_out[0, hd, :, rows] = jnp.concatenate([vt_h, ones_rows], axis=0).astype(bf16)

    def finish(ib):
        rows = slice(ib * rb, (ib + 1) * rb)
        if rope:
            tabs = []
            for comp in range(2):
                parts = []
                for g in range(per):
                    rr = rrow_ref[comp, pl.ds(grid_row0 + ib * per + g, 1), :]
                    parts.append(rcol_ref[comp] + rr)
                tabs.append(jnp.concatenate(parts, axis=0))
            cos, sin = tabs
            qa, qb = cos * gains[0:1], sin * gains[1:2]
            ka, kb = cos * gains[2:3], sin * gains[3:4]
        pe = pe_scr[rows, :]
        if rope:
            pe_rot = pltpu.roll(pe, to_rope, axis=1) * kb
        for hd in range(MLA_HEADS):
            sl = slice(hd * HEAD_PAD, (hd + 1) * HEAD_PAD)
            qh = q_scr[rows, sl]
            kh = kv_scr[rows, sl] + pe
            rq = qk_scale(qh)
            rk = qk_scale(kh)
            if rope:
                partner = slice(ALL_HEADS + hd * HEAD_PAD, ALL_HEADS + (hd + 1) * HEAD_PAD)
                qo = (qh * qa + q_scr[rows, partner] * qb) * rq
                ko = (kh * ka + pe_rot) * rk
            else:
                qo = qh * gains[0:1] * rq
                ko = kh * gains[2:3] * rk
            q_out[0, hd, rows, :] = qo.astype(bf16)
            k_out[0, hd, rows, :] = ko.astype(bf16)

    for st in range(tm // sub):
        project(st * sub)
        for ib in range(st * sub // rb, (st + 1) * sub // rb):
            finish(ib)


def _qkv(x, mods, layer, mod_row, n1g, w, rope_tabs, tm):
    b, t, d = x.shape
    hh, hp = MLA_HEADS, HEAD_PAD
    rope = rope_tabs is not None
    in_specs = [
        pl.BlockSpec((1, tm, d), lambda bi, i: (bi, i, 0)),
        _mod_spec(layer, mod_row),
        _const_spec((1, d)),
        _const_spec(w["w_in"].shape),
        _const_spec((1, Q_LORA_RANK)),
        _const_spec((1, KV_LORA_RANK)),
        _const_spec(w["w_uq"].shape),
        _const_spec(w["w_ukv"].shape),
        _const_spec(w["w_vt"].shape),
        _const_spec((8, hp)),
    ]
    args = [x, mods, n1g, w["w_in"], w["q_norm_g"], w["kv_norm_g"], w["w_uq"], w["w_ukv"], w["w_vt"],
            w["gains"]]
    if rope:
        rrow, rcol = rope_tabs
        in_specs += [_const_spec(rrow.shape), _const_spec(rcol.shape)]
        args += [rrow, rcol]
    head_spec = pl.BlockSpec((1, hh, tm, hp), lambda bi, i: (bi, 0, i, 0))
    head_shape = jax.ShapeDtypeStruct((b, hh, t, hp), bf16)
    return pl.pallas_call(
        functools.partial(_qkv_body, rope=rope),
        grid=(b, t // tm),
        in_specs=in_specs,
        out_specs=[head_spec, head_spec,
                   pl.BlockSpec((1, hh, V_T_ROWS, tm), lambda bi, i: (bi, 0, 0, i)),
                   pl.BlockSpec((1, tm, 3 * CONV_DIM), lambda bi, i: (bi, i, 0))],
        out_shape=[head_shape, head_shape,
                   jax.ShapeDtypeStruct((b, hh, V_T_ROWS, t), bf16),
                   jax.ShapeDtypeStruct((b, t, 3 * CONV_DIM), bf16)],
        scratch_shapes=[pltpu.VMEM((tm, 2 * ALL_HEADS), f32),
                        pltpu.VMEM((tm, ALL_HEADS), f32),
                        pltpu.VMEM((tm, hp), f32)],
        compiler_params=_cparams(("arbitrary", "arbitrary")),
        name="qkv_rope" if rope else "qkv_ctx",
    )(*args)


def _attn_body(q_ref, *refs, chunks, online):
    o_ref = refs[-1]
    k_refs, vt_refs = refs[0:-1:2], refs[1:-1:2]
    outs = []
    for par in range(2):
        q = q_ref[0, par]
        m = None
        acc = None
        for src, (n_chunks, tk) in enumerate(chunks):
            for c in range(n_chunks):
                ks = slice(c * tk, (c + 1) * tk)
                st = lax.dot_general(k_refs[src][0, par, ks, :], q, NT_DIMS, preferred_element_type=f32)
                if online:
                    m_cur = jnp.max(st, axis=0, keepdims=True)
                    m_new = m_cur if m is None else jnp.maximum(m, m_cur)
                    part = _dot(vt_refs[src][0, par, :, ks], jnp.exp2(st - m_new).astype(bf16))
                    acc = part if m is None else jnp.exp2(m - m_new) * acc + part
                    m = m_new
                else:
                    part = _dot(vt_refs[src][0, par, :, ks], jnp.exp2(st).astype(bf16))
                    acc = part if acc is None else acc + part
        outs.append(acc[:V_HEAD_DIM] / acc[V_HEAD_DIM:V_HEAD_DIM + 1])
    o_ref[0] = jnp.concatenate(outs, axis=0).T.astype(bf16)


def _attention(q, kv_sources, tq, tk, online):
    b, hh, t, hp = q.shape
    in_specs = [pl.BlockSpec((1, 2, tq, hp), lambda bi, pi, i: (bi, pi, i, 0))]
    args = [q]
    chunks = []
    for k, vt in kv_sources:
        n = k.shape[2]
        step = min(tk, n)
        chunks.append((n // step, step))
        in_specs += [pl.BlockSpec((1, 2, n, hp), lambda bi, pi, i: (bi, pi, 0, 0)),
                     pl.BlockSpec((1, 2, V_T_ROWS, n), lambda bi, pi, i: (bi, pi, 0, 0))]
        args += [k, vt]
    return pl.pallas_call(
        functools.partial(_attn_body, chunks=tuple(chunks), online=online),
        grid=(b, hh // 2, t // tq),
        in_specs=in_specs,
        out_specs=pl.BlockSpec((1, tq, hp), lambda bi, pi, i: (bi, i, pi)),
        out_shape=jax.ShapeDtypeStruct((b, t, hh * V_HEAD_DIM), bf16),
        compiler_params=_cparams(("arbitrary", "arbitrary", "arbitrary")),
        name="attn%d%s" % (len(kv_sources), "_online" if online else "_bounded"),
    )(*args)


WEIGHT_CAST_CHUNK = 256


def _ffn_weight_scratch(d, d_ff):
    return [pltpu.VMEM((d, d_ff), bf16), pltpu.VMEM((d, d_ff), bf16), pltpu.VMEM((d_ff, d), bf16),
            pltpu.VMEM((d, WEIGHT_CAST_CHUNK), f32), pltpu.VMEM((WEIGHT_CAST_CHUNK, d), f32)]


def _cast_ffn_weights(layer, w1_hbm, w3_hbm, w2_hbm, w1_ref, w3_ref, w2_ref, cols_ref, rows_ref):
    step = WEIGHT_CAST_CHUNK
    d_ff = w1_ref.shape[1]
    for lo in range(0, d_ff, step):
        for src, dst in ((w1_hbm, w1_ref), (w3_hbm, w3_ref)):
            pltpu.sync_copy(src.at[layer, :, pl.ds(lo, step)], cols_ref)
            dst[:, lo:lo + step] = cols_ref[...].astype(bf16)
        pltpu.sync_copy(w2_hbm.at[layer, pl.ds(lo, step), :], rows_ref)
        w2_ref[lo:lo + step, :] = rows_ref[...].astype(bf16)


def _gated_ffn(x, out, mod, n2g_ref, w1_ref, w3_ref, w2_ref):
    d = D_MODEL
    x1 = x + mod[:, 2 * d:3 * d] * out
    h2 = _modulated_norm(x1, mod, n2g_ref, 3 * d).astype(bf16)
    y = None
    for lo, hi in _ff_chunks(w1_ref.shape[1]):
        sl = slice(lo, hi)
        u = _dot(h2, w1_ref[:, sl])
        g = _dot(h2, w3_ref[:, sl])
        act = (u / (1.0 + jnp.exp(-u)) * g).astype(bf16)
        part = _dot(act, w2_ref[sl, :])
        y = part if y is None else y + part
    return x1 + mod[:, 5 * d:6 * d] * y


FFN_CHUNKS = 4


def _ff_chunks(d_ff):
    step = -(-d_ff // (FFN_CHUNKS * MXU_TILE)) * MXU_TILE
    return [(lo, min(lo + step, d_ff)) for lo in range(0, d_ff, step)]


def _mix_ffn_body(*refs, halo, layer):
    x_ref, a_ref, pc_ref = refs[:3]
    if halo:
        pprev_ref, pnext_ref = refs[3:5]
        refs = refs[5:]
    else:
        refs = refs[3:]
    (mod_ref, convw_ref, woa_ref, woc_ref, n2g_ref, w1_hbm, w3_hbm, w2_hbm, o_ref,
     w1_ref, w3_ref, w2_ref, cols_ref, rows_ref) = refs

    @pl.when((pl.program_id(0) == 0) & (pl.program_id(1) == 0))
    def _():
        _cast_ffn_weights(layer, w1_hbm, w3_hbm, w2_hbm, w1_ref, w3_ref, w2_ref, cols_ref, rows_ref)

    x = x_ref[0]
    mod = mod_ref[0, 0]
    tm = x.shape[0]

    out = _dot(a_ref[0], woa_ref[...])

    cd = CONV_DIM
    pc = pc_ref[0].astype(f32)
    z = pc[:, cd:2 * cd] * pc[:, 2 * cd:]
    row = lax.broadcasted_iota(jnp.int32, (tm, cd), 0)
    if halo:
        i = pl.program_id(1)
        last = pl.num_programs(1) - 1
        pp = pprev_ref[0, BF16_SUBLANES - 1:BF16_SUBLANES, :].astype(f32)
        pn = pnext_ref[0, 0:1, :].astype(f32)
        z_prev = jnp.where(i > 0, pp[:, cd:2 * cd] * pp[:, 2 * cd:], 0.0)
        z_next = jnp.where(i < last, pn[:, cd:2 * cd] * pn[:, 2 * cd:], 0.0)
    else:
        z_prev = jnp.zeros((1, cd), f32)
        z_next = jnp.zeros((1, cd), f32)
    z_up = jnp.where(row == 0, z_prev, pltpu.roll(z, 1, axis=0))
    z_dn = jnp.where(row == tm - 1, z_next, pltpu.roll(z, tm - 1, axis=0))
    cw = convw_ref[0]
    y = z_up * cw[0:1, :] + z * cw[1:2, :] + z_dn * cw[2:3, :]
    out = out + _dot((pc[:, :cd] * y).astype(bf16), woc_ref[...])

    o_ref[0] = _gated_ffn(x, out, mod, n2g_ref, w1_ref, w3_ref, w2_ref)


def _mix_ffn(x, a, pc, mods, layer, mod_row, conv_w, j, w, n2g, w1, w3, w2, tm):
    b, t, d = x.shape
    hh, hp = MLA_HEADS, HEAD_PAD
    halo = t > tm
    in_specs = [
        pl.BlockSpec((1, tm, d), lambda bi, i: (bi, i, 0)),
        pl.BlockSpec((1, tm, hh * V_HEAD_DIM), lambda bi, i: (bi, i, 0)),
        pl.BlockSpec((1, tm, 3 * CONV_DIM), lambda bi, i: (bi, i, 0)),
    ]
    args = [x, a, pc]
    if halo:
        per = tm // BF16_SUBLANES
        nblk = t // BF16_SUBLANES
        in_specs += [
            pl.BlockSpec((1, BF16_SUBLANES, 3 * CONV_DIM),
                         lambda bi, i: (bi, jnp.maximum(i * per - 1, 0), 0)),
            pl.BlockSpec((1, BF16_SUBLANES, 3 * CONV_DIM),
                         lambda bi, i: (bi, jnp.minimum((i + 1) * per, nblk - 1), 0)),
        ]
        args += [pc, pc]
    in_specs += [
        _mod_spec(layer, mod_row),
        _const_spec(conv_w.shape, j),
        _const_spec(w["w_o_attn"].shape),
        _const_spec(w["w_o_conv"].shape),
        _const_spec((1, d)),
        pl.BlockSpec(memory_space=pl.ANY),
        pl.BlockSpec(memory_space=pl.ANY),
        pl.BlockSpec(memory_space=pl.ANY),
    ]
    args += [mods, conv_w, w["w_o_attn"], w["w_o_conv"], n2g, w1, w3, w2]
    return pl.pallas_call(
        functools.partial(_mix_ffn_body, halo=halo, layer=layer),
        grid=(b, t // tm),
        in_specs=in_specs,
        out_specs=pl.BlockSpec((1, tm, d), lambda bi, i: (bi, i, 0)),
        out_shape=jax.ShapeDtypeStruct((b, t, d), f32),
        scratch_shapes=_ffn_weight_scratch(d, w1.shape[2]),
        compiler_params=_cparams(("arbitrary", "arbitrary")),
        name="mix_ffn_halo" if halo else "mix_ffn",
    )(*args)


SUBLANES = 8


def _dft_a_body(x_ref, mod_ref, n1g_ref, kw_ref, cd_ref, twc_ref, tws_ref, o_ref):
    d = D_MODEL
    gd = FOURIER_GROUP_DIM
    rows = DFT_T1 * SUBLANES
    mod = mod_ref[0, 0]
    halves = [[], []]
    for half in range(x_ref.shape[2] // SUBLANES):
        sub = slice(half * SUBLANES, (half + 1) * SUBLANES)
        x = x_ref[0, :, sub, :].reshape(rows, d)
        h = _modulated_norm(x, mod, n1g_ref, 0).astype(bf16)
        a = _dot(kw_ref[...], h).astype(bf16)
        tw_c = twc_ref[half]
        tw_s = tws_ref[half]
        re_cols, im_cols = [], []
        for g in range(FOURIER_GROUPS):
            sl = slice(g * gd, (g + 1) * gd)
            z = _dot(jnp.concatenate([a[:rows, sl], a[rows:, sl]], axis=1), cd_ref[...])
            z_re, z_im = z[:, :gd], z[:, gd:]
            c = jnp.tile(tw_c, (1, gd // LANES))
            s = jnp.tile(tw_s, (1, gd // LANES))
            re_cols.append(z_re * c + z_im * s)
            im_cols.append(z_im * c - z_re * s)
        halves[0].append(jnp.concatenate(re_cols, axis=1).reshape(DFT_T1, SUBLANES, d))
        halves[1].append(jnp.concatenate(im_cols, axis=1).reshape(DFT_T1, SUBLANES, d))
    for comp in range(2):
        o_ref[0, comp] = jnp.concatenate(halves[comp], axis=1).astype(bf16)


def _dft_a(x, mods, layer, n1g, tabs):
    b, t, d = x.shape
    t2 = t // DFT_T1
    k = BF16_SUBLANES
    rows = DFT_T1 * SUBLANES
    return pl.pallas_call(
        _dft_a_body,
        grid=(b, t2 // k),
        in_specs=[
            pl.BlockSpec((1, DFT_T1, k, d), lambda bi, j: (bi, 0, j, 0)),
            _mod_spec(layer, lambda bi: bi),
            _const_spec((1, d)),
            _const_spec((2 * rows, rows)),
            _const_spec((2 * FOURIER_GROUP_DIM, 2 * FOURIER_GROUP_DIM)),
            pl.BlockSpec((k // SUBLANES, rows, LANES), lambda bi, j: (j, 0, 0)),
            pl.BlockSpec((k // SUBLANES, rows, LANES), lambda bi, j: (j, 0, 0)),
        ],
        out_specs=pl.BlockSpec((1, 2, DFT_T1, k, d), lambda bi, j: (bi, 0, 0, j, 0)),
        out_shape=jax.ShapeDtypeStruct((b, 2, DFT_T1, t2, d), bf16),
        compiler_params=_cparams(("arbitrary", "arbitrary")),
        name="dft_a",
    )(x.reshape(b, DFT_T1, t2, d), mods, n1g, tabs["kw"], tabs["cd"], tabs["tw_cos"], tabs["tw_sin"])


def _dft_ffn_body(x_ref, z_ref, mod_ref, g_ref, wf_ref, n2g_ref, w1_hbm, w3_hbm, w2_hbm, o_ref,
                  w1_ref, w3_ref, w2_ref, cols_ref, rows_ref, *, layer):
    @pl.when((pl.program_id(0) == 0) & (pl.program_id(1) == 0) & (pl.program_id(2) == 0))
    def _():
        _cast_ffn_weights(layer, w1_hbm, w3_hbm, w2_hbm, w1_ref, w3_ref, w2_ref, cols_ref, rows_ref)

    d = D_MODEL
    tp = x_ref.shape[1]
    rows = tp * SUBLANES
    r0 = pl.multiple_of(pl.program_id(2) * tp, tp)
    cs = g_ref[pl.ds(r0, tp), :]
    t2 = z_ref.shape[3]
    f = jnp.stack([_dot(cs, z_ref[0, :, a].reshape(2 * t2, d)) for a in range(SUBLANES)], axis=0)
    f = jnp.swapaxes(f, 0, 1).reshape(rows, d).astype(bf16)
    out = _dot(f, wf_ref[0])
    x = x_ref[0].reshape(rows, d)
    res = _gated_ffn(x, out, mod_ref[0, 0], n2g_ref, w1_ref, w3_ref, w2_ref)
    o_ref[0] = res.reshape(tp, SUBLANES, d)


def _dft_ffn(x, z, mods, layer, tabs, wf, j, n2g, w1, w3, w2, tp):
    b, t, d = x.shape
    t2 = t // DFT_T1
    out = pl.pallas_call(
        functools.partial(_dft_ffn_body, layer=layer),
        grid=(b, DFT_T1 // SUBLANES, t2 // tp),
        in_specs=[
            pl.BlockSpec((1, tp, SUBLANES, d), lambda bi, jj, hh: (bi, hh, jj, 0)),
            pl.BlockSpec((1, 2, SUBLANES, t2, d), lambda bi, jj, hh: (bi, 0, jj, 0, 0)),
            pl.BlockSpec((1, 1, 1, 6 * d), lambda bi, jj, hh: (layer, bi, 0, 0)),
            _const_spec((t2, 2 * t2)),
            _const_spec(wf.shape, j),
            _const_spec((1, d)),
            pl.BlockSpec(memory_space=pl.ANY),
            pl.BlockSpec(memory_space=pl.ANY),
            pl.BlockSpec(memory_space=pl.ANY),
        ],
        out_specs=pl.BlockSpec((1, tp, SUBLANES, d), lambda bi, jj, hh: (bi, hh, jj, 0)),
        out_shape=jax.ShapeDtypeStruct((b, t2, DFT_T1, d), f32),
        scratch_shapes=_ffn_weight_scratch(d, w1.shape[2]),
        compiler_params=_cparams(("arbitrary", "arbitrary", "arbitrary")),
        name="dft_ffn",
    )(x.reshape(b, t2, DFT_T1, d), z, mods, tabs["g2"], wf, n2g, w1, w3, w2)
    return out.reshape(b, t, d)


def _angle(i, j, n):
    return (2.0 * math.pi / n) * ((i * j) % n).astype(f32)


def _dft_tables(t):
    t1, t2 = DFT_T1, t // DFT_T1
    i1 = jnp.arange(t1, dtype=jnp.int32)
    r = jnp.arange(t1 * SUBLANES, dtype=jnp.int32)
    ang64 = _angle(r[:, None] // SUBLANES, r[None, :] // SUBLANES, t1)
    keep = jnp.where(r[:, None] % SUBLANES == r[None, :] % SUBLANES, t1 ** -0.5, 0.0)
    kw = jnp.concatenate([jnp.cos(ang64) * keep, -jnp.sin(ang64) * keep], axis=0).astype(bf16)

    ic = jnp.arange(FOURIER_GROUP_DIM, dtype=jnp.int32)
    angc = _angle(ic[:, None], ic[None, :], FOURIER_GROUP_DIM)
    sc_ = FOURIER_GROUP_DIM ** -0.5
    cc, sn = jnp.cos(angc) * sc_, jnp.sin(angc) * sc_
    cd = jnp.concatenate([jnp.concatenate([cc, -sn], axis=1),
                          jnp.concatenate([sn, cc], axis=1)], axis=0).astype(bf16)

    i2 = jnp.arange(t2, dtype=jnp.int32)
    angt = _angle(i2[:, None], i1[None, :], t)
    angt = angt.reshape(t2 // SUBLANES, SUBLANES, t1).transpose(0, 2, 1).reshape(t2 // SUBLANES, -1)
    tw = lax.optimization_barrier((jnp.cos(angt), jnp.sin(angt)))
    tw_cos, tw_sin = (jnp.broadcast_to(v[:, :, None], v.shape + (LANES,)) for v in tw)

    ang2 = _angle(i2[:, None], i2[None, :], t2)
    s2 = t2 ** -0.5
    g2 = jnp.concatenate([jnp.cos(ang2) * s2, jnp.sin(ang2) * s2], axis=1).astype(bf16)
    return dict(kw=kw, cd=cd, tw_cos=tw_cos, tw_sin=tw_sin, g2=g2)


def _rope_tables(t):
    rows = t // GRID_W
    half = QK_ROPE_DIM // 2
    inv = 1.0 / (ROPE_BASE ** (jnp.arange(0, half, 2, dtype=f32) / half))
    ar = jnp.arange(rows, dtype=f32)[:, None] * inv
    ac = jnp.arange(GRID_W, dtype=f32)[:, None] * inv
    pad_hi = HEAD_PAD - QK_HEAD_DIM

    def place(n, first, second, at, lead):
        z = jnp.zeros((n, half), f32)
        blocks = [jnp.full((n, QK_NOPE_DIM), lead, f32)]
        blocks += [jnp.concatenate([first, second], axis=1), z] if at == 0 else \
                  [z, jnp.concatenate([first, second], axis=1)]
        blocks.append(jnp.zeros((n, pad_hi), f32))
        return jnp.concatenate(blocks, axis=1)

    rrow = jnp.stack([place(rows, jnp.cos(ar), jnp.cos(ar), 0, 0.0),
                      place(rows, -jnp.sin(ar), jnp.sin(ar), 0, 0.0)])
    rcol = jnp.stack([place(GRID_W, jnp.cos(ac), jnp.cos(ac), 1, 1.0),
                      place(GRID_W, -jnp.sin(ac), jnp.sin(ac), 1, 0.0)])
    return rrow, rcol


def _partner(a, axis):
    parts = jnp.split(a, 4, axis=axis)
    return jnp.concatenate([parts[1], parts[0], parts[3], parts[2]], axis=axis)


def _prep_even(j, w_in, q_norm_g, kv_norm_g, w_uq, w_ukv, q_gain, k_gain, w_o):
    hh, hp = MLA_HEADS, HEAD_PAD
    d = D_MODEL
    nope, hd = QK_NOPE_DIM, QK_HEAD_DIM
    wi = w_in[j]
    w_pe = wi[:, PE_COL:CONV_OFFSET]
    w_in_p = jnp.concatenate([wi[:, :PE_COL], jnp.zeros((d, nope), f32), w_pe, _partner(w_pe, 1),
                              wi[:, CONV_OFFSET:]], axis=1).astype(bf16)
    wq = w_uq[j].reshape(Q_LORA_RANK, hh, hd)
    pad_head = lambda a: jnp.pad(a, ((0, 0), (0, 0), (0, hp - hd))).reshape(Q_LORA_RANK, hh * hp)
    wq_sw = jnp.zeros_like(wq).at[:, :, nope:].set(_partner(wq[:, :, nope:], 2))
    w_uq_p = jnp.concatenate([pad_head(wq), pad_head(wq_sw)], axis=1)
    wkv = w_ukv[j].reshape(KV_LORA_RANK, hh, nope + V_HEAD_DIM)
    wk = jnp.pad(wkv[:, :, :nope], ((0, 0), (0, 0), (0, hp - nope))).reshape(KV_LORA_RANK, hh * hp)
    w_ukv_p = wk.astype(bf16)
    w_vt = wkv[:, :, nope:].reshape(KV_LORA_RANK, hh * V_HEAD_DIM).T.astype(bf16)
    w_uq_p = w_uq_p.astype(bf16)

    def gain_rows(g):
        sw = jnp.zeros_like(g).at[nope:].set(_partner(g[nope:], 0))
        return [jnp.pad(g, (0, hp - hd)), jnp.pad(sw, (0, hp - hd))]

    zero = jnp.zeros((hp,), f32)
    gains = jnp.stack(gain_rows(q_gain[j] * (QK_SCALE * LOG2E)) + gain_rows(k_gain[j]) + [zero] * 4)
    return dict(
        w_in=w_in_p, w_uq=w_uq_p, w_ukv=w_ukv_p, w_vt=w_vt,
        q_norm_g=q_norm_g[j].reshape(1, -1), kv_norm_g=kv_norm_g[j].reshape(1, -1), gains=gains,
        w_o_attn=w_o[j][:hh * V_HEAD_DIM].astype(bf16), w_o_conv=w_o[j][hh * V_HEAD_DIM:].astype(bf16))


def _pick(n, pref):
    return pref if n % pref == 0 else n


def kernel(x, c, ctx, c_ctx, ada_w, ada_b, norm1_g, norm2_g, w_in, q_norm_g, kv_norm_g, w_uq, w_ukv,
           q_gain, k_gain, conv_w, w_o, w_fourier, ffn_w1, ffn_w3, ffn_w2):
    b, s, d = x.shape
    depth = ada_w.shape[0]
    cvec = jnp.zeros((8, d), f32).at[:b].set(c).at[b].set(c_ctx)
    mods = _ada(cvec, ada_w, ada_b).reshape(depth, 8, 1, 6 * d)
    lat_row = lambda bi: bi
    ctx_row = lambda bi: b
    w1, w3, w2 = ffn_w1, ffn_w3, ffn_w2
    wf = w_fourier.astype(bf16)
    for i in range(depth):
        last = i == depth - 1
        j = i // 2
        n1g = norm1_g[i].reshape(1, d)
        n2g = norm2_g[i].reshape(1, d)
        if i % 2 == 0:
            w = _prep_even(j, w_in, q_norm_g, kv_norm_g, w_uq, w_ukv, q_gain, k_gain, w_o)
            tm = _pick(s, 1024)
            lc = ctx.shape[1]
            q_l, k_l, vt_l, pc_l = _qkv(x, mods, i, lat_row, n1g, w, _rope_tables(s), tm)
            q_c, k_c, vt_c, pc_c = _qkv(ctx, mods, i, ctx_row, n1g, w, None, lc)
            bound = (QK_SCALE * LOG2E * QK_HEAD_DIM * BF16_ROUNDING_SLACK
                     * jnp.max(jnp.abs(q_gain[j])) * jnp.max(jnp.abs(k_gain[j])))
            tq, tk = _pick(s, 2048), _pick(s, 1024)
            a_l = lax.cond(
                bound < EXP2_SAFE_SCORE,
                lambda q_, kl, vtl, kc, vtc: _attention(q_, [(kl, vtl), (kc, vtc)], tq, tk, online=False),
                lambda q_, kl, vtl, kc, vtc: _attention(q_, [(kl, vtl), (kc, vtc)], tm, tk, online=True),
                q_l, k_l, vt_l, k_c, vt_c)
            x_new = _mix_ffn(x, a_l, pc_l, mods, i, lat_row, conv_w, j, w, n2g, w1, w3, w2, tm)
            if not last:
                a_c = _attention(q_c, [(k_c, vt_c)], lc, lc, online=True)
                ctx = _mix_ffn(ctx, a_c, pc_c, mods, i, ctx_row, conv_w, j, w, n2g, w1, w3, w2, lc)
            x = x_new
        else:
            tabs = _dft_tables(s)
            t2 = s // DFT_T1
            z = _dft_a(x, mods, i, n1g, tabs)
            x = _dft_ffn(x, z, mods, i, tabs, wf, j, n2g, w1, w3, w2, _pick(t2, 128))
            assert last, "odd non-final layers are not implemented"
    return x
```

```python
import functools
import math

import jax
import jax.numpy as jnp
from jax import lax
from jax.experimental import pallas as pl
from jax.experimental.pallas import tpu as pltpu

D_MODEL = 1024
GRID_W = 64
MLA_HEADS = 8
QK_NOPE_DIM = 64
QK_ROPE_DIM = 32
QK_HEAD_DIM = QK_NOPE_DIM + QK_ROPE_DIM
V_HEAD_DIM = 64
Q_LORA_RANK = 384
KV_LORA_RANK = 256
QK_SCALE = QK_HEAD_DIM ** -0.5
ROPE_BASE = 10000.0
CONV_DIM = 512
CONV_OFFSET = Q_LORA_RANK + KV_LORA_RANK + QK_ROPE_DIM
FOURIER_GROUPS = 4
FOURIER_GROUP_DIM = D_MODEL // FOURIER_GROUPS
EPS = 1e-6

LANES = 128
BF16_SUBLANES = 16
MXU_TILE = 256
VMEM_LIMIT_BYTES = 60 * 1024 * 1024

HEAD_PAD = LANES
ALL_HEADS = MLA_HEADS * HEAD_PAD
PE_COL = Q_LORA_RANK + KV_LORA_RANK
CONV_COL = PE_COL + HEAD_PAD
IN_PROJ_PAD = CONV_COL + 3 * CONV_DIM
QKV_ROW_BLOCK = 128
QKV_SUB_ROWS = 256
DFT_T1 = 64
LOG2E = math.log2(math.e)
EXP2_SAFE_SCORE = 64.0
BF16_ROUNDING_SLACK = 1.02
NT_DIMS = (((1,), (1,)), ((), ()))
V_T_ROWS = V_HEAD_DIM + BF16_SUBLANES

bf16 = jnp.bfloat16
f32 = jnp.float32


def _cparams(sem):
    return pltpu.CompilerParams(dimension_semantics=sem, vmem_limit_bytes=VMEM_LIMIT_BYTES)


def _const_spec(shape, lead=None):
    if lead is None:
        nd = len(shape)
        return pl.BlockSpec(shape, lambda *_: (0,) * nd, pipeline_mode=pl.Buffered(1))
    nd = len(shape) - 1
    return pl.BlockSpec((1,) + tuple(shape[1:]), lambda *_: (lead,) + (0,) * nd,
                        pipeline_mode=pl.Buffered(1))


def _mod_spec(layer, mod_row):
    return pl.BlockSpec((1, 1, 1, 6 * D_MODEL), lambda bi, i: (layer, mod_row(bi), 0, 0))


def _dot(a, b):
    return jnp.dot(a, b, preferred_element_type=f32)


def _rms_scale(x, width):
    return lax.rsqrt(jnp.sum(x * x, axis=-1, keepdims=True) * (1.0 / width) + EPS)


def _ada_body(c_ref, w_ref, b_ref, o_ref):
    c = c_ref[...]
    s = c / (1.0 + jnp.exp(-c))
    o_ref[0] = _dot(s.astype(bf16), w_ref[0].astype(bf16)) + b_ref[0]


def _ada(cvec, ada_w, ada_b):
    depth, d, n = ada_w.shape
    tn = 1536
    return pl.pallas_call(
        _ada_body,
        grid=(depth, n // tn),
        in_specs=[
            pl.BlockSpec((8, d), lambda l, j: (0, 0)),
            pl.BlockSpec((1, d, tn), lambda l, j: (l, 0, j)),
            pl.BlockSpec((1, 1, tn), lambda l, j: (l, 0, j)),
        ],
        out_specs=pl.BlockSpec((1, 8, tn), lambda l, j: (l, 0, j)),
        out_shape=jax.ShapeDtypeStruct((depth, 8, n), f32),
        compiler_params=_cparams(("arbitrary", "arbitrary")),
        name="ada",
    )(cvec, ada_w, ada_b.reshape(depth, 1, n))


def _modulated_norm(x, mod, g_ref, lo):
    d = D_MODEL
    shift = mod[:, lo:lo + d]
    scale = mod[:, lo + d:lo + 2 * d]
    gain = g_ref[...] * (1.0 + scale)
    return (x * _rms_scale(x, d)) * gain + shift


def _qkv_body(*refs, rope):
    (x_ref, mod_ref, n1g_ref, win_ref, qng_ref, kvng_ref, wuq_ref, wukv_ref, wvt_ref,
     gains_ref) = refs[:10]
    if rope:
        rrow_ref, rcol_ref = refs[10:12]
        refs = refs[12:]
    else:
        refs = refs[10:]
    q_out, k_out, vt_out, pc_out, q_scr, kv_scr, pe_scr = refs
    tm = x_ref.shape[1]
    mod = mod_ref[0, 0]
    gains = gains_ref[...]
    lane = lax.broadcasted_iota(jnp.int32, (1, HEAD_PAD), 1)
    real = (lane < QK_HEAD_DIM).astype(f32)
    rb = min(QKV_ROW_BLOCK, tm)
    per = rb // GRID_W
    grid_row0 = pl.program_id(1) * (tm // GRID_W)
    to_rope = HEAD_PAD - QK_ROPE_DIM
    sub = min(QKV_SUB_ROWS, tm)

    def qk_scale(t):
        return lax.rsqrt(jnp.sum(t * t * real, axis=-1, keepdims=True) * (1.0 / QK_HEAD_DIM) + EPS)

    def project(lo):
        rows = slice(lo, lo + sub)
        h = _modulated_norm(x_ref[0, rows, :], mod, n1g_ref, 0)
        p = _dot(h.astype(bf16), win_ref[...])
        pc_out[0, rows, :] = p[:, CONV_COL:].astype(bf16)
        pe_scr[rows, :] = p[:, PE_COL:CONV_COL]
        cq = p[:, :Q_LORA_RANK]
        cqn = cq * _rms_scale(cq, Q_LORA_RANK) * qng_ref[...]
        q_scr[rows, :] = _dot(cqn.astype(bf16), wuq_ref[...])
        ckv = p[:, Q_LORA_RANK:PE_COL]
        ckvn = (ckv * _rms_scale(ckv, KV_LORA_RANK) * kvng_ref[...]).astype(bf16)
        kv_scr[rows, :] = _dot(ckvn, wukv_ref[...])
        vt = lax.dot_general(wvt_ref[...], ckvn, NT_DIMS, preferred_element_type=f32)
        ones_rows = (lax.broadcasted_iota(jnp.int32, (V_T_ROWS - V_HEAD_DIM, sub), 0) == 0).astype(f32)
        for hd in range(MLA_HEADS):
            vt_h = vt[hd * V_HEAD_DIM:(hd + 1) * V_HEAD_DIM]
            vt_out[0, hd, :, rows] = jnp.concatenate([vt_h, ones_rows], axis=0).astype(bf16)

    def finish(ib):
        rows = slice(ib * rb, (ib + 1) * rb)
        if rope:
            tabs = []
            for comp in range(2):
                parts = []
                for g in range(per):
                    rr = rrow_ref[comp, pl.ds(grid_row0 + ib * per + g, 1), :]
                    parts.append(rcol_ref[comp] + rr)
                tabs.append(jnp.concatenate(parts, axis=0))
            cos, sin = tabs
            qa, qb = cos * gains[0:1], sin * gains[1:2]
            ka, kb = cos * gains[2:3], sin * gains[3:4]
        pe = pe_scr[rows, :]
        if rope:
            pe_rot = pltpu.roll(pe, to_rope, axis=1) * kb
        for hd in range(MLA_HEADS):
            sl = slice(hd * HEAD_PAD, (hd + 1) * HEAD_PAD)
            qh = q_scr[rows, sl]
            kh = kv_scr[rows, sl] + pe
            rq = qk_scale(qh)
            rk = qk_scale(kh)
            if rope:
                partner = slice(ALL_HEADS + hd * HEAD_PAD, ALL_HEADS + (hd + 1) * HEAD_PAD)
                qo = (qh * qa + q_scr[rows, partner] * qb) * rq
                ko = (kh * ka + pe_rot) * rk
            else:
                qo = qh * gains[0:1] * rq
                ko = kh * gains[2:3] * rk
            q_out[0, hd, rows, :] = qo.astype(bf16)
            k_out[0, hd, rows, :] = ko.astype(bf16)

    for st in range(tm // sub):
        project(st * sub)
        for ib in range(st * sub // rb, (st + 1) * sub // rb):
            finish(ib)


def _qkv(x, mods, layer, mod_row, n1g, w, rope_tabs, tm):
    b, t, d = x.shape
    hh, hp = MLA_HEADS, HEAD_PAD
    rope = rope_tabs is not None
    in_specs = [
        pl.BlockSpec((1, tm, d), lambda bi, i: (bi, i, 0)),
        _mod_spec(layer, mod_row),
        _const_spec((1, d)),
        _const_spec(w["w_in"].shape),
        _const_spec((1, Q_LORA_RANK)),
        _const_spec((1, KV_LORA_RANK)),
        _const_spec(w["w_uq"].shape),
        _const_spec(w["w_ukv"].shape),
        _const_spec(w["w_vt"].shape),
        _const_spec((8, hp)),
    ]
    args = [x, mods, n1g, w["w_in"], w["q_norm_g"], w["kv_norm_g"], w["w_uq"], w["w_ukv"], w["w_vt"],
            w["gains"]]
    if rope:
        rrow, rcol = rope_tabs
        in_specs += [_const_spec(rrow.shape), _const_spec(rcol.shape)]
        args += [rrow, rcol]
    head_spec = pl.BlockSpec((1, hh, tm, hp), lambda bi, i: (bi, 0, i, 0))
    head_shape = jax.ShapeDtypeStruct((b, hh, t, hp), bf16)
    return pl.pallas_call(
        functools.partial(_qkv_body, rope=rope),
        grid=(b, t // tm),
        in_specs=in_specs,
        out_specs=[head_spec, head_spec,
                   pl.BlockSpec((1, hh, V_T_ROWS, tm), lambda bi, i: (bi, 0, 0, i)),
                   pl.BlockSpec((1, tm, 3 * CONV_DIM), lambda bi, i: (bi, i, 0))],
        out_shape=[head_shape, head_shape,
                   jax.ShapeDtypeStruct((b, hh, V_T_ROWS, t), bf16),
                   jax.ShapeDtypeStruct((b, t, 3 * CONV_DIM), bf16)],
        scratch_shapes=[pltpu.VMEM((tm, 2 * ALL_HEADS), f32),
                        pltpu.VMEM((tm, ALL_HEADS), f32),
                        pltpu.VMEM((tm, hp), f32)],
        compiler_params=_cparams(("arbitrary", "arbitrary")),
        name="qkv_rope" if rope else "qkv_ctx",
    )(*args)


def _attn_body(q_ref, *refs, chunks, online):
    o_ref = refs[-1]
    k_refs, vt_refs = refs[0:-1:2], refs[1:-1:2]
    outs = []
    for par in range(2):
        q = q_ref[0, par]
        m = None
        acc = None
        for src, (n_chunks, tk) in enumerate(chunks):
            for c in range(n_chunks):
                ks = slice(c * tk, (c + 1) * tk)
                st = lax.dot_general(k_refs[src][0, par, ks, :], q, NT_DIMS, preferred_element_type=f32)
                if online:
                    m_cur = jnp.max(st, axis=0, keepdims=True)
                    m_new = m_cur if m is None else jnp.maximum(m, m_cur)
                    part = _dot(vt_refs[src][0, par, :, ks], jnp.exp2(st - m_new).astype(bf16))
                    acc = part if m is None else jnp.exp2(m - m_new) * acc + part
                    m = m_new
                else:
                    part = _dot(vt_refs[src][0, par, :, ks], jnp.exp2(st).astype(bf16))
                    acc = part if acc is None else acc + part
        outs.append(acc[:V_HEAD_DIM] / acc[V_HEAD_DIM:V_HEAD_DIM + 1])
    o_ref[0] = jnp.concatenate(outs, axis=0).T.astype(bf16)


def _attention(q, kv_sources, tq, tk, online):
    b, hh, t, hp = q.shape
    in_specs = [pl.BlockSpec((1, 2, tq, hp), lambda bi, pi, i: (bi, pi, i, 0))]
    args = [q]
    chunks = []
    for k, vt in kv_sources:
        n = k.shape[2]
        step = min(tk, n)
        chunks.append((n // step, step))
        in_specs += [pl.BlockSpec((1, 2, n, hp), lambda bi, pi, i: (bi, pi, 0, 0)),
                     pl.BlockSpec((1, 2, V_T_ROWS, n), lambda bi, pi, i: (bi, pi, 0, 0))]
        args += [k, vt]
    return pl.pallas_call(
        functools.partial(_attn_body, chunks=tuple(chunks), online=online),
        grid=(b, hh // 2, t // tq),
        in_specs=in_specs,
        out_specs=pl.BlockSpec((1, tq, hp), lambda bi, pi, i: (bi, i, pi)),
        out_shape=jax.ShapeDtypeStruct((b, t, hh * V_HEAD_DIM), bf16),
        compiler_params=_cparams(("arbitrary", "arbitrary", "arbitrary")),
        name="attn%d%s" % (len(kv_sources), "_online" if online else "_bounded"),
    )(*args)


def _gated_ffn(x, out, mod, n2g_ref, w1_ref, w3_ref, w2_ref):
    d = D_MODEL
    x1 = x + mod[:, 2 * d:3 * d] * out
    h2 = _modulated_norm(x1, mod, n2g_ref, 3 * d).astype(bf16)
    y = None
    for lo, hi in _ff_chunks(w1_ref.shape[2]):
        sl = slice(lo, hi)
        u = _dot(h2, w1_ref[0, :, sl])
        g = _dot(h2, w3_ref[0, :, sl])
        act = (u / (1.0 + jnp.exp(-u)) * g).astype(bf16)
        part = _dot(act, w2_ref[0, sl, :])
        y = part if y is None else y + part
    return x1 + mod[:, 5 * d:6 * d] * y


FFN_CHUNKS = 4


def _ff_chunks(d_ff):
    step = -(-d_ff // (FFN_CHUNKS * MXU_TILE)) * MXU_TILE
    return [(lo, min(lo + step, d_ff)) for lo in range(0, d_ff, step)]


def _mix_ffn_body(*refs, halo):
    x_ref, a_ref, pc_ref = refs[:3]
    if halo:
        pprev_ref, pnext_ref = refs[3:5]
        refs = refs[5:]
    else:
        refs = refs[3:]
    (mod_ref, convw_ref, woa_ref, woc_ref, n2g_ref, w1_ref, w3_ref, w2_ref, o_ref) = refs
    x = x_ref[0]
    mod = mod_ref[0, 0]
    tm = x.shape[0]

    out = _dot(a_ref[0], woa_ref[...])

    cd = CONV_DIM
    pc = pc_ref[0].astype(f32)
    z = pc[:, cd:2 * cd] * pc[:, 2 * cd:]
    row = lax.broadcasted_iota(jnp.int32, (tm, cd), 0)
    if halo:
        i = pl.program_id(1)
        last = pl.num_programs(1) - 1
        pp = pprev_ref[0, BF16_SUBLANES - 1:BF16_SUBLANES, :].astype(f32)
        pn = pnext_ref[0, 0:1, :].astype(f32)
        z_prev = jnp.where(i > 0, pp[:, cd:2 * cd] * pp[:, 2 * cd:], 0.0)
        z_next = jnp.where(i < last, pn[:, cd:2 * cd] * pn[:, 2 * cd:], 0.0)
    else:
        z_prev = jnp.zeros((1, cd), f32)
        z_next = jnp.zeros((1, cd), f32)
    z_up = jnp.where(row == 0, z_prev, pltpu.roll(z, 1, axis=0))
    z_dn = jnp.where(row == tm - 1, z_next, pltpu.roll(z, tm - 1, axis=0))
    cw = convw_ref[0]
    y = z_up * cw[0:1, :] + z * cw[1:2, :] + z_dn * cw[2:3, :]
    out = out + _dot((pc[:, :cd] * y).astype(bf16), woc_ref[...])

    o_ref[0] = _gated_ffn(x, out, mod, n2g_ref, w1_ref, w3_ref, w2_ref)


def _mix_ffn(x, a, pc, mods, layer, mod_row, conv_w, j, w, n2g, w1, w3, w2, tm):
    b, t, d = x.shape
    hh, hp = MLA_HEADS, HEAD_PAD
    halo = t > tm
    in_specs = [
        pl.BlockSpec((1, tm, d), lambda bi, i: (bi, i, 0)),
        pl.BlockSpec((1, tm, hh * V_HEAD_DIM), lambda bi, i: (bi, i, 0)),
        pl.BlockSpec((1, tm, 3 * CONV_DIM), lambda bi, i: (bi, i, 0)),
    ]
    args = [x, a, pc]
    if halo:
        per = tm // BF16_SUBLANES
        nblk = t // BF16_SUBLANES
        in_specs += [
            pl.BlockSpec((1, BF16_SUBLANES, 3 * CONV_DIM),
                         lambda bi, i: (bi, jnp.maximum(i * per - 1, 0), 0)),
            pl.BlockSpec((1, BF16_SUBLANES, 3 * CONV_DIM),
                         lambda bi, i: (bi, jnp.minimum((i + 1) * per, nblk - 1), 0)),
        ]
        args += [pc, pc]
    in_specs += [
        _mod_spec(layer, mod_row),
        _const_spec(conv_w.shape, j),
        _const_spec(w["w_o_attn"].shape),
        _const_spec(w["w_o_conv"].shape),
        _const_spec((1, d)),
        _const_spec(w1.shape, layer),
        _const_spec(w3.shape, layer),
        _const_spec(w2.shape, layer),
    ]
    args += [mods, conv_w, w["w_o_attn"], w["w_o_conv"], n2g, w1, w3, w2]
    return pl.pallas_call(
        functools.partial(_mix_ffn_body, halo=halo),
        grid=(b, t // tm),
        in_specs=in_specs,
        out_specs=pl.BlockSpec((1, tm, d), lambda bi, i: (bi, i, 0)),
        out_shape=jax.ShapeDtypeStruct((b, t, d), f32),
        compiler_params=_cparams(("arbitrary", "arbitrary")),
        name="mix_ffn_halo" if halo else "mix_ffn",
    )(*args)


SUBLANES = 8


def _dft_a_body(x_ref, mod_ref, n1g_ref, kw_ref, cd_ref, twc_ref, tws_ref, o_ref):
    d = D_MODEL
    gd = FOURIER_GROUP_DIM
    rows = DFT_T1 * SUBLANES
    mod = mod_ref[0, 0]
    halves = [[], []]
    for half in range(x_ref.shape[2] // SUBLANES):
        sub = slice(half * SUBLANES, (half + 1) * SUBLANES)
        x = x_ref[0, :, sub, :].reshape(rows, d)
        h = _modulated_norm(x, mod, n1g_ref, 0).astype(bf16)
        a = _dot(kw_ref[...], h).astype(bf16)
        tw_c = twc_ref[half]
        tw_s = tws_ref[half]
        re_cols, im_cols = [], []
        for g in range(FOURIER_GROUPS):
            sl = slice(g * gd, (g + 1) * gd)
            z = _dot(jnp.concatenate([a[:rows, sl], a[rows:, sl]], axis=1), cd_ref[...])
            z_re, z_im = z[:, :gd], z[:, gd:]
            c = jnp.tile(tw_c, (1, gd // LANES))
            s = jnp.tile(tw_s, (1, gd // LANES))
            re_cols.append(z_re * c + z_im * s)
            im_cols.append(z_im * c - z_re * s)
        halves[0].append(jnp.concatenate(re_cols, axis=1).reshape(DFT_T1, SUBLANES, d))
        halves[1].append(jnp.concatenate(im_cols, axis=1).reshape(DFT_T1, SUBLANES, d))
    for comp in range(2):
        o_ref[0, comp] = jnp.concatenate(halves[comp], axis=1).astype(bf16)


def _dft_a(x, mods, layer, n1g, tabs):
    b, t, d = x.shape
    t2 = t // DFT_T1
    k = BF16_SUBLANES
    rows = DFT_T1 * SUBLANES
    return pl.pallas_call(
        _dft_a_body,
        grid=(b, t2 // k),
        in_specs=[
            pl.BlockSpec((1, DFT_T1, k, d), lambda bi, j: (bi, 0, j, 0)),
            _mod_spec(layer, lambda bi: bi),
            _const_spec((1, d)),
            _const_spec((2 * rows, rows)),
            _const_spec((2 * FOURIER_GROUP_DIM, 2 * FOURIER_GROUP_DIM)),
            pl.BlockSpec((k // SUBLANES, rows, LANES), lambda bi, j: (j, 0, 0)),
            pl.BlockSpec((k // SUBLANES, rows, LANES), lambda bi, j: (j, 0, 0)),
        ],
        out_specs=pl.BlockSpec((1, 2, DFT_T1, k, d), lambda bi, j: (bi, 0, 0, j, 0)),
        out_shape=jax.ShapeDtypeStruct((b, 2, DFT_T1, t2, d), bf16),
        compiler_params=_cparams(("arbitrary", "arbitrary")),
        name="dft_a",
    )(x.reshape(b, DFT_T1, t2, d), mods, n1g, tabs["kw"], tabs["cd"], tabs["tw_cos"], tabs["tw_sin"])


def _dft_ffn_body(x_ref, z_ref, mod_ref, g_ref, wf_ref, n2g_ref, w1_ref, w3_ref, w2_ref, o_ref):
    d = D_MODEL
    tp = x_ref.shape[1]
    rows = tp * SUBLANES
    r0 = pl.multiple_of(pl.program_id(2) * tp, tp)
    cs = g_ref[pl.ds(r0, tp), :]
    t2 = z_ref.shape[3]
    f = jnp.stack([_dot(cs, z_ref[0, :, a].reshape(2 * t2, d)) for a in range(SUBLANES)], axis=0)
    f = jnp.swapaxes(f, 0, 1).reshape(rows, d).astype(bf16)
    out = _dot(f, wf_ref[0])
    x = x_ref[0].reshape(rows, d)
    res = _gated_ffn(x, out, mod_ref[0, 0], n2g_ref, w1_ref, w3_ref, w2_ref)
    o_ref[0] = res.reshape(tp, SUBLANES, d)


def _dft_ffn(x, z, mods, layer, tabs, wf, j, n2g, w1, w3, w2, tp):
    b, t, d = x.shape
    t2 = t // DFT_T1
    out = pl.pallas_call(
        _dft_ffn_body,
        grid=(b, DFT_T1 // SUBLANES, t2 // tp),
        in_specs=[
            pl.BlockSpec((1, tp, SUBLANES, d), lambda bi, jj, hh: (bi, hh, jj, 0)),
            pl.BlockSpec((1, 2, SUBLANES, t2, d), lambda bi, jj, hh: (bi, 0, jj, 0, 0)),
            pl.BlockSpec((1, 1, 1, 6 * d), lambda bi, jj, hh: (layer, bi, 0, 0)),
            _const_spec((t2, 2 * t2)),
            _const_spec(wf.shape, j),
            _const_spec((1, d)),
            _const_spec(w1.shape, layer),
            _const_spec(w3.shape, layer),
            _const_spec(w2.shape, layer),
        ],
        out_specs=pl.BlockSpec((1, tp, SUBLANES, d), lambda bi, jj, hh: (bi, hh, jj, 0)),
        out_shape=jax.ShapeDtypeStruct((b, t2, DFT_T1, d), f32),
        compiler_params=_cparams(("arbitrary", "arbitrary", "arbitrary")),
        name="dft_ffn",
    )(x.reshape(b, t2, DFT_T1, d), z, mods, tabs["g2"], wf, n2g, w1, w3, w2)
    return out.reshape(b, t, d)


def _angle(i, j, n):
    return (2.0 * math.pi / n) * ((i * j) % n).astype(f32)


def _dft_tables(t):
    t1, t2 = DFT_T1, t // DFT_T1
    i1 = jnp.arange(t1, dtype=jnp.int32)
    r = jnp.arange(t1 * SUBLANES, dtype=jnp.int32)
    ang64 = _angle(r[:, None] // SUBLANES, r[None, :] // SUBLANES, t1)
    keep = jnp.where(r[:, None] % SUBLANES == r[None, :] % SUBLANES, t1 ** -0.5, 0.0)
    kw = jnp.concatenate([jnp.cos(ang64) * keep, -jnp.sin(ang64) * keep], axis=0).astype(bf16)

    ic = jnp.arange(FOURIER_GROUP_DIM, dtype=jnp.int32)
    angc = _angle(ic[:, None], ic[None, :], FOURIER_GROUP_DIM)
    sc_ = FOURIER_GROUP_DIM ** -0.5
    cc, sn = jnp.cos(angc) * sc_, jnp.sin(angc) * sc_
    cd = jnp.concatenate([jnp.concatenate([cc, -sn], axis=1),
                          jnp.concatenate([sn, cc], axis=1)], axis=0).astype(bf16)

    i2 = jnp.arange(t2, dtype=jnp.int32)
    angt = _angle(i2[:, None], i1[None, :], t)
    angt = angt.reshape(t2 // SUBLANES, SUBLANES, t1).transpose(0, 2, 1).reshape(t2 // SUBLANES, -1)
    tw = lax.optimization_barrier((jnp.cos(angt), jnp.sin(angt)))
    tw_cos, tw_sin = (jnp.broadcast_to(v[:, :, None], v.shape + (LANES,)) for v in tw)

    ang2 = _angle(i2[:, None], i2[None, :], t2)
    s2 = t2 ** -0.5
    g2 = jnp.concatenate([jnp.cos(ang2) * s2, jnp.sin(ang2) * s2], axis=1).astype(bf16)
    return dict(kw=kw, cd=cd, tw_cos=tw_cos, tw_sin=tw_sin, g2=g2)


def _rope_tables(t):
    rows = t // GRID_W
    half = QK_ROPE_DIM // 2
    inv = 1.0 / (ROPE_BASE ** (jnp.arange(0, half, 2, dtype=f32) / half))
    ar = jnp.arange(rows, dtype=f32)[:, None] * inv
    ac = jnp.arange(GRID_W, dtype=f32)[:, None] * inv
    pad_hi = HEAD_PAD - QK_HEAD_DIM

    def place(n, first, second, at, lead):
        z = jnp.zeros((n, half), f32)
        blocks = [jnp.full((n, QK_NOPE_DIM), lead, f32)]
        blocks += [jnp.concatenate([first, second], axis=1), z] if at == 0 else \
                  [z, jnp.concatenate([first, second], axis=1)]
        blocks.append(jnp.zeros((n, pad_hi), f32))
        return jnp.concatenate(blocks, axis=1)

    rrow = jnp.stack([place(rows, jnp.cos(ar), jnp.cos(ar), 0, 0.0),
                      place(rows, -jnp.sin(ar), jnp.sin(ar), 0, 0.0)])
    rcol = jnp.stack([place(GRID_W, jnp.cos(ac), jnp.cos(ac), 1, 1.0),
                      place(GRID_W, -jnp.sin(ac), jnp.sin(ac), 1, 0.0)])
    return rrow, rcol


def _partner(a, axis):
    parts = jnp.split(a, 4, axis=axis)
    return jnp.concatenate([parts[1], parts[0], parts[3], parts[2]], axis=axis)


def _prep_even(j, w_in, q_norm_g, kv_norm_g, w_uq, w_ukv, q_gain, k_gain, w_o):
    hh, hp = MLA_HEADS, HEAD_PAD
    d = D_MODEL
    nope, hd = QK_NOPE_DIM, QK_HEAD_DIM
    wi = w_in[j]
    w_pe = wi[:, PE_COL:CONV_OFFSET]
    w_in_p = jnp.concatenate([wi[:, :PE_COL], jnp.zeros((d, nope), f32), w_pe, _partner(w_pe, 1),
                              wi[:, CONV_OFFSET:]], axis=1).astype(bf16)
    wq = w_uq[j].reshape(Q_LORA_RANK, hh, hd)
    pad_head = lambda a: jnp.pad(a, ((0, 0), (0, 0), (0, hp - hd))).reshape(Q_LORA_RANK, hh * hp)
    wq_sw = jnp.zeros_like(wq).at[:, :, nope:].set(_partner(wq[:, :, nope:], 2))
    w_uq_p = jnp.concatenate([pad_head(wq), pad_head(wq_sw)], axis=1)
    wkv = w_ukv[j].reshape(KV_LORA_RANK, hh, nope + V_HEAD_DIM)
    wk = jnp.pad(wkv[:, :, :nope], ((0, 0), (0, 0), (0, hp - nope))).reshape(KV_LORA_RANK, hh * hp)
    w_ukv_p = wk.astype(bf16)
    w_vt = wkv[:, :, nope:].reshape(KV_LORA_RANK, hh * V_HEAD_DIM).T.astype(bf16)
    w_uq_p = w_uq_p.astype(bf16)

    def gain_rows(g):
        sw = jnp.zeros_like(g).at[nope:].set(_partner(g[nope:], 0))
        return [jnp.pad(g, (0, hp - hd)), jnp.pad(sw, (0, hp - hd))]

    zero = jnp.zeros((hp,), f32)
    gains = jnp.stack(gain_rows(q_gain[j] * (QK_SCALE * LOG2E)) + gain_rows(k_gain[j]) + [zero] * 4)
    return dict(
        w_in=w_in_p, w_uq=w_uq_p, w_ukv=w_ukv_p, w_vt=w_vt,
        q_norm_g=q_norm_g[j].reshape(1, -1), kv_norm_g=kv_norm_g[j].reshape(1, -1), gains=gains,
        w_o_attn=w_o[j][:hh * V_HEAD_DIM].astype(bf16), w_o_conv=w_o[j][hh * V_HEAD_DIM:].astype(bf16))


def _pick(n, pref):
    return pref if n % pref == 0 else n


def kernel(x, c, ctx, c_ctx, ada_w, ada_b, norm1_g, norm2_g, w_in, q_norm_g, kv_norm_g, w_uq, w_ukv,
           q_gain, k_gain, conv_w, w_o, w_fourier, ffn_w1, ffn_w3, ffn_w2):
    b, s, d = x.shape
    depth = ada_w.shape[0]
    cvec = jnp.zeros((8, d), f32).at[:b].set(c).at[b].set(c_ctx)
    mods = _ada(cvec, ada_w, ada_b).reshape(depth, 8, 1, 6 * d)
    lat_row = lambda bi: bi
    ctx_row = lambda bi: b
    w1 = ffn_w1.astype(bf16)
    w3 = ffn_w3.astype(bf16)
    w2 = ffn_w2.astype(bf16)
    wf = w_fourier.astype(bf16)
    for i in range(depth):
        last = i == depth - 1
        j = i // 2
        n1g = norm1_g[i].reshape(1, d)
        n2g = norm2_g[i].reshape(1, d)
        if i % 2 == 0:
            w = _prep_even(j, w_in, q_norm_g, kv_norm_g, w_uq, w_ukv, q_gain, k_gain, w_o)
            tm = _pick(s, 1024)
            lc = ctx.shape[1]
            q_l, k_l, vt_l, pc_l = _qkv(x, mods, i, lat_row, n1g, w, _rope_tables(s), tm)
            q_c, k_c, vt_c, pc_c = _qkv(ctx, mods, i, ctx_row, n1g, w, None, lc)
            bound = (QK_SCALE * LOG2E * QK_HEAD_DIM * BF16_ROUNDING_SLACK
                     * jnp.max(jnp.abs(q_gain[j])) * jnp.max(jnp.abs(k_gain[j])))
            tq, tk = _pick(s, 2048), _pick(s, 1024)
            a_l = lax.cond(
                bound < EXP2_SAFE_SCORE,
                lambda q_, kl, vtl, kc, vtc: _attention(q_, [(kl, vtl), (kc, vtc)], tq, tk, online=False),
                lambda q_, kl, vtl, kc, vtc: _attention(q_, [(kl, vtl), (kc, vtc)], tm, tk, online=True),
                q_l, k_l, vt_l, k_c, vt_c)
            x_new = _mix_ffn(x, a_l, pc_l, mods, i, lat_row, conv_w, j, w, n2g, w1, w3, w2, tm)
            if not last:
                a_c = _attention(q_c, [(k_c, vt_c)], lc, lc, online=True)
                ctx = _mix_ffn(ctx, a_c, pc_c, mods, i, ctx_row, conv_w, j, w, n2g, w1, w3, w2, lc)
            x = x_new
        else:
            tabs = _dft_tables(s)
            t2 = s // DFT_T1
            z = _dft_a(x, mods, i, n1g, tabs)
            x = _dft_ffn(x, z, mods, i, tabs, wf, j, n2g, w1, w3, w2, _pick(t2, 128))
            assert last, "odd non-final layers are not implemented"
    return x
```

```python
import functools
import math

import jax
import jax.numpy as jnp
from jax import lax
from jax.experimental import pallas as pl
from jax.experimental.pallas import tpu as pltpu

D_MODEL = 1024
GRID_W = 64
MLA_HEADS = 8
QK_NOPE_DIM = 64
QK_ROPE_DIM = 32
QK_HEAD_DIM = QK_NOPE_DIM + QK_ROPE_DIM
V_HEAD_DIM = 64
Q_LORA_RANK = 384
KV_LORA_RANK = 256
QK_SCALE = QK_HEAD_DIM ** -0.5
ROPE_BASE = 10000.0
CONV_DIM = 512
CONV_OFFSET = Q_LORA_RANK + KV_LORA_RANK + QK_ROPE_DIM
FOURIER_GROUPS = 4
FOURIER_GROUP_DIM = D_MODEL // FOURIER_GROUPS
EPS = 1e-6

LANES = 128
BF16_SUBLANES = 16
MXU_TILE = 256
VMEM_LIMIT_BYTES = 60 * 1024 * 1024

HEAD_PAD = LANES
ALL_HEADS = MLA_HEADS * HEAD_PAD
PE_COL = Q_LORA_RANK + KV_LORA_RANK
CONV_COL = PE_COL + HEAD_PAD
IN_PROJ_PAD = CONV_COL + 3 * CONV_DIM
QKV_ROW_BLOCK = 128
QKV_SUB_ROWS = 256
DFT_T1 = 64
LOG2E = math.log2(math.e)
EXP2_SAFE_SCORE = 64.0
BF16_ROUNDING_SLACK = 1.02
NT_DIMS = (((1,), (1,)), ((), ()))
V_T_ROWS = V_HEAD_DIM + BF16_SUBLANES

bf16 = jnp.bfloat16
f32 = jnp.float32


def _cparams(sem):
    return pltpu.CompilerParams(dimension_semantics=sem, vmem_limit_bytes=VMEM_LIMIT_BYTES)


def _const_spec(shape, lead=None):
    if lead is None:
        nd = len(shape)
        return pl.BlockSpec(shape, lambda *_: (0,) * nd, pipeline_mode=pl.Buffered(1))
    nd = len(shape) - 1
    return pl.BlockSpec((1,) + tuple(shape[1:]), lambda *_: (lead,) + (0,) * nd,
                        pipeline_mode=pl.Buffered(1))


def _mod_spec(layer, mod_row):
    return pl.BlockSpec((1, 1, 1, 6 * D_MODEL), lambda bi, i: (layer, mod_row(bi), 0, 0))


def _dot(a, b):
    return jnp.dot(a, b, preferred_element_type=f32)


def _rms_scale(x, width):
    return lax.rsqrt(jnp.sum(x * x, axis=-1, keepdims=True) * (1.0 / width) + EPS)


def _ada_body(c_ref, w_ref, b_ref, o_ref):
    c = c_ref[...]
    s = c / (1.0 + jnp.exp(-c))
    o_ref[0] = _dot(s.astype(bf16), w_ref[0].astype(bf16)) + b_ref[0]


def _ada(cvec, ada_w, ada_b):
    depth, d, n = ada_w.shape
    tn = 1536
    return pl.pallas_call(
        _ada_body,
        grid=(depth, n // tn),
        in_specs=[
            pl.BlockSpec((8, d), lambda l, j: (0, 0)),
            pl.BlockSpec((1, d, tn), lambda l, j: (l, 0, j)),
            pl.BlockSpec((1, 1, tn), lambda l, j: (l, 0, j)),
        ],
        out_specs=pl.BlockSpec((1, 8, tn), lambda l, j: (l, 0, j)),
        out_shape=jax.ShapeDtypeStruct((depth, 8, n), f32),
        compiler_params=_cparams(("arbitrary", "arbitrary")),
        name="ada",
    )(cvec, ada_w, ada_b.reshape(depth, 1, n))


def _modulated_norm(x, mod, g_ref, lo):
    d = D_MODEL
    shift = mod[:, lo:lo + d]
    scale = mod[:, lo + d:lo + 2 * d]
    gain = g_ref[...] * (1.0 + scale)
    return (x * _rms_scale(x, d)) * gain + shift


def _qkv_body(*refs, rope):
    (x_ref, mod_ref, n1g_ref, win_ref, qng_ref, kvng_ref, wuq_ref, wukv_ref, wvt_ref,
     gains_ref) = refs[:10]
    if rope:
        rrow_ref, rcol_ref = refs[10:12]
        refs = refs[12:]
    else:
        refs = refs[10:]
    q_out, k_out, vt_out, pc_out, q_scr, kv_scr, pe_scr = refs
    tm = x_ref.shape[1]
    mod = mod_ref[0, 0]
    gains = gains_ref[...]
    lane = lax.broadcasted_iota(jnp.int32, (1, HEAD_PAD), 1)
    real = (lane < QK_HEAD_DIM).astype(f32)
    rb = min(QKV_ROW_BLOCK, tm)
    per = rb // GRID_W
    grid_row0 = pl.program_id(1) * (tm // GRID_W)
    to_rope = HEAD_PAD - QK_ROPE_DIM
    sub = min(QKV_SUB_ROWS, tm)

    def qk_scale(t):
        return lax.rsqrt(jnp.sum(t * t * real, axis=-1, keepdims=True) * (1.0 / QK_HEAD_DIM) + EPS)

    def project(lo):
        rows = slice(lo, lo + sub)
        h = _modulated_norm(x_ref[0, rows, :], mod, n1g_ref, 0)
        p = _dot(h.astype(bf16), win_ref[...])
        gate_b = p[:, CONV_COL:CONV_COL + CONV_DIM]
        z = p[:, CONV_COL + CONV_DIM:CONV_COL + 2 * CONV_DIM] * p[:, CONV_COL + 2 * CONV_DIM:]
        pc_out[0, rows, :] = jnp.concatenate([gate_b, z], axis=1).astype(bf16)
        pe_scr[rows, :] = p[:, PE_COL:CONV_COL]
        cq = p[:, :Q_LORA_RANK]
        cqn = cq * _rms_scale(cq, Q_LORA_RANK) * qng_ref[...]
        q_scr[rows, :] = _dot(cqn.astype(bf16), wuq_ref[...])
        ckv = p[:, Q_LORA_RANK:PE_COL]
        ckvn = (ckv * _rms_scale(ckv, KV_LORA_RANK) * kvng_ref[...]).astype(bf16)
        kv_scr[rows, :] = _dot(ckvn, wukv_ref[...])
        vt = lax.dot_general(wvt_ref[...], ckvn, NT_DIMS, preferred_element_type=f32)
        ones_rows = (lax.broadcasted_iota(jnp.int32, (V_T_ROWS - V_HEAD_DIM, sub), 0) == 0).astype(f32)
        for hd in range(MLA_HEADS):
            vt_h = vt[hd * V_HEAD_DIM:(hd + 1) * V_HEAD_DIM]
            vt_out[0, hd, :, rows] = jnp.concatenate([vt_h, ones_rows], axis=0).astype(bf16)

    def finish(ib):
        rows = slice(ib * rb, (ib + 1) * rb)
        if rope:
            tabs = []
            for comp in range(2):
                parts = []
                for g in range(per):
                    rr = rrow_ref[comp, pl.ds(grid_row0 + ib * per + g, 1), :]
                    parts.append(rcol_ref[comp] + rr)
                tabs.append(jnp.concatenate(parts, axis=0))
            cos, sin = tabs
            qa, qb = cos * gains[0:1], sin * gains[1:2]
            ka, kb = cos * gains[2:3], sin * gains[3:4]
        pe = pe_scr[rows, :]
        if rope:
            pe_rot = pltpu.roll(pe, to_rope, axis=1) * kb
        for hd in range(MLA_HEADS):
            sl = slice(hd * HEAD_PAD, (hd + 1) * HEAD_PAD)
            qh = q_scr[rows, sl]
            kh = kv_scr[rows, sl] + pe
            rq = qk_scale(qh)
            rk = qk_scale(kh)
            if rope:
                partner = slice(ALL_HEADS + hd * HEAD_PAD, ALL_HEADS + (hd + 1) * HEAD_PAD)
                qo = (qh * qa + q_scr[rows, partner] * qb) * rq
                ko = (kh * ka + pe_rot) * rk
            else:
                qo = qh * gains[0:1] * rq
                ko = kh * gains[2:3] * rk
            q_out[0, hd, rows, :] = qo.astype(bf16)
            k_out[0, hd, rows, :] = ko.astype(bf16)

    for st in range(tm // sub):
        project(st * sub)
        for ib in range(st * sub // rb, (st + 1) * sub // rb):
            finish(ib)


def _qkv(x, mods, layer, mod_row, n1g, w, rope_tabs, tm):
    b, t, d = x.shape
    hh, hp = MLA_HEADS, HEAD_PAD
    rope = rope_tabs is not None
    in_specs = [
        pl.BlockSpec((1, tm, d), lambda bi, i: (bi, i, 0)),
        _mod_spec(layer, mod_row),
        _const_spec((1, d)),
        _const_spec(w["w_in"].shape),
        _const_spec((1, Q_LORA_RANK)),
        _const_spec((1, KV_LORA_RANK)),
        _const_spec(w["w_uq"].shape),
        _const_spec(w["w_ukv"].shape),
        _const_spec(w["w_vt"].shape),
        _const_spec((8, hp)),
    ]
    args = [x, mods, n1g, w["w_in"], w["q_norm_g"], w["kv_norm_g"], w["w_uq"], w["w_ukv"], w["w_vt"],
            w["gains"]]
    if rope:
        rrow, rcol = rope_tabs
        in_specs += [_const_spec(rrow.shape), _const_spec(rcol.shape)]
        args += [rrow, rcol]
    head_spec = pl.BlockSpec((1, hh, tm, hp), lambda bi, i: (bi, 0, i, 0))
    head_shape = jax.ShapeDtypeStruct((b, hh, t, hp), bf16)
    return pl.pallas_call(
        functools.partial(_qkv_body, rope=rope),
        grid=(b, t // tm),
        in_specs=in_specs,
        out_specs=[head_spec, head_spec,
                   pl.BlockSpec((1, hh, V_T_ROWS, tm), lambda bi, i: (bi, 0, 0, i)),
                   pl.BlockSpec((1, tm, 2 * CONV_DIM), lambda bi, i: (bi, i, 0))],
        out_shape=[head_shape, head_shape,
                   jax.ShapeDtypeStruct((b, hh, V_T_ROWS, t), bf16),
                   jax.ShapeDtypeStruct((b, t, 2 * CONV_DIM), bf16)],
        scratch_shapes=[pltpu.VMEM((tm, 2 * ALL_HEADS), f32),
                        pltpu.VMEM((tm, ALL_HEADS), f32),
                        pltpu.VMEM((tm, hp), f32)],
        compiler_params=_cparams(("arbitrary", "arbitrary")),
        name="qkv_rope" if rope else "qkv_ctx",
    )(*args)


def _attn_body(q_ref, *refs, chunks, online):
    o_ref = refs[-1]
    k_refs, vt_refs = refs[0:-1:2], refs[1:-1:2]
    outs = []
    for par in range(2):
        qt = q_ref[0, par].astype(f32).T.astype(bf16)
        m = None
        acc = None
        for src, (n_chunks, tk) in enumerate(chunks):
            for c in range(n_chunks):
                ks = slice(c * tk, (c + 1) * tk)
                st = _dot(k_refs[src][0, par, ks, :], qt)
                if online:
                    m_cur = jnp.max(st, axis=0, keepdims=True)
                    m_new = m_cur if m is None else jnp.maximum(m, m_cur)
                    part = _dot(vt_refs[src][0, par, :, ks], jnp.exp2(st - m_new).astype(bf16))
                    acc = part if m is None else jnp.exp2(m - m_new) * acc + part
                    m = m_new
                else:
                    part = _dot(vt_refs[src][0, par, :, ks], jnp.exp2(st).astype(bf16))
                    acc = part if acc is None else acc + part
        outs.append(acc[:V_HEAD_DIM] / acc[V_HEAD_DIM:V_HEAD_DIM + 1])
    o_ref[0] = jnp.concatenate(outs, axis=0).T.astype(bf16)


def _attention(q, kv_sources, tq, tk, online):
    b, hh, t, hp = q.shape
    in_specs = [pl.BlockSpec((1, 2, tq, hp), lambda bi, pi, i: (bi, pi, i, 0))]
    args = [q]
    chunks = []
    for k, vt in kv_sources:
        n = k.shape[2]
        step = min(tk, n)
        chunks.append((n // step, step))
        in_specs += [pl.BlockSpec((1, 2, n, hp), lambda bi, pi, i: (bi, pi, 0, 0)),
                     pl.BlockSpec((1, 2, V_T_ROWS, n), lambda bi, pi, i: (bi, pi, 0, 0))]
        args += [k, vt]
    return pl.pallas_call(
        functools.partial(_attn_body, chunks=tuple(chunks), online=online),
        grid=(b, hh // 2, t // tq),
        in_specs=in_specs,
        out_specs=pl.BlockSpec((1, tq, hp), lambda bi, pi, i: (bi, i, pi)),
        out_shape=jax.ShapeDtypeStruct((b, t, hh * V_HEAD_DIM), bf16),
        compiler_params=_cparams(("arbitrary", "arbitrary", "arbitrary")),
        name="attn%d%s" % (len(kv_sources), "_online" if online else "_bounded"),
    )(*args)


def _gated_ffn(x, out, mod, n2g_ref, w1_ref, w3_ref, w2_ref):
    d = D_MODEL
    x1 = x + mod[:, 2 * d:3 * d] * out
    h2 = _modulated_norm(x1, mod, n2g_ref, 3 * d).astype(bf16)
    y = None
    for lo, hi in _ff_chunks(w1_ref.shape[2]):
        sl = slice(lo, hi)
        u = _dot(h2, w1_ref[0, :, sl])
        g = _dot(h2, w3_ref[0, :, sl])
        act = (u / (1.0 + jnp.exp(-u)) * g).astype(bf16)
        part = _dot(act, w2_ref[0, sl, :])
        y = part if y is None else y + part
    return x1 + mod[:, 5 * d:6 * d] * y


FFN_CHUNKS = 4


def _ff_chunks(d_ff):
    step = -(-d_ff // (FFN_CHUNKS * MXU_TILE)) * MXU_TILE
    return [(lo, min(lo + step, d_ff)) for lo in range(0, d_ff, step)]


def _mix_ffn_body(*refs, halo):
    x_ref, a_ref, pc_ref = refs[:3]
    if halo:
        pprev_ref, pnext_ref = refs[3:5]
        refs = refs[5:]
    else:
        refs = refs[3:]
    (mod_ref, convw_ref, woa_ref, woc_ref, n2g_ref, w1_ref, w3_ref, w2_ref, o_ref) = refs
    x = x_ref[0]
    mod = mod_ref[0, 0]
    tm = x.shape[0]

    out = _dot(a_ref[0], woa_ref[...])

    cd = CONV_DIM
    pc = pc_ref[0].astype(f32)
    z = pc[:, cd:]
    row = lax.broadcasted_iota(jnp.int32, (tm, cd), 0)
    if halo:
        i = pl.program_id(1)
        last = pl.num_programs(1) - 1
        pp = pprev_ref[0, BF16_SUBLANES - 1:BF16_SUBLANES, :].astype(f32)
        pn = pnext_ref[0, 0:1, :].astype(f32)
        z_prev = jnp.where(i > 0, pp[:, cd:], 0.0)
        z_next = jnp.where(i < last, pn[:, cd:], 0.0)
    else:
        z_prev = jnp.zeros((1, cd), f32)
        z_next = jnp.zeros((1, cd), f32)
    z_up = jnp.where(row == 0, z_prev, pltpu.roll(z, 1, axis=0))
    z_dn = jnp.where(row == tm - 1, z_next, pltpu.roll(z, tm - 1, axis=0))
    cw = convw_ref[0]
    y = z_up * cw[0:1, :] + z * cw[1:2, :] + z_dn * cw[2:3, :]
    out = out + _dot((pc[:, :cd] * y).astype(bf16), woc_ref[...])

    o_ref[0] = _gated_ffn(x, out, mod, n2g_ref, w1_ref, w3_ref, w2_ref)


def _mix_ffn(x, a, pc, mods, layer, mod_row, conv_w, j, w, n2g, w1, w3, w2, tm):
    b, t, d = x.shape
    hh, hp = MLA_HEADS, HEAD_PAD
    halo = t > tm
    in_specs = [
        pl.BlockSpec((1, tm, d), lambda bi, i: (bi, i, 0)),
        pl.BlockSpec((1, tm, hh * V_HEAD_DIM), lambda bi, i: (bi, i, 0)),
        pl.BlockSpec((1, tm, 2 * CONV_DIM), lambda bi, i: (bi, i, 0)),
    ]
    args = [x, a, pc]
    if halo:
        per = tm // BF16_SUBLANES
        nblk = t // BF16_SUBLANES
        in_specs += [
            pl.BlockSpec((1, BF16_SUBLANES, 2 * CONV_DIM),
                         lambda bi, i: (bi, jnp.maximum(i * per - 1, 0), 0)),
            pl.BlockSpec((1, BF16_SUBLANES, 2 * CONV_DIM),
                         lambda bi, i: (bi, jnp.minimum((i + 1) * per, nblk - 1), 0)),
        ]
        args += [pc, pc]
    in_specs += [
        _mod_spec(layer, mod_row),
        _const_spec(conv_w.shape, j),
        _const_spec(w["w_o_attn"].shape),
        _const_spec(w["w_o_conv"].shape),
        _const_spec((1, d)),
        _const_spec(w1.shape, layer),
        _const_spec(w3.shape, layer),
        _const_spec(w2.shape, layer),
    ]
    args += [mods, conv_w, w["w_o_attn"], w["w_o_conv"], n2g, w1, w3, w2]
    return pl.pallas_call(
        functools.partial(_mix_ffn_body, halo=halo),
        grid=(b, t // tm),
        in_specs=in_specs,
        out_specs=pl.BlockSpec((1, tm, d), lambda bi, i: (bi, i, 0)),
        out_shape=jax.ShapeDtypeStruct((b, t, d), f32),
        compiler_params=_cparams(("arbitrary", "arbitrary")),
        name="mix_ffn_halo" if halo else "mix_ffn",
    )(*args)


SUBLANES = 8


def _dft_a_body(x_ref, mod_ref, n1g_ref, kw_ref, cd_ref, twc_ref, tws_ref, o_ref):
    d = D_MODEL
    gd = FOURIER_GROUP_DIM
    rows = DFT_T1 * SUBLANES
    mod = mod_ref[0, 0]
    halves = [[], []]
    for half in range(x_ref.shape[2] // SUBLANES):
        sub = slice(half * SUBLANES, (half + 1) * SUBLANES)
        x = x_ref[0, :, sub, :].reshape(rows, d)
        h = _modulated_norm(x, mod, n1g_ref, 0).astype(bf16)
        a = _dot(kw_ref[...], h).astype(bf16)
        tw_c = twc_ref[half]
        tw_s = tws_ref[half]
        re_cols, im_cols = [], []
        for g in range(FOURIER_GROUPS):
            sl = slice(g * gd, (g + 1) * gd)
            z = _dot(jnp.concatenate([a[:rows, sl], a[rows:, sl]], axis=1), cd_ref[...])
            z_re, z_im = z[:, :gd], z[:, gd:]
            c = jnp.tile(tw_c, (1, gd // LANES))
            s = jnp.tile(tw_s, (1, gd // LANES))
            re_cols.append(z_re * c + z_im * s)
            im_cols.append(z_im * c - z_re * s)
        halves[0].append(jnp.concatenate(re_cols, axis=1).reshape(DFT_T1, SUBLANES, d))
        halves[1].append(jnp.concatenate(im_cols, axis=1).reshape(DFT_T1, SUBLANES, d))
    for comp in range(2):
        o_ref[0, comp] = jnp.concatenate(halves[comp], axis=1).astype(bf16)


def _dft_a(x, mods, layer, n1g, tabs):
    b, t, d = x.shape
    t2 = t // DFT_T1
    k = BF16_SUBLANES
    rows = DFT_T1 * SUBLANES
    return pl.pallas_call(
        _dft_a_body,
        grid=(b, t2 // k),
        in_specs=[
            pl.BlockSpec((1, DFT_T1, k, d), lambda bi, j: (bi, 0, j, 0)),
            _mod_spec(layer, lambda bi: bi),
            _const_spec((1, d)),
            _const_spec((2 * rows, rows)),
            _const_spec((2 * FOURIER_GROUP_DIM, 2 * FOURIER_GROUP_DIM)),
            pl.BlockSpec((k // SUBLANES, rows, LANES), lambda bi, j: (j, 0, 0)),
            pl.BlockSpec((k // SUBLANES, rows, LANES), lambda bi, j: (j, 0, 0)),
        ],
        out_specs=pl.BlockSpec((1, 2, DFT_T1, k, d), lambda bi, j: (bi, 0, 0, j, 0)),
        out_shape=jax.ShapeDtypeStruct((b, 2, DFT_T1, t2, d), bf16),
        compiler_params=_cparams(("arbitrary", "arbitrary")),
        name="dft_a",
    )(x.reshape(b, DFT_T1, t2, d), mods, n1g, tabs["kw"], tabs["cd"], tabs["tw_cos"], tabs["tw_sin"])


def _dft_ffn_body(x_ref, z_ref, mod_ref, g_ref, wf_ref, n2g_ref, w1_ref, w3_ref, w2_ref, o_ref):
    d = D_MODEL
    tp = x_ref.shape[1]
    rows = tp * SUBLANES
    r0 = pl.multiple_of(pl.program_id(2) * tp, tp)
    cs = g_ref[pl.ds(r0, tp), :]
    t2 = z_ref.shape[3]
    f = jnp.stack([_dot(cs, z_ref[0, :, a].reshape(2 * t2, d)) for a in range(SUBLANES)], axis=0)
    f = jnp.swapaxes(f, 0, 1).reshape(rows, d).astype(bf16)
    out = _dot(f, wf_ref[0])
    x = x_ref[0].reshape(rows, d)
    res = _gated_ffn(x, out, mod_ref[0, 0], n2g_ref, w1_ref, w3_ref, w2_ref)
    o_ref[0] = res.reshape(tp, SUBLANES, d)


def _dft_ffn(x, z, mods, layer, tabs, wf, j, n2g, w1, w3, w2, tp):
    b, t, d = x.shape
    t2 = t // DFT_T1
    out = pl.pallas_call(
        _dft_ffn_body,
        grid=(b, DFT_T1 // SUBLANES, t2 // tp),
        in_specs=[
            pl.BlockSpec((1, tp, SUBLANES, d), lambda bi, jj, hh: (bi, hh, jj, 0)),
            pl.BlockSpec((1, 2, SUBLANES, t2, d), lambda bi, jj, hh: (bi, 0, jj, 0, 0)),
            pl.BlockSpec((1, 1, 1, 6 * d), lambda bi, jj, hh: (layer, bi, 0, 0)),
            _const_spec((t2, 2 * t2)),
            _const_spec(wf.shape, j),
            _const_spec((1, d)),
            _const_spec(w1.shape, layer),
            _const_spec(w3.shape, layer),
            _const_spec(w2.shape, layer),
        ],
        out_specs=pl.BlockSpec((1, tp, SUBLANES, d), lambda bi, jj, hh: (bi, hh, jj, 0)),
        out_shape=jax.ShapeDtypeStruct((b, t2, DFT_T1, d), f32),
        compiler_params=_cparams(("arbitrary", "arbitrary", "arbitrary")),
        name="dft_ffn",
    )(x.reshape(b, t2, DFT_T1, d), z, mods, tabs["g2"], wf, n2g, w1, w3, w2)
    return out.reshape(b, t, d)


def _angle(i, j, n):
    return (2.0 * math.pi / n) * ((i * j) % n).astype(f32)


def _dft_tables(t):
    t1, t2 = DFT_T1, t // DFT_T1
    i1 = jnp.arange(t1, dtype=jnp.int32)
    r = jnp.arange(t1 * SUBLANES, dtype=jnp.int32)
    ang64 = _angle(r[:, None] // SUBLANES, r[None, :] // SUBLANES, t1)
    keep = jnp.where(r[:, None] % SUBLANES == r[None, :] % SUBLANES, t1 ** -0.5, 0.0)
    kw = jnp.concatenate([jnp.cos(ang64) * keep, -jnp.sin(ang64) * keep], axis=0).astype(bf16)

    ic = jnp.arange(FOURIER_GROUP_DIM, dtype=jnp.int32)
    angc = _angle(ic[:, None], ic[None, :], FOURIER_GROUP_DIM)
    sc_ = FOURIER_GROUP_DIM ** -0.5
    cc, sn = jnp.cos(angc) * sc_, jnp.sin(angc) * sc_
    cd = jnp.concatenate([jnp.concatenate([cc, -sn], axis=1),
                          jnp.concatenate([sn, cc], axis=1)], axis=0).astype(bf16)

    i2 = jnp.arange(t2, dtype=jnp.int32)
    angt = _angle(i2[:, None], i1[None, :], t)
    angt = angt.reshape(t2 // SUBLANES, SUBLANES, t1).transpose(0, 2, 1).reshape(t2 // SUBLANES, -1)
    tw = lax.optimization_barrier((jnp.cos(angt), jnp.sin(angt)))
    tw_cos, tw_sin = (jnp.broadcast_to(v[:, :, None], v.shape + (LANES,)) for v in tw)

    ang2 = _angle(i2[:, None], i2[None, :], t2)
    s2 = t2 ** -0.5
    g2 = jnp.concatenate([jnp.cos(ang2) * s2, jnp.sin(ang2) * s2], axis=1).astype(bf16)
    return dict(kw=kw, cd=cd, tw_cos=tw_cos, tw_sin=tw_sin, g2=g2)


def _rope_tables(t):
    rows = t // GRID_W
    half = QK_ROPE_DIM // 2
    inv = 1.0 / (ROPE_BASE ** (jnp.arange(0, half, 2, dtype=f32) / half))
    ar = jnp.arange(rows, dtype=f32)[:, None] * inv
    ac = jnp.arange(GRID_W, dtype=f32)[:, None] * inv
    pad_hi = HEAD_PAD - QK_HEAD_DIM

    def place(n, first, second, at, lead):
        z = jnp.zeros((n, half), f32)
        blocks = [jnp.full((n, QK_NOPE_DIM), lead, f32)]
        blocks += [jnp.concatenate([first, second], axis=1), z] if at == 0 else \
                  [z, jnp.concatenate([first, second], axis=1)]
        blocks.append(jnp.zeros((n, pad_hi), f32))
        return jnp.concatenate(blocks, axis=1)

    rrow = jnp.stack([place(rows, jnp.cos(ar), jnp.cos(ar), 0, 0.0),
                      place(rows, -jnp.sin(ar), jnp.sin(ar), 0, 0.0)])
    rcol = jnp.stack([place(GRID_W, jnp.cos(ac), jnp.cos(ac), 1, 1.0),
                      place(GRID_W, -jnp.sin(ac), jnp.sin(ac), 1, 0.0)])
    return rrow, rcol


def _partner(a, axis):
    parts = jnp.split(a, 4, axis=axis)
    return jnp.concatenate([parts[1], parts[0], parts[3], parts[2]], axis=axis)


def _prep_even(j, w_in, q_norm_g, kv_norm_g, w_uq, w_ukv, q_gain, k_gain, w_o):
    hh, hp = MLA_HEADS, HEAD_PAD
    d = D_MODEL
    nope, hd = QK_NOPE_DIM, QK_HEAD_DIM
    wi = w_in[j]
    w_pe = wi[:, PE_COL:CONV_OFFSET]
    w_in_p = jnp.concatenate([wi[:, :PE_COL], jnp.zeros((d, nope), f32), w_pe, _partner(w_pe, 1),
                              wi[:, CONV_OFFSET:]], axis=1).astype(bf16)
    wq = w_uq[j].reshape(Q_LORA_RANK, hh, hd)
    pad_head = lambda a: jnp.pad(a, ((0, 0), (0, 0), (0, hp - hd))).reshape(Q_LORA_RANK, hh * hp)
    wq_sw = jnp.zeros_like(wq).at[:, :, nope:].set(_partner(wq[:, :, nope:], 2))
    w_uq_p = jnp.concatenate([pad_head(wq), pad_head(wq_sw)], axis=1)
    wkv = w_ukv[j].reshape(KV_LORA_RANK, hh, nope + V_HEAD_DIM)
    wk = jnp.pad(wkv[:, :, :nope], ((0, 0), (0, 0), (0, hp - nope))).reshape(KV_LORA_RANK, hh * hp)
    w_ukv_p = wk.astype(bf16)
    w_vt = wkv[:, :, nope:].reshape(KV_LORA_RANK, hh * V_HEAD_DIM).T.astype(bf16)
    w_uq_p = w_uq_p.astype(bf16)

    def gain_rows(g):
        sw = jnp.zeros_like(g).at[nope:].set(_partner(g[nope:], 0))
        return [jnp.pad(g, (0, hp - hd)), jnp.pad(sw, (0, hp - hd))]

    zero = jnp.zeros((hp,), f32)
    gains = jnp.stack(gain_rows(q_gain[j] * (QK_SCALE * LOG2E)) + gain_rows(k_gain[j]) + [zero] * 4)
    return dict(
        w_in=w_in_p, w_uq=w_uq_p, w_ukv=w_ukv_p, w_vt=w_vt,
        q_norm_g=q_norm_g[j].reshape(1, -1), kv_norm_g=kv_norm_g[j].reshape(1, -1), gains=gains,
        w_o_attn=w_o[j][:hh * V_HEAD_DIM].astype(bf16), w_o_conv=w_o[j][hh * V_HEAD_DIM:].astype(bf16))


def _pick(n, pref):
    return pref if n % pref == 0 else n


def kernel(x, c, ctx, c_ctx, ada_w, ada_b, norm1_g, norm2_g, w_in, q_norm_g, kv_norm_g, w_uq, w_ukv,
           q_gain, k_gain, conv_w, w_o, w_fourier, ffn_w1, ffn_w3, ffn_w2):
    b, s, d = x.shape
    depth = ada_w.shape[0]
    cvec = jnp.zeros((8, d), f32).at[:b].set(c).at[b].set(c_ctx)
    mods = _ada(cvec, ada_w, ada_b).reshape(depth, 8, 1, 6 * d)
    lat_row = lambda bi: bi
    ctx_row = lambda bi: b
    w1 = ffn_w1.astype(bf16)
    w3 = ffn_w3.astype(bf16)
    w2 = ffn_w2.astype(bf16)
    wf = w_fourier.astype(bf16)
    for i in range(depth):
        last = i == depth - 1
        j = i // 2
        n1g = norm1_g[i].reshape(1, d)
        n2g = norm2_g[i].reshape(1, d)
        if i % 2 == 0:
            w = _prep_even(j, w_in, q_norm_g, kv_norm_g, w_uq, w_ukv, q_gain, k_gain, w_o)
            tm = _pick(s, 1024)
            lc = ctx.shape[1]
            q_l, k_l, vt_l, pc_l = _qkv(x, mods, i, lat_row, n1g, w, _rope_tables(s), tm)
            q_c, k_c, vt_c, pc_c = _qkv(ctx, mods, i, ctx_row, n1g, w, None, lc)
            bound = (QK_SCALE * LOG2E * QK_HEAD_DIM * BF16_ROUNDING_SLACK
                     * jnp.max(jnp.abs(q_gain[j])) * jnp.max(jnp.abs(k_gain[j])))
            tq, tk = _pick(s, 2048), _pick(s, 1024)
            a_l = lax.cond(
                bound < EXP2_SAFE_SCORE,
                lambda q_, kl, vtl, kc, vtc: _attention(q_, [(kl, vtl), (kc, vtc)], tq, tk, online=False),
                lambda q_, kl, vtl, kc, vtc: _attention(q_, [(kl, vtl), (kc, vtc)], tm, tk, online=True),
                q_l, k_l, vt_l, k_c, vt_c)
            x_new = _mix_ffn(x, a_l, pc_l, mods, i, lat_row, conv_w, j, w, n2g, w1, w3, w2, tm)
            if not last:
                a_c = _attention(q_c, [(k_c, vt_c)], lc, lc, online=True)
                ctx = _mix_ffn(ctx, a_c, pc_c, mods, i, ctx_row, conv_w, j, w, n2g, w1, w3, w2, lc)
            x = x_new
        else:
            tabs = _dft_tables(s)
            t2 = s // DFT_T1
            z = _dft_a(x, mods, i, n1g, tabs)
            x = _dft_ffn(x, z, mods, i, tabs, wf, j, n2g, w1, w3, w2, _pick(t2, 128))
            assert last, "odd non-final layers are not implemented"
    return x
```
